```python
import math
import jax, jax.numpy as jnp
from jax import lax
import numpy as np

D_MODEL = 1024
BATCH = 4
SEQ = 4096
DEPTH = 2

GRID_W = 64
CTX_LEN = 256
N_EVEN = (DEPTH + 1) // 2
N_ODD = DEPTH // 2
N_MOD = 6

N_Q_HEADS = 8
N_KV_HEADS = 2
Q_PER_KV = N_Q_HEADS // N_KV_HEADS
HEAD_DIM = 64
ATTN_WIDTH = N_Q_HEADS * HEAD_DIM
KV_WIDTH = N_KV_HEADS * HEAD_DIM
FOURIER_GROUPS = 8
FOURIER_GROUP_W = 64
FOURIER_WIDTH = FOURIER_GROUPS * FOURIER_GROUP_W
KV_START = FOURIER_WIDTH + ATTN_WIDTH
MIX_IN_WIDTH = KV_START + 2 * KV_WIDTH
MIX_OUT_WIDTH = FOURIER_WIDTH + ATTN_WIDTH
WINDOW = 128
BLOCK = 128
ATTN_SCALE = HEAD_DIM ** -0.5
ROPE_BASE = 10000.0
NEG_INF = -1e30

SSM_GROUP_W = 16
SSM_GROUPS = D_MODEL // SSM_GROUP_W
SSM_STATE = 64
DT_MIN = 0.001
DT_MAX = 0.1

FFN_MULT = 256
FFN_HIDDEN = -(-8 * D_MODEL // (3 * FFN_MULT)) * FFN_MULT
RMS_EPS = 1e-6

kernel_name = 'hybrid_fourier_swa_s5_dit_trunk'


def rms_norm(x, g):
    x32 = x.astype(jnp.float32)
    y = x32 * lax.rsqrt(jnp.mean(x32 * x32, axis=-1, keepdims=True) + RMS_EPS)
    return (y * g.astype(jnp.float32)).astype(x.dtype)


def adaln(cond, w, b):
    return jnp.split(jax.nn.silu(cond) @ w + b, N_MOD, axis=-1)


def swiglu(h, w_gate, w_up, w_down):
    return (jax.nn.silu(h @ w_gate) * (h @ w_up)) @ w_down


def axial_rope(rows):
    t_row = jnp.repeat(jnp.arange(rows, dtype=jnp.float32), GRID_W)
    t_col = jnp.tile(jnp.arange(GRID_W, dtype=jnp.float32), rows)
    n_freq = HEAD_DIM // 4
    inv_freq = ROPE_BASE ** (-jnp.arange(n_freq, dtype=jnp.float32) / n_freq)
    ang = jnp.concatenate([t_row[:, None] * inv_freq, t_col[:, None] * inv_freq], axis=-1)
    return jnp.cos(ang), jnp.sin(ang)


def apply_rope(x, cos, sin):
    x32 = x.astype(jnp.float32)
    x1, x2 = x32[..., 0::2], x32[..., 1::2]
    cs, sn = cos[None, :, None, :], sin[None, :, None, :]
    out = jnp.stack([x1 * cs - x2 * sn, x1 * sn + x2 * cs], axis=-1).reshape(x.shape)
    return out.astype(x.dtype)


def fourier_mix(f):
    bn, t, _ = f.shape
    g = f.astype(jnp.float32).reshape(bn, t, FOURIER_GROUPS, FOURIER_GROUP_W)
    y = jnp.fft.fft2(g, axes=(1, 3), norm='ortho').real
    return y.reshape(bn, t, FOURIER_WIDTH).astype(f.dtype)


def sink_column(sink, lead_shape):
    s = sink.astype(jnp.float32).reshape(N_KV_HEADS, Q_PER_KV, 1, 1)
    return jnp.broadcast_to(s, lead_shape + (1,))


def window_attention(q, k, v, kc, vc, sink):
    bn, n_tok = q.shape[:2]
    n_ctx = kc.shape[1]
    nb = n_tok // BLOCK
    qb = q.reshape(bn, nb, BLOCK, N_KV_HEADS, Q_PER_KV, HEAD_DIM)

    def bands(t):
        tp = jnp.pad(t, ((0, 0), (BLOCK, BLOCK), (0, 0), (0, 0)))
        tp = tp.reshape(bn, nb + 2, BLOCK, N_KV_HEADS, HEAD_DIM)
        return jnp.concatenate([tp[:, :-2], tp[:, 1:-1], tp[:, 2:]], axis=2)

    kw, vw = bands(k), bands(v)
    s_win = jnp.einsum('bnqhgd,bnkhd->bnhgqk', qb, kw).astype(jnp.float32) * ATTN_SCALE
    s_ctx = jnp.einsum('bnqhgd,bchd->bnhgqc', qb, kc).astype(jnp.float32) * ATTN_SCALE
    qi = jnp.arange(BLOCK)[:, None]
    kj = jnp.arange(3 * BLOCK)[None, :]
    in_win = jnp.abs(kj - BLOCK - qi) <= WINDOW
    kpos = jnp.arange(nb)[:, None] * BLOCK + kj - BLOCK
    valid = in_win[None] & ((kpos >= 0) & (kpos < n_tok))[:, None, :]
    s_win = jnp.where(valid[None, :, None, None], s_win, NEG_INF)
    logits = jnp.concatenate([sink_column(sink, s_ctx.shape[:-1]), s_ctx, s_win], axis=-1)
    p = jax.nn.softmax(logits, axis=-1)
    p_ctx = p[..., 1:1 + n_ctx].astype(v.dtype)
    p_win = p[..., 1 + n_ctx:].astype(v.dtype)
    o = (jnp.einsum('bnhgqc,bchd->bnqhgd', p_ctx, vc)
         + jnp.einsum('bnhgqk,bnkhd->bnqhgd', p_win, vw))
    return o.reshape(bn, n_tok, ATTN_WIDTH)


def context_attention(qc, kc, vc, sink):
    bn, n_ctx = qc.shape[:2]
    q = qc.reshape(bn, n_ctx, N_KV_HEADS, Q_PER_KV, HEAD_DIM)
    s = jnp.einsum('bqhgd,bkhd->bhgqk', q, kc).astype(jnp.float32) * ATTN_SCALE
    p = jax.nn.softmax(jnp.concatenate([sink_column(sink, s.shape[:-1]), s], axis=-1), axis=-1)
    o = jnp.einsum('bhgqk,bkhd->bqhgd', p[..., 1:].astype(vc.dtype), vc)
    return o.reshape(bn, n_ctx, ATTN_WIDTH)


def fourier_attention_mix(h_lat, h_ctx, w_in, w_out, sink, cos, sin, with_ctx_out):
    bn, n_tok, _ = h_lat.shape
    n_ctx = h_ctx.shape[1]
    p = h_lat @ w_in
    f = p[..., :FOURIER_WIDTH]
    q = p[..., FOURIER_WIDTH:KV_START].reshape(bn, n_tok, N_Q_HEADS, HEAD_DIM)
    k = p[..., KV_START:KV_START + KV_WIDTH].reshape(bn, n_tok, N_KV_HEADS, HEAD_DIM)
    v = p[..., KV_START + KV_WIDTH:].reshape(bn, n_tok, N_KV_HEADS, HEAD_DIM)
    kvc = h_ctx @ w_in[:, KV_START:]
    kc = kvc[..., :KV_WIDTH].reshape(bn, n_ctx, N_KV_HEADS, HEAD_DIM)
    vc = kvc[..., KV_WIDTH:].reshape(bn, n_ctx, N_KV_HEADS, HEAD_DIM)
    q, k = apply_rope(q, cos, sin), apply_rope(k, cos, sin)
    o_lat = jnp.concatenate([fourier_mix(f), window_attention(q, k, v, kc, vc, sink)], axis=-1) @ w_out
    o_ctx = None
    if with_ctx_out:
        pc = h_ctx @ w_in[:, :KV_START]
        qc = pc[..., FOURIER_WIDTH:].reshape(bn, n_ctx, N_Q_HEADS, HEAD_DIM)
        o_ctx = jnp.concatenate([fourier_mix(pc[..., :FOURIER_WIDTH]),
                                 context_attention(qc, kc, vc, sink)], axis=-1) @ w_out
    return o_lat, o_ctx


def s5_discretize(a_re, a_im, log_dt, b_re, b_im):
    a_re, a_im = a_re.astype(jnp.float32), a_im.astype(jnp.float32)
    b_re, b_im = b_re.astype(jnp.float32), b_im.astype(jnp.float32)
    dt = jnp.exp(log_dt.astype(jnp.float32))[:, None]
    mag = jnp.exp(a_re * dt)
    abar_re, abar_im = mag * jnp.cos(a_im * dt), mag * jnp.sin(a_im * dt)
    nr, ni = abar_re - 1.0, abar_im
    den = a_re * a_re + a_im * a_im
    f_re = (nr * a_re + ni * a_im) / den
    f_im = (ni * a_re - nr * a_im) / den
    bbar_re = f_re[..., None] * b_re - f_im[..., None] * b_im
    bbar_im = f_re[..., None] * b_im + f_im[..., None] * b_re
    return abar_re, abar_im, bbar_re, bbar_im


def complex_scan(abar_re, abar_im, bu_re, bu_im, h0):
    if h0 is not None:
        h0_re, h0_im = h0
        bu_re = bu_re.at[0].add(abar_re * h0_re - abar_im * h0_im)
        bu_im = bu_im.at[0].add(abar_re * h0_im + abar_im * h0_re)
    n_t = bu_re.shape[0]
    a_re = jnp.broadcast_to(abar_re, (n_t, 1) + abar_re.shape)
    a_im = jnp.broadcast_to(abar_im, (n_t, 1) + abar_im.shape)

    def combine(e1, e2):
        ar1, ai1, br1, bi1 = e1
        ar2, ai2, br2, bi2 = e2
        return (ar1 * ar2 - ai1 * ai2, ar1 * ai2 + ai1 * ar2,
                ar2 * br1 - ai2 * bi1 + br2, ar2 * bi1 + ai2 * br1 + bi2)

    _, _, h_re, h_im = lax.associative_scan(combine, (a_re, a_im, bu_re, bu_im), axis=0)
    return h_re, h_im


def s5_drive(bbar_re, bbar_im, u):
    return (jnp.einsum('gph,tbgh->tbgp', bbar_re, u), jnp.einsum('gph,tbgh->tbgp', bbar_im, u))


def s5_readout(c_re, c_im, h_re, h_im):
    return jnp.einsum('ghp,tbgp->tbgh', c_re, h_re) - jnp.einsum('ghp,tbgp->tbgh', c_im, h_im)


def s5_glu(y, like, glu_w):
    n_t, bn = y.shape[:2]
    y = y.reshape(n_t, bn, D_MODEL).transpose(1, 0, 2).astype(like.dtype)
    a, g = jnp.split(jax.nn.gelu(y) @ glu_w, 2, axis=-1)
    return a * jax.nn.sigmoid(g)


def s5_mix(h_lat, h_ctx, a_re, a_im, log_dt, b_re, b_im, c_re, c_im, d_skip, glu_w, with_ctx_out):
    bn = h_lat.shape[0]

    def time_major(t):
        return t.astype(jnp.float32).transpose(1, 0, 2).reshape(t.shape[1], bn, SSM_GROUPS, SSM_GROUP_W)

    u_lat, u_ctx = time_major(h_lat), time_major(h_ctx)
    d_g = d_skip.astype(jnp.float32).reshape(SSM_GROUPS, SSM_GROUP_W)
    y_lat = d_g * u_lat
    y_ctx = d_g * u_ctx if with_ctx_out else None
    for direction in range(2):
        abar_re, abar_im, bbar_re, bbar_im = s5_discretize(
            a_re[direction], a_im[direction], log_dt[direction], b_re[direction], b_im[direction])
        cr, ci = c_re[direction].astype(jnp.float32), c_im[direction].astype(jnp.float32)
        seq_ctx = u_ctx if direction == 0 else u_ctx[::-1]
        seq_lat = u_lat if direction == 0 else u_lat[::-1]
        hc_re, hc_im = complex_scan(abar_re, abar_im, *s5_drive(bbar_re, bbar_im, seq_ctx), None)
        hl_re, hl_im = complex_scan(abar_re, abar_im, *s5_drive(bbar_re, bbar_im, seq_lat),
                                    (hc_re[-1], hc_im[-1]))
        yl = s5_readout(cr, ci, hl_re, hl_im)
        y_lat = y_lat + (yl if direction == 0 else yl[::-1])
        if with_ctx_out:
            yc = s5_readout(cr, ci, hc_re, hc_im)
            y_ctx = y_ctx + (yc if direction == 0 else yc[::-1])
    o_lat = s5_glu(y_lat, h_lat, glu_w)
    o_ctx = s5_glu(y_ctx, h_ctx, glu_w) if with_ctx_out else None
    return o_lat, o_ctx


def setup_inputs(seed: int = 0) -> dict:
    key = jax.random.key(seed)
    ks = jax.random.split(key, 23)
    f32 = jnp.float32
    D, F, G, P, H = D_MODEL, FFN_HIDDEN, SSM_GROUPS, SSM_STATE, SSM_GROUP_W

    def nrm(k, shape, std):
        return std * jax.random.normal(k, shape, f32)

    return {
        'x': nrm(ks[0], (BATCH, SEQ, D), 1.0),
        'c': nrm(ks[1], (BATCH, D), 1.0),
        'ctx': nrm(ks[2], (BATCH, CTX_LEN, D), 1.0),
        'c_ctx': nrm(ks[3], (D,), 1.0),
        'mod_w': nrm(ks[4], (DEPTH, D, N_MOD * D), 0.5 * D ** -0.5),
        'mod_b': nrm(ks[5], (DEPTH, N_MOD * D), 0.02),
        'norm_g': 1.0 + nrm(ks[6], (DEPTH, 2, D), 0.02),
        'ffn_w_gate': nrm(ks[7], (DEPTH, D, F), D ** -0.5),
        'ffn_w_up': nrm(ks[8], (DEPTH, D, F), D ** -0.5),
        'ffn_w_down': nrm(ks[9], (DEPTH, F, D), F ** -0.5),
        'mix_w_in': nrm(ks[10], (N_EVEN, D, MIX_IN_WIDTH), D ** -0.5),
        'mix_w_out': nrm(ks[11], (N_EVEN, MIX_OUT_WIDTH, D), MIX_OUT_WIDTH ** -0.5),
        'attn_sink': nrm(ks[12], (N_EVEN, N_Q_HEADS), 0.5),
        'ssm_a_re': -0.5 + nrm(ks[13], (N_ODD, 2, G, P), 0.01),
        'ssm_a_im': jnp.pi * jnp.arange(P, dtype=f32) + nrm(ks[14], (N_ODD, 2, G, P), 0.01),
        'ssm_log_dt': jax.random.uniform(ks[15], (N_ODD, 2, G), f32, math.log(DT_MIN), math.log(DT_MAX)),
        'ssm_b_re': nrm(ks[16], (N_ODD, 2, G, P, H), (2 * H) ** -0.5),
        'ssm_b_im': nrm(ks[17], (N_ODD, 2, G, P, H), (2 * H) ** -0.5),
        'ssm_c_re': nrm(ks[18], (N_ODD, 2, G, H, P), 0.5),
        'ssm_c_im': nrm(ks[19], (N_ODD, 2, G, H, P), 0.5),
        'ssm_d': nrm(ks[20], (N_ODD, D), 1.0),
        'ssm_glu_w': nrm(ks[21], (N_ODD, D, 2 * D), D ** -0.5),
        'final_g': 1.0 + nrm(ks[22], (D,), 0.02),
    }


def reference(x, c, ctx, c_ctx, mod_w, mod_b, norm_g, ffn_w_gate, ffn_w_up, ffn_w_down,
              mix_w_in, mix_w_out, attn_sink, ssm_a_re, ssm_a_im, ssm_log_dt, ssm_b_re, ssm_b_im,
              ssm_c_re, ssm_c_im, ssm_d, ssm_glu_w, final_g):
    rows = x.shape[1] // GRID_W
    cos, sin = axial_rope(rows)
    h, hc = x, ctx
    for layer in range(DEPTH):
        with_ctx_out = layer < DEPTH - 1
        i = layer // 2
        sh1, sc1, g1, sh2, sc2, g2 = [m[:, None, :] for m in adaln(c, mod_w[layer], mod_b[layer])]
        csh1, csc1, cg1, csh2, csc2, cg2 = adaln(c_ctx, mod_w[layer], mod_b[layer])
        n_lat = rms_norm(h, norm_g[layer, 0]) * (1.0 + sc1) + sh1
        n_ctx = rms_norm(hc, norm_g[layer, 0]) * (1.0 + csc1) + csh1
        if layer % 2 == 0:
            o_lat, o_ctx = fourier_attention_mix(n_lat, n_ctx, mix_w_in[i], mix_w_out[i], attn_sink[i],
                                                 cos, sin, with_ctx_out)
        else:
            o_lat, o_ctx = s5_mix(n_lat, n_ctx, ssm_a_re[i], ssm_a_im[i], ssm_log_dt[i], ssm_b_re[i],
                                  ssm_b_im[i], ssm_c_re[i], ssm_c_im[i], ssm_d[i], ssm_glu_w[i],
                                  with_ctx_out)
        h = h + g1 * o_lat
        h = h + g2 * swiglu(rms_norm(h, norm_g[layer, 1]) * (1.0 + sc2) + sh2,
                            ffn_w_gate[layer], ffn_w_up[layer], ffn_w_down[layer])
        if with_ctx_out:
            hc = hc + cg1 * o_ctx
            hc = hc + cg2 * swiglu(rms_norm(hc, norm_g[layer, 1]) * (1.0 + csc2) + csh2,
                                   ffn_w_gate[layer], ffn_w_up[layer], ffn_w_down[layer])
    return rms_norm(h, final_g)
```

```python
import functools
import math

import numpy as np
import jax
import jax.numpy as jnp
from jax import lax
from jax.experimental import pallas as pl
from jax.experimental.pallas import tpu as pltpu

F32 = jnp.float32
BF16 = jnp.bfloat16

D_MODEL = 1024
GRID_W = 64
N_MOD = 6
N_Q_HEADS = 8
N_KV_HEADS = 2
HEAD_DIM = 64
ATTN_WIDTH = N_Q_HEADS * HEAD_DIM
KV_WIDTH = N_KV_HEADS * HEAD_DIM
FOURIER_GROUPS = 8
FOURIER_GROUP_W = 64
FOURIER_WIDTH = FOURIER_GROUPS * FOURIER_GROUP_W
KV_START = FOURIER_WIDTH + ATTN_WIDTH
MIX_IN_WIDTH = KV_START + 2 * KV_WIDTH
WINDOW = 128
BLOCK = 128
ATTN_SCALE = HEAD_DIM ** -0.5
ROPE_BASE = 10000.0
NEG_INF = -1e30
SSM_GROUP_W = 16
SSM_GROUPS = D_MODEL // SSM_GROUP_W
SSM_STATE = 64
SSM_WIDTH = SSM_GROUPS * SSM_STATE
RMS_EPS = 1e-6

LANES = 128
SUBLANES = 8
VMEM_LIMIT_BYTES = 56 * 1024 * 1024

TOKEN_TILE = 512
S5_POST_TILE = 256
FFN_CHUNK = 256
MOD_TILE = 1536
DFT_ROW_TILE = 1024
DFT_K_TILE = 2048
S5_CHUNK = 256
S5_SLAB_GROUPS = LANES // SSM_GROUP_W
S5_SLABS = SSM_GROUPS // S5_SLAB_GROUPS
S5_SLAB_STATES = S5_SLAB_GROUPS * SSM_STATE
S5_COLS = SSM_WIDTH // LANES
S5_PLANES = S5_COLS // SUBLANES


def _cparams(*sem):
    return pltpu.CompilerParams(dimension_semantics=sem, vmem_limit_bytes=VMEM_LIMIT_BYTES)


def _sigmoid(x):
    return 1.0 / (1.0 + jnp.exp(-x))


def _norm_mod(x, g, scale, shift):
    y = x * lax.rsqrt(jnp.mean(x * x, axis=-1, keepdims=True) + RMS_EPS)
    return (y * g) * (1.0 + scale) + shift


def _split_bf16(x):
    hi = x.astype(BF16)
    lo = (x - hi.astype(F32)).astype(BF16)
    return hi, lo


def _mod_kernel(cond_ref, w_ref, b_ref, o_ref):
    x = cond_ref[...]
    s_hi, s_lo = _split_bf16(x * _sigmoid(x))
    w_hi, w_lo = _split_bf16(w_ref[0])
    acc = jnp.dot(s_hi, w_hi, preferred_element_type=F32)
    acc += jnp.dot(s_lo, w_hi, preferred_element_type=F32)
    acc += jnp.dot(s_hi, w_lo, preferred_element_type=F32)
    o_ref[0] = acc + b_ref[0]


def _modulations(cond, mod_w, mod_b):
    depth, d, n = mod_w.shape
    rows = cond.shape[0]
    return pl.pallas_call(
        _mod_kernel,
        grid=(depth, n // MOD_TILE),
        in_specs=[
            pl.BlockSpec((rows, d), lambda l, j: (0, 0)),
            pl.BlockSpec((1, d, MOD_TILE), lambda l, j: (l, 0, j)),
            pl.BlockSpec((1, 1, MOD_TILE), lambda l, j: (l, 0, j)),
        ],
        out_specs=pl.BlockSpec((1, rows, MOD_TILE), lambda l, j: (l, 0, j)),
        out_shape=jax.ShapeDtypeStruct((depth, rows, n), F32),
        compiler_params=_cparams("parallel", "parallel"),
        name="adaln_mod",
    )(cond, mod_w, mod_b.reshape(depth, 1, n))


def _pre0_kernel(x_ref, mod_ref, g_ref, w_ref, bcs_ref, cos_ref, sin_ref,
                 uv_ref, q_ref, k_ref, v_ref, *, rope):
    m = mod_ref[0]
    n = _norm_mod(x_ref[0], g_ref[...], m[1:2], m[0:1])
    p = jnp.dot(n.astype(BF16), w_ref[...], preferred_element_type=F32)
    uv = jnp.dot(p[:, :FOURIER_WIDTH].astype(BF16), bcs_ref[...], preferred_element_type=F32)
    uv_ref[0, 0] = uv[:, :FOURIER_WIDTH].astype(BF16)
    uv_ref[0, 1] = uv[:, FOURIER_WIDTH:].astype(BF16)

    def rot(x):
        if not rope:
            return x
        lane = lax.broadcasted_iota(jnp.int32, x.shape, 1)
        first = (lane & (HEAD_DIM // 2)) == 0
        partner = jnp.where(first, pltpu.roll(x, LANES - HEAD_DIM // 2, 1),
                            pltpu.roll(x, HEAD_DIM // 2, 1))
        return x * cos_ref[...] + partner * sin_ref[...]

    for j in range(ATTN_WIDTH // LANES):
        lo = FOURIER_WIDTH + j * LANES
        q_ref[0, :, j * LANES:(j + 1) * LANES] = rot(p[:, lo:lo + LANES]).astype(BF16)
    k_ref[0] = rot(p[:, KV_START:KV_START + KV_WIDTH]).astype(BF16)
    v_ref[0] = p[:, KV_START + KV_WIDTH:].astype(BF16)


def _pre0(x, mods, g, w_in, bcs, cos_t, sin_t, *, rope, rows):
    b, length, d = x.shape
    grid = (b, length // rows)
    row_map = lambda i, t: (i, t, 0)
    const2 = lambda i, t: (0, 0)
    return pl.pallas_call(
        functools.partial(_pre0_kernel, rope=rope),
        grid=grid,
        in_specs=[
            pl.BlockSpec((1, rows, d), row_map),
            pl.BlockSpec((1, N_MOD, d), lambda i, t: (i if mods.shape[0] > 1 else 0, 0, 0)),
            pl.BlockSpec((1, d), const2),
            pl.BlockSpec((d, MIX_IN_WIDTH), const2),
            pl.BlockSpec((FOURIER_WIDTH, 2 * FOURIER_WIDTH), const2),
            pl.BlockSpec((rows, LANES), lambda i, t: (t, 0)),
            pl.BlockSpec((rows, LANES), lambda i, t: (t, 0)),
        ],
        out_specs=[
            pl.BlockSpec((1, 2, rows, FOURIER_WIDTH), lambda i, t: (i, 0, t, 0)),
            pl.BlockSpec((1, rows, ATTN_WIDTH), row_map),
            pl.BlockSpec((1, rows, KV_WIDTH), row_map),
            pl.BlockSpec((1, rows, KV_WIDTH), row_map),
        ],
        out_shape=[
            jax.ShapeDtypeStruct((b, 2, length, FOURIER_WIDTH), BF16),
            jax.ShapeDtypeStruct((b, length, ATTN_WIDTH), BF16),
            jax.ShapeDtypeStruct((b, length, KV_WIDTH), BF16),
            jax.ShapeDtypeStruct((b, length, KV_WIDTH), BF16),
        ],
        compiler_params=_cparams("parallel", "parallel"),
        name="mix_in_rope" if rope else "mix_in_ctx",
    )(x, mods, g, w_in, bcs, cos_t, sin_t)


def _dft_kernel(cs_ref, uv_ref, o_ref, acc_ref):
    k = pl.program_id(2)

    @pl.when(k == 0)
    def _():
        acc_ref[...] = jnp.zeros_like(acc_ref)

    acc_ref[...] += jnp.dot(cs_ref[...], uv_ref[0], preferred_element_type=F32)

    @pl.when(k == pl.num_programs(2) - 1)
    def _():
        o_ref[0] = acc_ref[...].astype(BF16)


def _time_dft(cs, uv):
    b, k2, w = uv.shape
    t = cs.shape[0]
    tm = min(DFT_ROW_TILE, t)
    tk = min(DFT_K_TILE, k2)
    return pl.pallas_call(
        _dft_kernel,
        grid=(b, t // tm, k2 // tk),
        in_specs=[
            pl.BlockSpec((tm, tk), lambda i, r, k: (r, k)),
            pl.BlockSpec((1, tk, w), lambda i, r, k: (i, k, 0)),
        ],
        out_specs=pl.BlockSpec((1, tm, w), lambda i, r, k: (i, r, 0)),
        out_shape=jax.ShapeDtypeStruct((b, t, w), BF16),
        scratch_shapes=[pltpu.VMEM((tm, w), F32)],
        compiler_params=_cparams("parallel", "parallel", "arbitrary"),
        name="time_dft",
    )(cs, uv)


def _attn_kernel(sink_ref, q_ref, *refs, window):
    if window:
        kp_ref, kc_ref, kn_ref, vp_ref, vc_ref, vn_ref, kx_ref, vx_ref, o_ref = refs
        kcat = jnp.concatenate([kx_ref[0], kp_ref[0], kc_ref[0], kn_ref[0]], axis=0)
        vcat = jnp.concatenate([vx_ref[0], vp_ref[0], vc_ref[0], vn_ref[0]], axis=0)
    else:
        kx_ref, vx_ref, o_ref = refs
        kcat, vcat = kx_ref[0], vx_ref[0]
    n_keys = kcat.shape[0]
    n_q = q_ref.shape[1]

    lane = lax.broadcasted_iota(jnp.int32, kcat.shape, 1)
    low = lane < HEAD_DIM
    zero = jnp.zeros_like(kcat)
    kswap = pltpu.roll(kcat.astype(F32), HEAD_DIM, 1).astype(BF16)
    vswap = pltpu.roll(vcat.astype(F32), HEAD_DIM, 1).astype(BF16)
    k_lo = (jnp.where(low, kcat, zero), jnp.where(low, kswap, zero))
    k_hi = (jnp.where(low, zero, kswap), jnp.where(low, zero, kcat))
    v_lo = (jnp.where(low, vcat, zero), jnp.where(low, vswap, zero))
    v_hi = (jnp.where(low, zero, vswap), jnp.where(low, zero, vcat))

    if window:
        blk = pl.program_id(1)
        last = pl.num_programs(1) - 1
        n_ctx = kx_ref.shape[1]
        row = lax.broadcasted_iota(jnp.int32, (n_q, n_keys), 0)
        col = lax.broadcasted_iota(jnp.int32, (n_q, n_keys), 1)
        in_prev = (col >= n_ctx) & (col < n_ctx + BLOCK)
        in_next = col >= n_ctx + 2 * BLOCK
        off_prev = jnp.where(blk > 0, 0, BLOCK)
        off_next = jnp.where(blk < last, 0, BLOCK)
        ok_prev = col - n_ctx >= row + off_prev
        ok_next = col - (n_ctx + 2 * BLOCK) <= row - off_next
        valid = (jnp.logical_not(in_prev) | ok_prev) & (jnp.logical_not(in_next) | ok_next)

    q_per_kv = N_Q_HEADS // N_KV_HEADS
    for j in range(ATTN_WIDTH // LANES):
        g = (2 * j) // q_per_kv
        qp = q_ref[0, :, j * LANES:(j + 1) * LANES]
        out = None
        for half, (k_ext, v_ext) in enumerate(((k_lo[g], v_lo[g]), (k_hi[g], v_hi[g]))):
            s = lax.dot_general(qp, k_ext, (((1,), (1,)), ((), ())),
                                preferred_element_type=F32) * ATTN_SCALE
            if window:
                s = jnp.where(valid, s, NEG_INF)
            sink = sink_ref[2 * j + half]
            mx = jnp.maximum(jnp.max(s, axis=1, keepdims=True), sink)
            p = jnp.exp(s - mx)
            denom = jnp.sum(p, axis=1, keepdims=True) + jnp.exp(sink - mx)
            o = jnp.dot(p.astype(BF16), v_ext, preferred_element_type=F32) * (1.0 / denom)
            out = o if out is None else out + o
        o_ref[0, :, j * LANES:(j + 1) * LANES] = out.astype(BF16)


def _window_attention(sink, q, k, v, kx, vx):
    b, length, _ = q.shape
    nb = length // BLOCK
    n_ctx = kx.shape[1]
    kv_spec = lambda f: pl.BlockSpec((1, BLOCK, KV_WIDTH), f)
    prev = lambda i, n: (i, jnp.maximum(n - 1, 0), 0)
    cur = lambda i, n: (i, n, 0)
    nxt = lambda i, n: (i, jnp.minimum(n + 1, nb - 1), 0)
    ctx_spec = pl.BlockSpec((1, n_ctx, KV_WIDTH), lambda i, n: (i, 0, 0))
    return pl.pallas_call(
        functools.partial(_attn_kernel, window=True),
        grid=(b, nb),
        in_specs=[
            pl.BlockSpec(memory_space=pltpu.SMEM),
            pl.BlockSpec((1, BLOCK, ATTN_WIDTH), cur),
            kv_spec(prev), kv_spec(cur), kv_spec(nxt),
            kv_spec(prev), kv_spec(cur), kv_spec(nxt),
            ctx_spec, ctx_spec,
        ],
        out_specs=pl.BlockSpec((1, BLOCK, ATTN_WIDTH), cur),
        out_shape=jax.ShapeDtypeStruct((b, length, ATTN_WIDTH), BF16),
        compiler_params=_cparams("parallel", "parallel"),
        name="window_attn",
    )(sink, q, k, k, k, v, v, v, kx, vx)


def _context_attention(sink, q, kx, vx):
    b, n_ctx, _ = q.shape
    spec = lambda w: pl.BlockSpec((1, n_ctx, w), lambda i: (i, 0, 0))
    return pl.pallas_call(
        functools.partial(_attn_kernel, window=False),
        grid=(b,),
        in_specs=[pl.BlockSpec(memory_space=pltpu.SMEM), spec(ATTN_WIDTH), spec(KV_WIDTH), spec(KV_WIDTH)],
        out_specs=spec(ATTN_WIDTH),
        out_shape=jax.ShapeDtypeStruct((b, n_ctx, ATTN_WIDTH), BF16),
        compiler_params=_cparams("parallel"),
        name="context_attn",
    )(sink, q, kx, vx)


def _swiglu_residual(h1, m, g2, wg_ref, wu_ref, wd_ref, acc_ref):
    n2 = _norm_mod(h1, g2, m[4:5], m[3:4]).astype(BF16)
    acc_ref[...] = jnp.zeros_like(acc_ref)

    def chunk(c, carry):
        gate = jnp.dot(n2, wg_ref[c], preferred_element_type=F32)
        up = jnp.dot(n2, wu_ref[c], preferred_element_type=F32)
        act = (gate * _sigmoid(gate)) * up
        acc_ref[...] += jnp.dot(act.astype(BF16), wd_ref[c], preferred_element_type=F32)
        return carry

    lax.fori_loop(0, wg_ref.shape[0], chunk, 0)
    return h1 + m[5:6] * acc_ref[...]


def _post0_kernel(h_ref, mod_ref, g2_ref, f_ref, a_ref, wo_ref, wg_ref, wu_ref, wd_ref, o_ref, acc_ref):
    m = mod_ref[0]
    mix = jnp.dot(f_ref[0], wo_ref[:FOURIER_WIDTH, :], preferred_element_type=F32)
    mix += jnp.dot(a_ref[0], wo_ref[FOURIER_WIDTH:, :], preferred_element_type=F32)
    h1 = h_ref[0] + m[2:3] * mix
    o_ref[0] = _swiglu_residual(h1, m, g2_ref[...], wg_ref, wu_ref, wd_ref, acc_ref)


def _gelu_tanh(x):
    sqrt_2_over_pi = np.sqrt(2 / np.pi).astype(np.float32)
    cdf = 0.5 * (1.0 + jnp.tanh(sqrt_2_over_pi * (x + 0.044715 * (x ** 3))))
    return x * cdf


def _post1_kernel(h_ref, mod_ref, g1_ref, g2_ref, gf_ref, dskip_ref, yf_ref, yb_ref, wglu_ref,
                  wg_ref, wu_ref, wd_ref, o_ref, acc_ref):
    m = mod_ref[0]
    h = h_ref[0]
    u = _norm_mod(h, g1_ref[...], m[1:2], m[0:1])
    y = dskip_ref[...] * u + yf_ref[0] + yb_ref[0]
    z = jnp.dot(_gelu_tanh(y).astype(BF16), wglu_ref[...], preferred_element_type=F32)
    h1 = h + m[2:3] * (z[:, :D_MODEL] * _sigmoid(z[:, D_MODEL:]))
    h2 = _swiglu_residual(h1, m, g2_ref[...], wg_ref, wu_ref, wd_ref, acc_ref)
    o_ref[0] = (h2 * lax.rsqrt(jnp.mean(h2 * h2, axis=-1, keepdims=True) + RMS_EPS)) * gf_ref[...]


def _resident(shape):
    zeros = (0,) * len(shape)
    return pl.BlockSpec(shape, lambda i, t: zeros, pipeline_mode=pl.Buffered(1))


def _post_call(kernel, name, h, mods, row_inputs, vec_inputs, weights, rows):
    b, length, d = h.shape
    row_map = lambda i, t: (i, t, 0)
    in_specs = [pl.BlockSpec((1, rows, d), row_map),
                pl.BlockSpec((1, N_MOD, d), lambda i, t: (i if mods.shape[0] > 1 else 0, 0, 0))]
    in_specs += [pl.BlockSpec((1, d), lambda i, t: (0, 0)) for _ in vec_inputs]
    in_specs += [pl.BlockSpec((1, rows, a.shape[-1]), row_map) for a in row_inputs]
    in_specs += [_resident(w.shape) for w in weights]
    return pl.pallas_call(
        kernel,
        grid=(b, length // rows),
        in_specs=in_specs,
        out_specs=pl.BlockSpec((1, rows, d), row_map),
        out_shape=jax.ShapeDtypeStruct((b, length, d), F32),
        scratch_shapes=[pltpu.VMEM((rows, d), F32)],
        compiler_params=_cparams("parallel", "parallel"),
        name=name,
    )(h, mods, *vec_inputs, *row_inputs, *weights)


def _pre1_kernel(x_ref, mod_ref, g_ref, o_ref):
    m = mod_ref[0]
    o_ref[0] = _norm_mod(x_ref[0], g_ref[...], m[1:2], m[0:1]).astype(BF16)


def _pre1(x, mods, g, rows):
    b, length, d = x.shape
    row_map = lambda i, t: (i, t, 0)
    return pl.pallas_call(
        _pre1_kernel,
        grid=(b, length // rows),
        in_specs=[pl.BlockSpec((1, rows, d), row_map),
                  pl.BlockSpec((1, N_MOD, d), lambda i, t: (i if mods.shape[0] > 1 else 0, 0, 0)),
                  pl.BlockSpec((1, d), lambda i, t: (0, 0))],
        out_specs=pl.BlockSpec((1, rows, d), row_map),
        out_shape=jax.ShapeDtypeStruct((b, length, d), BF16),
        compiler_params=_cparams("parallel", "parallel"),
        name="s5_input_norm",
    )(x, mods, g)


def _disc_kernel(are_ref, aim_ref, ldt_ref, bre_ref, bim_ref, oar_ref, oai_ref, obr_ref, obi_ref):
    a_re, a_im = are_ref[0], aim_ref[0]
    dt = jnp.exp(ldt_ref[0])
    mag = jnp.exp(a_re * dt)
    abar_re, abar_im = mag * jnp.cos(a_im * dt), mag * jnp.sin(a_im * dt)
    nr, ni = abar_re - 1.0, abar_im
    den = a_re * a_re + a_im * a_im
    f_re = (nr * a_re + ni * a_im) / den
    f_im = (ni * a_re - nr * a_im) / den
    oar_ref[0] = abar_re
    oai_ref[0] = abar_im
    obr_ref[0] = f_re * bre_ref[0] - f_im * bim_ref[0]
    obi_ref[0] = f_re * bim_ref[0] + f_im * bre_ref[0]


def _discretize(a_re, a_im, log_dt, b_re, b_im):
    nd = a_re.shape[0]
    col = lambda a: a.reshape(nd, SSM_WIDTH, 1)
    ldt = jnp.broadcast_to(log_dt[:, :, None], (nd, SSM_GROUPS, SSM_STATE))
    mat = lambda a: a.reshape(nd, SSM_WIDTH, SSM_GROUP_W)
    cspec = pl.BlockSpec((1, SSM_WIDTH, 1), lambda i: (i, 0, 0))
    mspec = pl.BlockSpec((1, SSM_WIDTH, SSM_GROUP_W), lambda i: (i, 0, 0))
    cshape = jax.ShapeDtypeStruct((nd, SSM_WIDTH, 1), F32)
    mshape = jax.ShapeDtypeStruct((nd, SSM_WIDTH, SSM_GROUP_W), F32)
    return pl.pallas_call(
        _disc_kernel,
        grid=(nd,),
        in_specs=[cspec, cspec, cspec, mspec, mspec],
        out_specs=[cspec, cspec, mspec, mspec],
        out_shape=[cshape, cshape, mshape, mshape],
        compiler_params=_cparams("parallel"),
        name="s5_discretize",
    )(col(a_re), col(a_im), col(ldt), mat(b_re), mat(b_im))


def _s5_kernel(uc_ref, ul_ref, wd_ref, wr_ref, are_ref, aim_ref, y_ref, xre, xim, hre, him, *, reverse):
    step_idx = pl.program_id(1)
    steps = uc_ref.shape[1]

    @pl.when(step_idx == 0)
    def _():
        hre[...] = jnp.zeros_like(hre)
        him[...] = jnp.zeros_like(him)

    def plane_of(col):
        return col // SUBLANES, col % SUBLANES

    def run(u, readout):
        cols_per_slab = S5_SLAB_STATES // LANES
        for kk in range(S5_SLABS):
            bu = jnp.dot(u[:, kk * LANES:(kk + 1) * LANES], wd_ref[kk], preferred_element_type=F32)
            for c in range(cols_per_slab):
                pln, sub = plane_of(kk * cols_per_slab + c)
                xre[pln, pl.ds(sub, steps, stride=SUBLANES), :] = bu[:, c * LANES:(c + 1) * LANES]
                xim[pln, pl.ds(sub, steps, stride=SUBLANES), :] = (
                    bu[:, S5_SLAB_STATES + c * LANES:S5_SLAB_STATES + (c + 1) * LANES])

        a_r = [are_ref[p] for p in range(S5_PLANES)]
        a_i = [aim_ref[p] for p in range(S5_PLANES)]

        def step(i, carry):
            h_r, h_i = carry
            t = (steps - 1 - i) if reverse else i
            r0 = pl.multiple_of(t * SUBLANES, SUBLANES)
            new_r, new_i = [], []
            for p in range(S5_PLANES):
                n_r = a_r[p] * h_r[p] - a_i[p] * h_i[p] + xre[p, pl.ds(r0, SUBLANES), :]
                n_i = a_r[p] * h_i[p] + a_i[p] * h_r[p] + xim[p, pl.ds(r0, SUBLANES), :]
                xre[p, pl.ds(r0, SUBLANES), :] = n_r
                xim[p, pl.ds(r0, SUBLANES), :] = n_i
                new_r.append(n_r)
                new_i.append(n_i)
            return tuple(new_r), tuple(new_i)

        init = (tuple(hre[p] for p in range(S5_PLANES)), tuple(him[p] for p in range(S5_PLANES)))
        h_r, h_i = lax.fori_loop(0, steps, step, init, unroll=8)
        for p in range(S5_PLANES):
            hre[p] = h_r[p]
            him[p] = h_i[p]

        if readout:
            for kk in range(S5_SLABS):
                parts = []
                for src in (xre, xim):
                    for c in range(cols_per_slab):
                        pln, sub = plane_of(kk * cols_per_slab + c)
                        parts.append(src[pln, pl.ds(sub, steps, stride=SUBLANES), :].astype(BF16))
                lhs = jnp.concatenate(parts, axis=1)
                y_ref[0, :, kk * LANES:(kk + 1) * LANES] = jnp.dot(
                    lhs, wr_ref[kk], preferred_element_type=F32)

    @pl.when(step_idx == 0)
    def _():
        run(uc_ref[0], readout=False)

    @pl.when(step_idx > 0)
    def _():
        run(ul_ref[0], readout=True)


def _s5_direction(u_ctx, u_lat, w_drive, w_read, a_re, a_im, *, reverse):
    b, length, d = u_lat.shape
    n_ctx = u_ctx.shape[1]
    assert n_ctx == S5_CHUNK and length % S5_CHUNK == 0
    n_chunks = length // S5_CHUNK
    if reverse:
        lat_map = lambda i, n: (i, jnp.minimum(n_chunks - n, n_chunks - 1), 0)
    else:
        lat_map = lambda i, n: (i, jnp.maximum(n - 1, 0), 0)
    whole = lambda shape: pl.BlockSpec(shape, lambda i, n: (0,) * len(shape))
    return pl.pallas_call(
        functools.partial(_s5_kernel, reverse=reverse),
        grid=(b, n_chunks + 1),
        in_specs=[
            pl.BlockSpec((1, n_ctx, d), lambda i, n: (i, 0, 0)),
            pl.BlockSpec((1, S5_CHUNK, d), lat_map),
            whole(w_drive.shape), whole(w_read.shape), whole(a_re.shape), whole(a_im.shape),
        ],
        out_specs=pl.BlockSpec((1, S5_CHUNK, d), lat_map),
        out_shape=jax.ShapeDtypeStruct((b, length, d), F32),
        scratch_shapes=[
            pltpu.VMEM((S5_PLANES, S5_CHUNK * SUBLANES, LANES), F32),
            pltpu.VMEM((S5_PLANES, S5_CHUNK * SUBLANES, LANES), F32),
            pltpu.VMEM((S5_PLANES, SUBLANES, LANES), F32),
            pltpu.VMEM((S5_PLANES, SUBLANES, LANES), F32),
        ],
        compiler_params=_cparams("parallel", "arbitrary"),
        name="s5_scan_bwd" if reverse else "s5_scan_fwd",
    )(u_ctx, u_lat, w_drive, w_read, a_re, a_im)


def _rope_tables(length):
    rows = length // GRID_W
    t_row = jnp.repeat(jnp.arange(rows, dtype=F32), GRID_W)
    t_col = jnp.tile(jnp.arange(GRID_W, dtype=F32), rows)
    n_freq = HEAD_DIM // 4
    inv_freq = ROPE_BASE ** (-jnp.arange(n_freq, dtype=F32) / n_freq)
    ang = jnp.concatenate([t_row[:, None] * inv_freq, t_col[:, None] * inv_freq], axis=-1)
    cos, sin = jnp.cos(ang), jnp.sin(ang)
    reps = LANES // HEAD_DIM
    return (jnp.tile(jnp.concatenate([cos, cos], axis=-1), (1, reps)),
            jnp.tile(jnp.concatenate([-sin, sin], axis=-1), (1, reps)))


def _half_split_perm():
    perm = np.arange(MIX_IN_WIDTH)
    head = np.concatenate([np.arange(0, HEAD_DIM, 2), np.arange(1, HEAD_DIM, 2)])
    for start in range(FOURIER_WIDTH, KV_START + KV_WIDTH, HEAD_DIM):
        perm[start:start + HEAD_DIM] = start + head
    return perm


def _channel_dft():
    c = np.arange(FOURIER_GROUP_W)
    ang = 2.0 * np.pi * np.outer(c, c) / FOURIER_GROUP_W
    eye = np.eye(FOURIER_GROUPS)
    scale = FOURIER_GROUP_W ** -0.5
    table = np.concatenate([np.kron(eye, np.cos(ang) * scale), np.kron(eye, np.sin(ang) * scale)], axis=1)
    return jnp.asarray(table, F32).astype(BF16)


def _time_dft_matrix(t):
    k = jnp.arange(t, dtype=jnp.int32)
    ang = ((k[:, None] * k[None, :]) % t).astype(F32) * (2.0 * math.pi / t)
    scale = t ** -0.5
    return jnp.concatenate([jnp.cos(ang) * scale, jnp.sin(ang) * (-scale)], axis=1).astype(BF16)


def _ffn_weights(w_gate, w_up, w_down):
    d, f = w_gate.shape
    n = f // FFN_CHUNK
    by_cols = lambda w: w.astype(BF16).reshape(d, n, FFN_CHUNK).transpose(1, 0, 2)
    return by_cols(w_gate), by_cols(w_up), w_down.astype(BF16).reshape(n, FFN_CHUNK, d)


def _s5_weights(bbar_re, bbar_im, c_re, c_im):
    nd = bbar_re.shape[0]
    eye = jnp.eye(S5_SLAB_GROUPS, dtype=F32)

    def drive(bb):
        bb = bb.reshape(nd, S5_SLABS, S5_SLAB_GROUPS, SSM_STATE, SSM_GROUP_W)
        w = jnp.einsum("dkgph,gG->dkghGp", bb, eye)
        return w.reshape(nd, S5_SLABS, LANES, S5_SLAB_STATES)

    def read(cc):
        cc = cc.astype(F32).reshape(nd, S5_SLABS, S5_SLAB_GROUPS, SSM_GROUP_W, SSM_STATE)
        w = jnp.einsum("dkghp,gG->dkgpGh", cc, eye)
        return w.reshape(nd, S5_SLABS, S5_SLAB_STATES, LANES)

    w_drive = jnp.concatenate([drive(bbar_re), drive(bbar_im)], axis=-1).astype(BF16)
    w_read = jnp.concatenate([read(c_re), -read(c_im)], axis=2).astype(BF16)
    return w_drive, w_read


def kernel(x, c, ctx, c_ctx, mod_w, mod_b, norm_g, ffn_w_gate, ffn_w_up, ffn_w_down, mix_w_in, mix_w_out,
           attn_sink, ssm_a_re, ssm_a_im, ssm_log_dt, ssm_b_re, ssm_b_im, ssm_c_re, ssm_c_im, ssm_d,
           ssm_glu_w, final_g):
    bn, length, d = x.shape
    n_ctx = ctx.shape[1]
    depth = mod_w.shape[0]
    assert depth == 2 and d == D_MODEL

    cond = jnp.zeros((SUBLANES, d), F32).at[:bn].set(c).at[bn].set(c_ctx)
    mods = _modulations(cond, mod_w, mod_b).reshape(depth, SUBLANES, N_MOD, d)
    mods_lat, mods_ctx = mods[:, :bn], mods[:, bn:bn + 1]
    vec = lambda a: a.reshape(1, d)

    w_in = mix_w_in[0][:, _half_split_perm()].astype(BF16)
    w_out = mix_w_out[0].astype(BF16)
    bcs = _channel_dft()
    cos_t, sin_t = _rope_tables(length)
    ffn0 = _ffn_weights(ffn_w_gate[0], ffn_w_up[0], ffn_w_down[0])
    g_mix, g_ffn = vec(norm_g[0, 0]), vec(norm_g[0, 1])

    uv, q, k, v = _pre0(x, mods_lat[0], g_mix, w_in, bcs, cos_t, sin_t, rope=True, rows=TOKEN_TILE)
    uvc, qc, kc, vc = _pre0(ctx, mods_ctx[0], g_mix, w_in, bcs, cos_t, sin_t, rope=False, rows=n_ctx)
    four = _time_dft(_time_dft_matrix(length), uv.reshape(bn, 2 * length, FOURIER_WIDTH))
    four_c = _time_dft(_time_dft_matrix(n_ctx), uvc.reshape(bn, 2 * n_ctx, FOURIER_WIDTH))
    att = _window_attention(attn_sink[0], q, k, v, kc, vc)
    att_c = _context_attention(attn_sink[0], qc, kc, vc)
    h = _post_call(_post0_kernel, "mix_out_ffn", x, mods_lat[0], [four, att], [g_ffn],
                   [w_out, *ffn0], TOKEN_TILE)
    hc = _post_call(_post0_kernel, "mix_out_ffn_ctx", ctx, mods_ctx[0], [four_c, att_c], [g_ffn],
                    [w_out, *ffn0], n_ctx)

    g_mix, g_ffn = vec(norm_g[1, 0]), vec(norm_g[1, 1])
    u_lat = _pre1(h, mods_lat[1], g_mix, TOKEN_TILE)
    u_ctx = _pre1(hc, mods_ctx[1], g_mix, n_ctx)
    abar_re, abar_im, bbar_re, bbar_im = _discretize(
        ssm_a_re[0], ssm_a_im[0], ssm_log_dt[0], ssm_b_re[0], ssm_b_im[0])
    w_drive, w_read = _s5_weights(bbar_re, bbar_im, ssm_c_re[0], ssm_c_im[0])
    planes = lambda a: a.reshape(2, S5_PLANES, SUBLANES, LANES)
    abar_re, abar_im = planes(abar_re), planes(abar_im)
    y_dir = [_s5_direction(u_ctx, u_lat, w_drive[i], w_read[i], abar_re[i], abar_im[i], reverse=bool(i))
             for i in range(2)]
    ffn1 = _ffn_weights(ffn_w_gate[1], ffn_w_up[1], ffn_w_down[1])
    return _post_call(_post1_kernel, "s5_glu_ffn_norm", h, mods_lat[1], y_dir,
                      [g_mix, g_ffn, vec(final_g), vec(ssm_d[0])],
                      [ssm_glu_w[0].astype(BF16), *ffn1], S5_POST_TILE)
```

```python
import functools
import math

import numpy as np
import jax
import jax.numpy as jnp
from jax import lax
from jax.experimental import pallas as pl
from jax.experimental.pallas import tpu as pltpu

F32 = jnp.float32
BF16 = jnp.bfloat16

D_MODEL = 1024
GRID_W = 64
N_MOD = 6
N_Q_HEADS = 8
N_KV_HEADS = 2
HEAD_DIM = 64
ATTN_WIDTH = N_Q_HEADS * HEAD_DIM
KV_WIDTH = N_KV_HEADS * HEAD_DIM
FOURIER_GROUPS = 8
FOURIER_GROUP_W = 64
FOURIER_WIDTH = FOURIER_GROUPS * FOURIER_GROUP_W
KV_START = FOURIER_WIDTH + ATTN_WIDTH
MIX_IN_WIDTH = KV_START + 2 * KV_WIDTH
WINDOW = 128
BLOCK = 128
ATTN_SCALE = HEAD_DIM ** -0.5
ROPE_BASE = 10000.0
NEG_INF = -1e30
SSM_GROUP_W = 16
SSM_GROUPS = D_MODEL // SSM_GROUP_W
SSM_STATE = 64
SSM_WIDTH = SSM_GROUPS * SSM_STATE
RMS_EPS = 1e-6

LANES = 128
SUBLANES = 8
VMEM_LIMIT_BYTES = 56 * 1024 * 1024

TOKEN_TILE = 512
FFN_CHUNK = 256
MOD_TILE = 1536
DFT_ROW_TILE = 1024
DFT_K_TILE = 2048
S5_CHUNK = 256
S5_SLAB_GROUPS = LANES // SSM_GROUP_W
S5_SLABS = SSM_GROUPS // S5_SLAB_GROUPS
S5_SLAB_STATES = S5_SLAB_GROUPS * SSM_STATE
S5_COLS = SSM_WIDTH // LANES
S5_PLANES = S5_COLS // SUBLANES


def _cparams(*sem):
    return pltpu.CompilerParams(dimension_semantics=sem, vmem_limit_bytes=VMEM_LIMIT_BYTES)


def _sigmoid(x):
    return 1.0 / (1.0 + jnp.exp(-x))


def _norm_mod(x, g, scale, shift):
    y = x * lax.rsqrt(jnp.mean(x * x, axis=-1, keepdims=True) + RMS_EPS)
    return (y * g) * (1.0 + scale) + shift


def _split_bf16(x):
    hi = x.astype(BF16)
    lo = (x - hi.astype(F32)).astype(BF16)
    return hi, lo


def _mod_kernel(cond_ref, w_ref, b_ref, o_ref):
    x = cond_ref[...]
    s_hi, s_lo = _split_bf16(x * _sigmoid(x))
    w_hi, w_lo = _split_bf16(w_ref[0])
    acc = jnp.dot(s_hi, w_hi, preferred_element_type=F32)
    acc += jnp.dot(s_lo, w_hi, preferred_element_type=F32)
    acc += jnp.dot(s_hi, w_lo, preferred_element_type=F32)
    o_ref[0] = acc + b_ref[0]


def _modulations(cond, mod_w, mod_b):
    depth, d, n = mod_w.shape
    rows = cond.shape[0]
    return pl.pallas_call(
        _mod_kernel,
        grid=(depth, n // MOD_TILE),
        in_specs=[
            pl.BlockSpec((rows, d), lambda l, j: (0, 0)),
            pl.BlockSpec((1, d, MOD_TILE), lambda l, j: (l, 0, j)),
            pl.BlockSpec((1, 1, MOD_TILE), lambda l, j: (l, 0, j)),
        ],
        out_specs=pl.BlockSpec((1, rows, MOD_TILE), lambda l, j: (l, 0, j)),
        out_shape=jax.ShapeDtypeStruct((depth, rows, n), F32),
        compiler_params=_cparams("parallel", "parallel"),
        name="adaln_mod",
    )(cond, mod_w, mod_b.reshape(depth, 1, n))


def _pre0_kernel(x_ref, mod_ref, g_ref, w_ref, bcs_ref, cos_ref, sin_ref,
                 uv_ref, q_ref, k_ref, v_ref, *, rope):
    m = mod_ref[0]
    n = _norm_mod(x_ref[0], g_ref[...], m[1:2], m[0:1])
    p = jnp.dot(n.astype(BF16), w_ref[...], preferred_element_type=F32)
    uv = jnp.dot(p[:, :FOURIER_WIDTH].astype(BF16), bcs_ref[...], preferred_element_type=F32)
    uv_ref[0, 0] = uv[:, :FOURIER_WIDTH].astype(BF16)
    uv_ref[0, 1] = uv[:, FOURIER_WIDTH:].astype(BF16)

    def rot(x):
        if not rope:
            return x
        lane = lax.broadcasted_iota(jnp.int32, x.shape, 1)
        first = (lane & (HEAD_DIM // 2)) == 0
        partner = jnp.where(first, pltpu.roll(x, LANES - HEAD_DIM // 2, 1),
                            pltpu.roll(x, HEAD_DIM // 2, 1))
        return x * cos_ref[...] + partner * sin_ref[...]

    for j in range(ATTN_WIDTH // LANES):
        lo = FOURIER_WIDTH + j * LANES
        q_ref[0, :, j * LANES:(j + 1) * LANES] = rot(p[:, lo:lo + LANES]).astype(BF16)
    k_ref[0] = rot(p[:, KV_START:KV_START + KV_WIDTH]).astype(BF16)
    v_ref[0] = p[:, KV_START + KV_WIDTH:].astype(BF16)


def _pre0(x, mods, g, w_in, bcs, cos_t, sin_t, *, rope, rows):
    b, length, d = x.shape
    grid = (b, length // rows)
    row_map = lambda i, t: (i, t, 0)
    const2 = lambda i, t: (0, 0)
    return pl.pallas_call(
        functools.partial(_pre0_kernel, rope=rope),
        grid=grid,
        in_specs=[
            pl.BlockSpec((1, rows, d), row_map),
            pl.BlockSpec((1, N_MOD, d), lambda i, t: (i if mods.shape[0] > 1 else 0, 0, 0)),
            pl.BlockSpec((1, d), const2),
            pl.BlockSpec((d, MIX_IN_WIDTH), const2),
            pl.BlockSpec((FOURIER_WIDTH, 2 * FOURIER_WIDTH), const2),
            pl.BlockSpec((rows, LANES), lambda i, t: (t, 0)),
            pl.BlockSpec((rows, LANES), lambda i, t: (t, 0)),
        ],
        out_specs=[
            pl.BlockSpec((1, 2, rows, FOURIER_WIDTH), lambda i, t: (i, 0, t, 0)),
            pl.BlockSpec((1, rows, ATTN_WIDTH), row_map),
            pl.BlockSpec((1, rows, KV_WIDTH), row_map),
            pl.BlockSpec((1, rows, KV_WIDTH), row_map),
        ],
        out_shape=[
            jax.ShapeDtypeStruct((b, 2, length, FOURIER_WIDTH), BF16),
            jax.ShapeDtypeStruct((b, length, ATTN_WIDTH), BF16),
            jax.ShapeDtypeStruct((b, length, KV_WIDTH), BF16),
            jax.ShapeDtypeStruct((b, length, KV_WIDTH), BF16),
        ],
        compiler_params=_cparams("parallel", "parallel"),
        name="mix_in_rope" if rope else "mix_in_ctx",
    )(x, mods, g, w_in, bcs, cos_t, sin_t)


def _dft_kernel(c_ref, s_ref, u_ref, v_ref, o_ref, acc_ref):
    k = pl.program_id(2)

    @pl.when(k == 0)
    def _():
        acc_ref[...] = jnp.zeros_like(acc_ref)

    acc_ref[...] += (jnp.dot(c_ref[...], u_ref[0, 0], preferred_element_type=F32)
                     + jnp.dot(s_ref[...], v_ref[0, 0], preferred_element_type=F32))

    @pl.when(k == pl.num_programs(2) - 1)
    def _():
        o_ref[0] = acc_ref[...].astype(BF16)


def _time_dft(cmat, smat, uv):
    b, _, t, w = uv.shape
    tm = min(DFT_ROW_TILE, t)
    tk = min(DFT_K_TILE, t)
    mat_spec = pl.BlockSpec((tm, tk), lambda i, r, k: (r, k))
    return pl.pallas_call(
        _dft_kernel,
        grid=(b, t // tm, t // tk),
        in_specs=[
            mat_spec, mat_spec,
            pl.BlockSpec((1, 1, tk, w), lambda i, r, k: (i, 0, k, 0)),
            pl.BlockSpec((1, 1, tk, w), lambda i, r, k: (i, 1, k, 0)),
        ],
        out_specs=pl.BlockSpec((1, tm, w), lambda i, r, k: (i, r, 0)),
        out_shape=jax.ShapeDtypeStruct((b, t, w), BF16),
        scratch_shapes=[pltpu.VMEM((tm, w), F32)],
        compiler_params=_cparams("parallel", "parallel", "arbitrary"),
        name="time_dft",
    )(cmat, smat, uv, uv)


def _attn_kernel(sink_ref, q_ref, *refs, window):
    if window:
        kp_ref, kc_ref, kn_ref, vp_ref, vc_ref, vn_ref, kx_ref, vx_ref, o_ref = refs
        kcat = jnp.concatenate([kx_ref[0], kp_ref[0], kc_ref[0], kn_ref[0]], axis=0)
        vcat = jnp.concatenate([vx_ref[0], vp_ref[0], vc_ref[0], vn_ref[0]], axis=0)
    else:
        kx_ref, vx_ref, o_ref = refs
        kcat, vcat = kx_ref[0], vx_ref[0]
    n_keys = kcat.shape[0]
    n_q = q_ref.shape[1]

    lane = lax.broadcasted_iota(jnp.int32, kcat.shape, 1)
    low = lane < HEAD_DIM
    zero = jnp.zeros_like(kcat)
    kswap = pltpu.roll(kcat.astype(F32), HEAD_DIM, 1).astype(BF16)
    vswap = pltpu.roll(vcat.astype(F32), HEAD_DIM, 1).astype(BF16)
    k_lo = (jnp.where(low, kcat, zero), jnp.where(low, kswap, zero))
    k_hi = (jnp.where(low, zero, kswap), jnp.where(low, zero, kcat))
    v_lo = (jnp.where(low, vcat, zero), jnp.where(low, vswap, zero))
    v_hi = (jnp.where(low, zero, vswap), jnp.where(low, zero, vcat))

    if window:
        blk = pl.program_id(1)
        last = pl.num_programs(1) - 1
        n_ctx = kx_ref.shape[1]
        row = lax.broadcasted_iota(jnp.int32, (n_q, n_keys), 0)
        col = lax.broadcasted_iota(jnp.int32, (n_q, n_keys), 1)
        in_prev = (col >= n_ctx) & (col < n_ctx + BLOCK)
        in_next = col >= n_ctx + 2 * BLOCK
        off_prev = jnp.where(blk > 0, 0, BLOCK)
        off_next = jnp.where(blk < last, 0, BLOCK)
        ok_prev = col - n_ctx >= row + off_prev
        ok_next = col - (n_ctx + 2 * BLOCK) <= row - off_next
        valid = (jnp.logical_not(in_prev) | ok_prev) & (jnp.logical_not(in_next) | ok_next)

    q_per_kv = N_Q_HEADS // N_KV_HEADS
    for j in range(ATTN_WIDTH // LANES):
        g = (2 * j) // q_per_kv
        qp = q_ref[0, :, j * LANES:(j + 1) * LANES]
        out = None
        for half, (k_ext, v_ext) in enumerate(((k_lo[g], v_lo[g]), (k_hi[g], v_hi[g]))):
            s = lax.dot_general(qp, k_ext, (((1,), (1,)), ((), ())),
                                preferred_element_type=F32) * ATTN_SCALE
            if window:
                s = jnp.where(valid, s, NEG_INF)
            sink = sink_ref[2 * j + half]
            mx = jnp.maximum(jnp.max(s, axis=1, keepdims=True), sink)
            p = jnp.exp(s - mx)
            denom = jnp.sum(p, axis=1, keepdims=True) + jnp.exp(sink - mx)
            o = jnp.dot(p.astype(BF16), v_ext, preferred_element_type=F32) * (1.0 / denom)
            out = o if out is None else out + o
        o_ref[0, :, j * LANES:(j + 1) * LANES] = out.astype(BF16)


def _window_attention(sink, q, k, v, kx, vx):
    b, length, _ = q.shape
    nb = length // BLOCK
    n_ctx = kx.shape[1]
    kv_spec = lambda f: pl.BlockSpec((1, BLOCK, KV_WIDTH), f)
    prev = lambda i, n: (i, jnp.maximum(n - 1, 0), 0)
    cur = lambda i, n: (i, n, 0)
    nxt = lambda i, n: (i, jnp.minimum(n + 1, nb - 1), 0)
    ctx_spec = pl.BlockSpec((1, n_ctx, KV_WIDTH), lambda i, n: (i, 0, 0))
    return pl.pallas_call(
        functools.partial(_attn_kernel, window=True),
        grid=(b, nb),
        in_specs=[
            pl.BlockSpec(memory_space=pltpu.SMEM),
            pl.BlockSpec((1, BLOCK, ATTN_WIDTH), cur),
            kv_spec(prev), kv_spec(cur), kv_spec(nxt),
            kv_spec(prev), kv_spec(cur), kv_spec(nxt),
            ctx_spec, ctx_spec,
        ],
        out_specs=pl.BlockSpec((1, BLOCK, ATTN_WIDTH), cur),
        out_shape=jax.ShapeDtypeStruct((b, length, ATTN_WIDTH), BF16),
        compiler_params=_cparams("parallel", "parallel"),
        name="window_attn",
    )(sink, q, k, k, k, v, v, v, kx, vx)


def _context_attention(sink, q, kx, vx):
    b, n_ctx, _ = q.shape
    spec = lambda w: pl.BlockSpec((1, n_ctx, w), lambda i: (i, 0, 0))
    return pl.pallas_call(
        functools.partial(_attn_kernel, window=False),
        grid=(b,),
        in_specs=[pl.BlockSpec(memory_space=pltpu.SMEM), spec(ATTN_WIDTH), spec(KV_WIDTH), spec(KV_WIDTH)],
        out_specs=spec(ATTN_WIDTH),
        out_shape=jax.ShapeDtypeStruct((b, n_ctx, ATTN_WIDTH), BF16),
        compiler_params=_cparams("parallel"),
        name="context_attn",
    )(sink, q, kx, vx)


def _swiglu_residual(h1, m, g2, wg_ref, wu_ref, wd_ref, acc_ref):
    n2 = _norm_mod(h1, g2, m[4:5], m[3:4]).astype(BF16)
    for c in range(wg_ref.shape[1] // FFN_CHUNK):
        cols = slice(c * FFN_CHUNK, (c + 1) * FFN_CHUNK)
        gate = jnp.dot(n2, wg_ref[:, cols], preferred_element_type=F32)
        up = jnp.dot(n2, wu_ref[:, cols], preferred_element_type=F32)
        act = (gate * _sigmoid(gate)) * up
        part = jnp.dot(act.astype(BF16), wd_ref[cols, :], preferred_element_type=F32)
        if c == 0:
            acc_ref[...] = part
        else:
            acc_ref[...] += part
    return h1 + m[5:6] * acc_ref[...]


def _post0_kernel(h_ref, mod_ref, g2_ref, f_ref, a_ref, wo_ref, wg_ref, wu_ref, wd_ref, o_ref, acc_ref):
    m = mod_ref[0]
    mix = jnp.dot(f_ref[0], wo_ref[:FOURIER_WIDTH, :], preferred_element_type=F32)
    mix += jnp.dot(a_ref[0], wo_ref[FOURIER_WIDTH:, :], preferred_element_type=F32)
    h1 = h_ref[0] + m[2:3] * mix
    o_ref[0] = _swiglu_residual(h1, m, g2_ref[...], wg_ref, wu_ref, wd_ref, acc_ref)


def _gelu_tanh(x):
    sqrt_2_over_pi = np.sqrt(2 / np.pi).astype(np.float32)
    cdf = 0.5 * (1.0 + jnp.tanh(sqrt_2_over_pi * (x + 0.044715 * (x ** 3))))
    return x * cdf


def _post1_kernel(h_ref, mod_ref, g1_ref, g2_ref, gf_ref, dskip_ref, yf_ref, yb_ref, wglu_ref,
                  wg_ref, wu_ref, wd_ref, o_ref, acc_ref):
    m = mod_ref[0]
    h = h_ref[0]
    u = _norm_mod(h, g1_ref[...], m[1:2], m[0:1])
    y = dskip_ref[...] * u + yf_ref[0] + yb_ref[0]
    z = jnp.dot(_gelu_tanh(y).astype(BF16), wglu_ref[...], preferred_element_type=F32)
    h1 = h + m[2:3] * (z[:, :D_MODEL] * _sigmoid(z[:, D_MODEL:]))
    h2 = _swiglu_residual(h1, m, g2_ref[...], wg_ref, wu_ref, wd_ref, acc_ref)
    o_ref[0] = (h2 * lax.rsqrt(jnp.mean(h2 * h2, axis=-1, keepdims=True) + RMS_EPS)) * gf_ref[...]


def _resident(shape):
    zeros = (0,) * len(shape)
    return pl.BlockSpec(shape, lambda i, t: zeros, pipeline_mode=pl.Buffered(1))


def _post_call(kernel, name, h, mods, row_inputs, vec_inputs, weights, rows):
    b, length, d = h.shape
    row_map = lambda i, t: (i, t, 0)
    in_specs = [pl.BlockSpec((1, rows, d), row_map),
                pl.BlockSpec((1, N_MOD, d), lambda i, t: (i if mods.shape[0] > 1 else 0, 0, 0))]
    in_specs += [pl.BlockSpec((1, d), lambda i, t: (0, 0)) for _ in vec_inputs]
    in_specs += [pl.BlockSpec((1, rows, a.shape[-1]), row_map) for a in row_inputs]
    in_specs += [_resident(w.shape) for w in weights]
    return pl.pallas_call(
        kernel,
        grid=(b, length // rows),
        in_specs=in_specs,
        out_specs=pl.BlockSpec((1, rows, d), row_map),
        out_shape=jax.ShapeDtypeStruct((b, length, d), F32),
        scratch_shapes=[pltpu.VMEM((rows, d), F32)],
        compiler_params=_cparams("parallel", "parallel"),
        name=name,
    )(h, mods, *vec_inputs, *row_inputs, *weights)


def _pre1_kernel(x_ref, mod_ref, g_ref, o_ref):
    m = mod_ref[0]
    o_ref[0] = _norm_mod(x_ref[0], g_ref[...], m[1:2], m[0:1]).astype(BF16)


def _pre1(x, mods, g, rows):
    b, length, d = x.shape
    row_map = lambda i, t: (i, t, 0)
    return pl.pallas_call(
        _pre1_kernel,
        grid=(b, length // rows),
        in_specs=[pl.BlockSpec((1, rows, d), row_map),
                  pl.BlockSpec((1, N_MOD, d), lambda i, t: (i if mods.shape[0] > 1 else 0, 0, 0)),
                  pl.BlockSpec((1, d), lambda i, t: (0, 0))],
        out_specs=pl.BlockSpec((1, rows, d), row_map),
        out_shape=jax.ShapeDtypeStruct((b, length, d), BF16),
        compiler_params=_cparams("parallel", "parallel"),
        name="s5_input_norm",
    )(x, mods, g)


def _disc_kernel(are_ref, aim_ref, ldt_ref, bre_ref, bim_ref, oar_ref, oai_ref, obr_ref, obi_ref):
    a_re, a_im = are_ref[0], aim_ref[0]
    dt = jnp.exp(ldt_ref[0])
    mag = jnp.exp(a_re * dt)
    abar_re, abar_im = mag * jnp.cos(a_im * dt), mag * jnp.sin(a_im * dt)
    nr, ni = abar_re - 1.0, abar_im
    den = a_re * a_re + a_im * a_im
    f_re = (nr * a_re + ni * a_im) / den
    f_im = (ni * a_re - nr * a_im) / den
    oar_ref[0] = abar_re
    oai_ref[0] = abar_im
    obr_ref[0] = f_re[None] * bre_ref[0] - f_im[None] * bim_ref[0]
    obi_ref[0] = f_re[None] * bim_ref[0] + f_im[None] * bre_ref[0]


def _discretize(a_re, a_im, log_dt, b_re, b_im):
    nd = a_re.shape[0]
    rows = SSM_WIDTH // LANES
    flat = lambda a: a.reshape(nd, rows, LANES)
    ldt = jnp.broadcast_to(log_dt[:, :, None], (nd, SSM_GROUPS, SSM_STATE))
    chan = lambda a: a.reshape(nd, SSM_WIDTH, SSM_GROUP_W).transpose(0, 2, 1).reshape(nd, SSM_GROUP_W, rows, LANES)
    aspec = pl.BlockSpec((1, rows, LANES), lambda i: (i, 0, 0))
    bspec = pl.BlockSpec((1, SSM_GROUP_W, rows, LANES), lambda i: (i, 0, 0, 0))
    ashape = jax.ShapeDtypeStruct((nd, rows, LANES), F32)
    bshape = jax.ShapeDtypeStruct((nd, SSM_GROUP_W, rows, LANES), F32)
    return pl.pallas_call(
        _disc_kernel,
        grid=(nd,),
        in_specs=[aspec, aspec, aspec, bspec, bspec],
        out_specs=[aspec, aspec, bspec, bspec],
        out_shape=[ashape, ashape, bshape, bshape],
        compiler_params=_cparams("parallel"),
        name="s5_discretize",
    )(flat(a_re), flat(a_im), flat(ldt), chan(b_re), chan(b_im))


def _s5_kernel(uc_ref, ul_ref, wd_ref, wr_ref, are_ref, aim_ref, y_ref, xre, xim, hre, him, *, reverse):
    step_idx = pl.program_id(1)
    steps = uc_ref.shape[1]

    @pl.when(step_idx == 0)
    def _():
        hre[...] = jnp.zeros_like(hre)
        him[...] = jnp.zeros_like(him)

    def plane_of(col):
        return col // SUBLANES, col % SUBLANES

    def run(u, readout):
        cols_per_slab = S5_SLAB_STATES // LANES
        for kk in range(S5_SLABS):
            bu = jnp.dot(u[:, kk * LANES:(kk + 1) * LANES], wd_ref[kk], preferred_element_type=F32)
            for c in range(cols_per_slab):
                pln, sub = plane_of(kk * cols_per_slab + c)
                xre[pln, pl.ds(sub, steps, stride=SUBLANES), :] = bu[:, c * LANES:(c + 1) * LANES]
                xim[pln, pl.ds(sub, steps, stride=SUBLANES), :] = (
                    bu[:, S5_SLAB_STATES + c * LANES:S5_SLAB_STATES + (c + 1) * LANES])

        a_r = [are_ref[p] for p in range(S5_PLANES)]
        a_i = [aim_ref[p] for p in range(S5_PLANES)]

        def step(i, carry):
            h_r, h_i = carry
            t = (steps - 1 - i) if reverse else i
            r0 = pl.multiple_of(t * SUBLANES, SUBLANES)
            new_r, new_i = [], []
            for p in range(S5_PLANES):
                n_r = a_r[p] * h_r[p] - a_i[p] * h_i[p] + xre[p, pl.ds(r0, SUBLANES), :]
                n_i = a_r[p] * h_i[p] + a_i[p] * h_r[p] + xim[p, pl.ds(r0, SUBLANES), :]
                xre[p, pl.ds(r0, SUBLANES), :] = n_r
                xim[p, pl.ds(r0, SUBLANES), :] = n_i
                new_r.append(n_r)
                new_i.append(n_i)
            return tuple(new_r), tuple(new_i)

        init = (tuple(hre[p] for p in range(S5_PLANES)), tuple(him[p] for p in range(S5_PLANES)))
        h_r, h_i = lax.fori_loop(0, steps, step, init, unroll=8)
        for p in range(S5_PLANES):
            hre[p] = h_r[p]
            him[p] = h_i[p]

        if readout:
            for kk in range(S5_SLABS):
                parts = []
                for src in (xre, xim):
                    for c in range(cols_per_slab):
                        pln, sub = plane_of(kk * cols_per_slab + c)
                        parts.append(src[pln, pl.ds(sub, steps, stride=SUBLANES), :].astype(BF16))
                lhs = jnp.concatenate(parts, axis=1)
                y_ref[0, :, kk * LANES:(kk + 1) * LANES] = jnp.dot(
                    lhs, wr_ref[kk], preferred_element_type=F32)

    @pl.when(step_idx == 0)
    def _():
        run(uc_ref[0], readout=False)

    @pl.when(step_idx > 0)
    def _():
        run(ul_ref[0], readout=True)


def _s5_direction(u_ctx, u_lat, w_drive, w_read, a_re, a_im, *, reverse):
    b, length, d = u_lat.shape
    n_ctx = u_ctx.shape[1]
    assert n_ctx == S5_CHUNK and length % S5_CHUNK == 0
    n_chunks = length // S5_CHUNK
    if reverse:
        lat_map = lambda i, n: (i, jnp.minimum(n_chunks - n, n_chunks - 1), 0)
    else:
        lat_map = lambda i, n: (i, jnp.maximum(n - 1, 0), 0)
    whole = lambda shape: pl.BlockSpec(shape, lambda i, n: (0,) * len(shape))
    return pl.pallas_call(
        functools.partial(_s5_kernel, reverse=reverse),
        grid=(b, n_chunks + 1),
        in_specs=[
            pl.BlockSpec((1, n_ctx, d), lambda i, n: (i, 0, 0)),
            pl.BlockSpec((1, S5_CHUNK, d), lat_map),
            whole(w_drive.shape), whole(w_read.shape), whole(a_re.shape), whole(a_im.shape),
        ],
        out_specs=pl.BlockSpec((1, S5_CHUNK, d), lat_map),
        out_shape=jax.ShapeDtypeStruct((b, length, d), F32),
        scratch_shapes=[
            pltpu.VMEM((S5_PLANES, S5_CHUNK * SUBLANES, LANES), F32),
            pltpu.VMEM((S5_PLANES, S5_CHUNK * SUBLANES, LANES), F32),
            pltpu.VMEM((S5_PLANES, SUBLANES, LANES), F32),
            pltpu.VMEM((S5_PLANES, SUBLANES, LANES), F32),
        ],
        compiler_params=_cparams("parallel", "arbitrary"),
        name="s5_scan_bwd" if reverse else "s5_scan_fwd",
    )(u_ctx, u_lat, w_drive, w_read, a_re, a_im)


def _rope_tables(length):
    rows = length // GRID_W
    t_row = jnp.repeat(jnp.arange(rows, dtype=F32), GRID_W)
    t_col = jnp.tile(jnp.arange(GRID_W, dtype=F32), rows)
    n_freq = HEAD_DIM // 4
    inv_freq = ROPE_BASE ** (-jnp.arange(n_freq, dtype=F32) / n_freq)
    ang = jnp.concatenate([t_row[:, None] * inv_freq, t_col[:, None] * inv_freq], axis=-1)
    cos, sin = jnp.cos(ang), jnp.sin(ang)
    reps = LANES // HEAD_DIM
    return (jnp.tile(jnp.concatenate([cos, cos], axis=-1), (1, reps)),
            jnp.tile(jnp.concatenate([-sin, sin], axis=-1), (1, reps)))


def _half_split_perm():
    perm = np.arange(MIX_IN_WIDTH)
    head = np.concatenate([np.arange(0, HEAD_DIM, 2), np.arange(1, HEAD_DIM, 2)])
    for start in range(FOURIER_WIDTH, KV_START + KV_WIDTH, HEAD_DIM):
        perm[start:start + HEAD_DIM] = start + head
    return perm


def _channel_dft():
    c = np.arange(FOURIER_GROUP_W)
    ang = 2.0 * np.pi * np.outer(c, c) / FOURIER_GROUP_W
    eye = np.eye(FOURIER_GROUPS)
    scale = FOURIER_GROUP_W ** -0.5
    table = np.concatenate([np.kron(eye, np.cos(ang) * scale), np.kron(eye, np.sin(ang) * scale)], axis=1)
    return jnp.asarray(table, F32).astype(BF16)


def _time_dft_matrices(t):
    r1 = 1 << ((t.bit_length() - 1 + 1) // 2)
    r2 = t // r1
    assert r1 * r2 == t
    k = jnp.arange(t, dtype=jnp.int32)[:, None]
    unit = 2.0 * math.pi / t
    ang_a = ((k * (jnp.arange(r1, dtype=jnp.int32) * r2)[None, :]) % t).astype(F32) * unit
    ang_b = ((k * jnp.arange(r2, dtype=jnp.int32)[None, :]) % t).astype(F32) * unit
    scale = t ** -0.5
    ca, sa = jnp.cos(ang_a)[:, :, None], jnp.sin(ang_a)[:, :, None]
    cb, sb = (jnp.cos(ang_b) * scale)[:, None, :], (jnp.sin(ang_b) * scale)[:, None, :]
    cmat = (ca * cb - sa * sb).reshape(t, t).astype(BF16)
    smat = (-(sa * cb + ca * sb)).reshape(t, t).astype(BF16)
    return cmat, smat


def _ffn_weights(w_gate, w_up, w_down):
    return w_gate.astype(BF16), w_up.astype(BF16), w_down.astype(BF16)


def _s5_weights(bbar_re, bbar_im, c_re, c_im):
    nd = bbar_re.shape[0]
    eye = jnp.eye(S5_SLAB_GROUPS, dtype=F32)

    def drive(bb):
        bb = bb.reshape(nd, SSM_GROUP_W, S5_SLABS, S5_SLAB_GROUPS, SSM_STATE)
        w = jnp.einsum("dhkgp,gG->dkghGp", bb, eye)
        return w.reshape(nd, S5_SLABS, LANES, S5_SLAB_STATES)

    def read(cc):
        cc = cc.astype(F32).reshape(nd, S5_SLABS, S5_SLAB_GROUPS, SSM_GROUP_W, SSM_STATE)
        w = jnp.einsum("dkghp,gG->dkgpGh", cc, eye)
        return w.reshape(nd, S5_SLABS, S5_SLAB_STATES, LANES)

    w_drive = jnp.concatenate([drive(bbar_re), drive(bbar_im)], axis=-1).astype(BF16)
    w_read = jnp.concatenate([read(c_re), -read(c_im)], axis=2).astype(BF16)
    return w_drive, w_read


def kernel(x, c, ctx, c_ctx, mod_w, mod_b, norm_g, ffn_w_gate, ffn_w_up, ffn_w_down, mix_w_in, mix_w_out,
           attn_sink, ssm_a_re, ssm_a_im, ssm_log_dt, ssm_b_re, ssm_b_im, ssm_c_re, ssm_c_im, ssm_d,
           ssm_glu_w, final_g):
    bn, length, d = x.shape
    n_ctx = ctx.shape[1]
    depth = mod_w.shape[0]
    assert depth == 2 and d == D_MODEL

    cond = jnp.zeros((SUBLANES, d), F32).at[:bn].set(c).at[bn].set(c_ctx)
    mods = _modulations(cond, mod_w, mod_b).reshape(depth, SUBLANES, N_MOD, d)
    mods_lat, mods_ctx = mods[:, :bn], mods[:, bn:bn + 1]
    vec = lambda a: a.reshape(1, d)

    w_in = mix_w_in[0][:, _half_split_perm()].astype(BF16)
    w_out = mix_w_out[0].astype(BF16)
    bcs = _channel_dft()
    cos_t, sin_t = _rope_tables(length)
    ffn0 = _ffn_weights(ffn_w_gate[0], ffn_w_up[0], ffn_w_down[0])
    g_mix, g_ffn = vec(norm_g[0, 0]), vec(norm_g[0, 1])

    uv, q, k, v = _pre0(x, mods_lat[0], g_mix, w_in, bcs, cos_t, sin_t, rope=True, rows=TOKEN_TILE)
    uvc, qc, kc, vc = _pre0(ctx, mods_ctx[0], g_mix, w_in, bcs, cos_t, sin_t, rope=False, rows=n_ctx)
    four = _time_dft(*_time_dft_matrices(length), uv)
    four_c = _time_dft(*_time_dft_matrices(n_ctx), uvc)
    att = _window_attention(attn_sink[0], q, k, v, kc, vc)
    att_c = _context_attention(attn_sink[0], qc, kc, vc)
    h = _post_call(_post0_kernel, "mix_out_ffn", x, mods_lat[0], [four, att], [g_ffn],
                   [w_out, *ffn0], TOKEN_TILE)
    hc = _post_call(_post0_kernel, "mix_out_ffn_ctx", ctx, mods_ctx[0], [four_c, att_c], [g_ffn],
                    [w_out, *ffn0], n_ctx)

    g_mix, g_ffn = vec(norm_g[1, 0]), vec(norm_g[1, 1])
    u_lat = _pre1(h, mods_lat[1], g_mix, TOKEN_TILE)
    u_ctx = _pre1(hc, mods_ctx[1], g_mix, n_ctx)
    abar_re, abar_im, bbar_re, bbar_im = _discretize(
        ssm_a_re[0], ssm_a_im[0], ssm_log_dt[0], ssm_b_re[0], ssm_b_im[0])
    w_drive, w_read = _s5_weights(bbar_re, bbar_im, ssm_c_re[0], ssm_c_im[0])
    planes = lambda a: a.reshape(2, S5_PLANES, SUBLANES, LANES)
    abar_re, abar_im = planes(abar_re), planes(abar_im)
    y_dir = [_s5_direction(u_ctx, u_lat, w_drive[i], w_read[i], abar_re[i], abar_im[i], reverse=bool(i))
             for i in range(2)]
    ffn1 = _ffn_weights(ffn_w_gate[1], ffn_w_up[1], ffn_w_down[1])
    return _post_call(_post1_kernel, "s5_glu_ffn_norm", h, mods_lat[1], y_dir,
                      [g_mix, g_ffn, vec(final_g), vec(ssm_d[0])],
                      [ssm_glu_w[0].astype(BF16), *ffn1], TOKEN_TILE)
```

```python
import functools
import math

import numpy as np
import jax
import jax.numpy as jnp
from jax import lax
from jax.experimental import pallas as pl
from jax.experimental.pallas import tpu as pltpu

F32 = jnp.float32
BF16 = jnp.bfloat16

D_MODEL = 1024
GRID_W = 64
N_MOD = 6
N_Q_HEADS = 8
N_KV_HEADS = 2
HEAD_DIM = 64
ATTN_WIDTH = N_Q_HEADS * HEAD_DIM
KV_WIDTH = N_KV_HEADS * HEAD_DIM
FOURIER_GROUPS = 8
FOURIER_GROUP_W = 64
FOURIER_WIDTH = FOURIER_GROUPS * FOURIER_GROUP_W
KV_START = FOURIER_WIDTH + ATTN_WIDTH
MIX_IN_WIDTH = KV_START + 2 * KV_WIDTH
WINDOW = 128
BLOCK = 128
ATTN_SCALE = HEAD_DIM ** -0.5
ROPE_BASE = 10000.0
NEG_INF = -1e30
SSM_GROUP_W = 16
SSM_GROUPS = D_MODEL // SSM_GROUP_W
SSM_STATE = 64
SSM_WIDTH = SSM_GROUPS * SSM_STATE
RMS_EPS = 1e-6

LANES = 128
SUBLANES = 8
VMEM_LIMIT_BYTES = 56 * 1024 * 1024

TOKEN_TILE = 512
FFN_CHUNK = 256
MOD_TILE = 1536
DFT_ROW_TILE = 1024
DFT_K_TILE = 2048
S5_CHUNK = 256
S5_SLAB_GROUPS = LANES // SSM_GROUP_W
S5_SLABS = SSM_GROUPS // S5_SLAB_GROUPS
S5_SLAB_STATES = S5_SLAB_GROUPS * SSM_STATE
S5_COLS = SSM_WIDTH // LANES
S5_PLANES = S5_COLS // SUBLANES


def _cparams(*sem):
    return pltpu.CompilerParams(dimension_semantics=sem, vmem_limit_bytes=VMEM_LIMIT_BYTES)


def _sigmoid(x):
    return 1.0 / (1.0 + jnp.exp(-x))


def _norm_mod(x, g, scale, shift):
    y = x * lax.rsqrt(jnp.mean(x * x, axis=-1, keepdims=True) + RMS_EPS)
    return (y * g) * (1.0 + scale) + shift


def _split_bf16(x):
    hi = x.astype(BF16)
    lo = (x - hi.astype(F32)).astype(BF16)
    return hi, lo


def _mod_kernel(cond_ref, w_ref, b_ref, o_ref):
    x = cond_ref[...]
    s_hi, s_lo = _split_bf16(x * _sigmoid(x))
    w_hi, w_lo = _split_bf16(w_ref[0])
    acc = jnp.dot(s_hi, w_hi, preferred_element_type=F32)
    acc += jnp.dot(s_lo, w_hi, preferred_element_type=F32)
    acc += jnp.dot(s_hi, w_lo, preferred_element_type=F32)
    o_ref[0] = acc + b_ref[0]


def _modulations(cond, mod_w, mod_b):
    depth, d, n = mod_w.shape
    rows = cond.shape[0]
    return pl.pallas_call(
        _mod_kernel,
        grid=(depth, n // MOD_TILE),
        in_specs=[
            pl.BlockSpec((rows, d), lambda l, j: (0, 0)),
            pl.BlockSpec((1, d, MOD_TILE), lambda l, j: (l, 0, j)),
            pl.BlockSpec((1, 1, MOD_TILE), lambda l, j: (l, 0, j)),
        ],
        out_specs=pl.BlockSpec((1, rows, MOD_TILE), lambda l, j: (l, 0, j)),
        out_shape=jax.ShapeDtypeStruct((depth, rows, n), F32),
        compiler_params=_cparams("parallel", "parallel"),
        name="adaln_mod",
    )(cond, mod_w, mod_b.reshape(depth, 1, n))


def _pre0_kernel(x_ref, mod_ref, g_ref, w_ref, bcs_ref, cos_ref, sin_ref,
                 uv_ref, q_ref, k_ref, v_ref, *, rope):
    m = mod_ref[0]
    n = _norm_mod(x_ref[0], g_ref[...], m[1:2], m[0:1])
    p = jnp.dot(n.astype(BF16), w_ref[...], preferred_element_type=F32)
    uv = jnp.dot(p[:, :FOURIER_WIDTH].astype(BF16), bcs_ref[...], preferred_element_type=F32)
    uv_ref[0, 0] = uv[:, :FOURIER_WIDTH].astype(BF16)
    uv_ref[0, 1] = uv[:, FOURIER_WIDTH:].astype(BF16)

    def rot(x):
        if not rope:
            return x
        lane = lax.broadcasted_iota(jnp.int32, x.shape, 1)
        first = (lane & (HEAD_DIM // 2)) == 0
        partner = jnp.where(first, pltpu.roll(x, LANES - HEAD_DIM // 2, 1),
                            pltpu.roll(x, HEAD_DIM // 2, 1))
        return x * cos_ref[...] + partner * sin_ref[...]

    for j in range(ATTN_WIDTH // LANES):
        lo = FOURIER_WIDTH + j * LANES
        q_ref[0, :, j * LANES:(j + 1) * LANES] = (rot(p[:, lo:lo + LANES]) * ATTN_SCALE).astype(BF16)
    k_ref[0] = rot(p[:, KV_START:KV_START + KV_WIDTH]).astype(BF16)
    v_ref[0] = p[:, KV_START + KV_WIDTH:].astype(BF16)


def _pre0(x, mods, g, w_in, bcs, cos_t, sin_t, *, rope, rows):
    b, length, d = x.shape
    grid = (b, length // rows)
    row_map = lambda i, t: (i, t, 0)
    const2 = lambda i, t: (0, 0)
    return pl.pallas_call(
        functools.partial(_pre0_kernel, rope=rope),
        grid=grid,
        in_specs=[
            pl.BlockSpec((1, rows, d), row_map),
            pl.BlockSpec((1, N_MOD, d), lambda i, t: (i if mods.shape[0] > 1 else 0, 0, 0)),
            pl.BlockSpec((1, d), const2),
            pl.BlockSpec((d, MIX_IN_WIDTH), const2),
            pl.BlockSpec((FOURIER_WIDTH, 2 * FOURIER_WIDTH), const2),
            pl.BlockSpec((rows, LANES), lambda i, t: (t, 0)),
            pl.BlockSpec((rows, LANES), lambda i, t: (t, 0)),
        ],
        out_specs=[
            pl.BlockSpec((1, 2, rows, FOURIER_WIDTH), lambda i, t: (i, 0, t, 0)),
            pl.BlockSpec((1, rows, ATTN_WIDTH), row_map),
            pl.BlockSpec((1, rows, KV_WIDTH), row_map),
            pl.BlockSpec((1, rows, KV_WIDTH), row_map),
        ],
        out_shape=[
            jax.ShapeDtypeStruct((b, 2, length, FOURIER_WIDTH), BF16),
            jax.ShapeDtypeStruct((b, length, ATTN_WIDTH), BF16),
            jax.ShapeDtypeStruct((b, length, KV_WIDTH), BF16),
            jax.ShapeDtypeStruct((b, length, KV_WIDTH), BF16),
        ],
        compiler_params=_cparams("parallel", "parallel"),
        name="mix_in_rope" if rope else "mix_in_ctx",
    )(x, mods, g, w_in, bcs, cos_t, sin_t)


def _dft_kernel(c_ref, s_ref, u_ref, v_ref, o_ref, acc_ref):
    k = pl.program_id(2)

    @pl.when(k == 0)
    def _():
        acc_ref[...] = jnp.zeros_like(acc_ref)

    acc_ref[...] += (jnp.dot(c_ref[...], u_ref[0, 0], preferred_element_type=F32)
                     + jnp.dot(s_ref[...], v_ref[0, 0], preferred_element_type=F32))

    @pl.when(k == pl.num_programs(2) - 1)
    def _():
        o_ref[0] = acc_ref[...].astype(BF16)


def _time_dft(cmat, smat, uv):
    b, _, t, w = uv.shape
    tm = min(DFT_ROW_TILE, t)
    tk = min(DFT_K_TILE, t)
    mat_spec = pl.BlockSpec((tm, tk), lambda i, r, k: (r, k))
    return pl.pallas_call(
        _dft_kernel,
        grid=(b, t // tm, t // tk),
        in_specs=[
            mat_spec, mat_spec,
            pl.BlockSpec((1, 1, tk, w), lambda i, r, k: (i, 0, k, 0)),
            pl.BlockSpec((1, 1, tk, w), lambda i, r, k: (i, 1, k, 0)),
        ],
        out_specs=pl.BlockSpec((1, tm, w), lambda i, r, k: (i, r, 0)),
        out_shape=jax.ShapeDtypeStruct((b, t, w), BF16),
        scratch_shapes=[pltpu.VMEM((tm, w), F32)],
        compiler_params=_cparams("parallel", "parallel", "arbitrary"),
        name="time_dft",
    )(cmat, smat, uv, uv)


def _attn_kernel(sink_ref, q_ref, *refs, window):
    if window:
        kp_ref, kc_ref, kn_ref, vp_ref, vc_ref, vn_ref, kx_ref, vx_ref, o_ref, s_scr, p_scr = refs
        kcat = jnp.concatenate([kx_ref[0], kp_ref[0], kc_ref[0], kn_ref[0]], axis=0)
        vcat = jnp.concatenate([vx_ref[0], vp_ref[0], vc_ref[0], vn_ref[0]], axis=0)
    else:
        kx_ref, vx_ref, o_ref, s_scr, p_scr = refs
        kcat, vcat = kx_ref[0], vx_ref[0]
    n_keys = kcat.shape[0]
    n_q = q_ref.shape[1]

    lane = lax.broadcasted_iota(jnp.int32, kcat.shape, 1)
    low = lane < HEAD_DIM
    zero = jnp.zeros_like(kcat)
    kswap = pltpu.roll(kcat.astype(F32), HEAD_DIM, 1).astype(BF16)
    vswap = pltpu.roll(vcat.astype(F32), HEAD_DIM, 1).astype(BF16)
    k_lo = (jnp.where(low, kcat, zero), jnp.where(low, kswap, zero))
    k_hi = (jnp.where(low, zero, kswap), jnp.where(low, zero, kcat))
    v_lo = (jnp.where(low, vcat, zero), jnp.where(low, vswap, zero))
    v_hi = (jnp.where(low, zero, vswap), jnp.where(low, zero, vcat))

    if window:
        blk = pl.program_id(1)
        last = pl.num_programs(1) - 1
        n_ctx = kx_ref.shape[1]
        row = lax.broadcasted_iota(jnp.int32, (n_q, BLOCK), 0)
        col = lax.broadcasted_iota(jnp.int32, (n_q, BLOCK), 1)
        off_prev = jnp.where(blk > 0, 0, BLOCK)
        off_next = jnp.where(blk < last, 0, BLOCK)
        ok_prev = col >= row + off_prev
        ok_next = col <= row - off_next

    q_per_kv = N_Q_HEADS // N_KV_HEADS
    n_pairs = ATTN_WIDTH // LANES
    for j in range(n_pairs):
        g = (2 * j) // q_per_kv
        qp = q_ref[0, :, j * LANES:(j + 1) * LANES]
        for half, k_ext in enumerate((k_lo[g], k_hi[g])):
            s = lax.dot_general(qp, k_ext, (((1,), (1,)), ((), ())), preferred_element_type=F32)
            head = 2 * j + half
            if window:
                s_scr[head, :, :n_ctx] = s[:, :n_ctx]
                s_scr[head, :, n_ctx:n_ctx + BLOCK] = jnp.where(ok_prev, s[:, n_ctx:n_ctx + BLOCK], NEG_INF)
                s_scr[head, :, n_ctx + BLOCK:n_ctx + 2 * BLOCK] = s[:, n_ctx + BLOCK:n_ctx + 2 * BLOCK]
                s_scr[head, :, n_ctx + 2 * BLOCK:] = jnp.where(ok_next, s[:, n_ctx + 2 * BLOCK:], NEG_INF)
            else:
                s_scr[head] = s
    inv = []
    for head in range(N_Q_HEADS):
        s = s_scr[head]
        sink = sink_ref[head]
        mx = jnp.maximum(jnp.max(s, axis=1, keepdims=True), sink)
        p = jnp.exp(s - mx)
        inv.append(1.0 / (jnp.sum(p, axis=1, keepdims=True) + jnp.exp(sink - mx)))
        p_scr[head] = p.astype(BF16)
    for j in range(n_pairs):
        g = (2 * j) // q_per_kv
        out = jnp.dot(p_scr[2 * j], v_lo[g], preferred_element_type=F32) * inv[2 * j]
        out += jnp.dot(p_scr[2 * j + 1], v_hi[g], preferred_element_type=F32) * inv[2 * j + 1]
        o_ref[0, :, j * LANES:(j + 1) * LANES] = out.astype(BF16)


def _attn_scratch(n_q, n_keys):
    return [pltpu.VMEM((N_Q_HEADS, n_q, n_keys), F32), pltpu.VMEM((N_Q_HEADS, n_q, n_keys), BF16)]


def _window_attention(sink, q, k, v, kx, vx):
    b, length, _ = q.shape
    nb = length // BLOCK
    n_ctx = kx.shape[1]
    kv_spec = lambda f: pl.BlockSpec((1, BLOCK, KV_WIDTH), f)
    prev = lambda i, n: (i, jnp.maximum(n - 1, 0), 0)
    cur = lambda i, n: (i, n, 0)
    nxt = lambda i, n: (i, jnp.minimum(n + 1, nb - 1), 0)
    ctx_spec = pl.BlockSpec((1, n_ctx, KV_WIDTH), lambda i, n: (i, 0, 0))
    return pl.pallas_call(
        functools.partial(_attn_kernel, window=True),
        grid=(b, nb),
        in_specs=[
            pl.BlockSpec(memory_space=pltpu.SMEM),
            pl.BlockSpec((1, BLOCK, ATTN_WIDTH), cur),
            kv_spec(prev), kv_spec(cur), kv_spec(nxt),
            kv_spec(prev), kv_spec(cur), kv_spec(nxt),
            ctx_spec, ctx_spec,
        ],
        out_specs=pl.BlockSpec((1, BLOCK, ATTN_WIDTH), cur),
        out_shape=jax.ShapeDtypeStruct((b, length, ATTN_WIDTH), BF16),
        scratch_shapes=_attn_scratch(BLOCK, n_ctx + 3 * BLOCK),
        compiler_params=_cparams("parallel", "parallel"),
        name="window_attn",
    )(sink, q, k, k, k, v, v, v, kx, vx)


def _context_attention(sink, q, kx, vx):
    b, n_ctx, _ = q.shape
    spec = lambda w: pl.BlockSpec((1, n_ctx, w), lambda i: (i, 0, 0))
    return pl.pallas_call(
        functools.partial(_attn_kernel, window=False),
        grid=(b,),
        in_specs=[pl.BlockSpec(memory_space=pltpu.SMEM), spec(ATTN_WIDTH), spec(KV_WIDTH), spec(KV_WIDTH)],
        out_specs=spec(ATTN_WIDTH),
        out_shape=jax.ShapeDtypeStruct((b, n_ctx, ATTN_WIDTH), BF16),
        scratch_shapes=_attn_scratch(n_ctx, n_ctx),
        compiler_params=_cparams("parallel"),
        name="context_attn",
    )(sink, q, kx, vx)


def _swiglu_residual(h1, m, g2, wg_ref, wu_ref, wd_ref, acc_ref):
    n2 = _norm_mod(h1, g2, m[4:5], m[3:4]).astype(BF16)
    for c in range(wg_ref.shape[1] // FFN_CHUNK):
        cols = slice(c * FFN_CHUNK, (c + 1) * FFN_CHUNK)
        gate = jnp.dot(n2, wg_ref[:, cols], preferred_element_type=F32)
        up = jnp.dot(n2, wu_ref[:, cols], preferred_element_type=F32)
        act = (gate * _sigmoid(gate)) * up
        part = jnp.dot(act.astype(BF16), wd_ref[cols, :], preferred_element_type=F32)
        if c == 0:
            acc_ref[...] = part
        else:
            acc_ref[...] += part
    return h1 + m[5:6] * acc_ref[...]


def _post0_kernel(h_ref, mod_ref, nmod_ref, g2_ref, gn_ref, f_ref, a_ref, wo_ref, wg_ref, wu_ref, wd_ref,
                  o_ref, u_ref, acc_ref):
    m = mod_ref[0]
    mix = jnp.dot(f_ref[0], wo_ref[:FOURIER_WIDTH, :], preferred_element_type=F32)
    mix += jnp.dot(a_ref[0], wo_ref[FOURIER_WIDTH:, :], preferred_element_type=F32)
    h1 = h_ref[0] + m[2:3] * mix
    h2 = _swiglu_residual(h1, m, g2_ref[...], wg_ref, wu_ref, wd_ref, acc_ref)
    o_ref[0] = h2
    nm = nmod_ref[0]
    u_ref[0] = _norm_mod(h2, gn_ref[...], nm[1:2], nm[0:1]).astype(BF16)


def _gelu_tanh(x):
    sqrt_2_over_pi = np.sqrt(2 / np.pi).astype(np.float32)
    cdf = 0.5 * (1.0 + jnp.tanh(sqrt_2_over_pi * (x + 0.044715 * (x ** 3))))
    return x * cdf


def _post1_kernel(h_ref, mod_ref, g1_ref, g2_ref, gf_ref, dskip_ref, yf_ref, yb_ref, wglu_ref,
                  wg_ref, wu_ref, wd_ref, o_ref, acc_ref):
    m = mod_ref[0]
    h = h_ref[0]
    u = _norm_mod(h, g1_ref[...], m[1:2], m[0:1])
    y = dskip_ref[...] * u + yf_ref[0] + yb_ref[0]
    z = jnp.dot(_gelu_tanh(y).astype(BF16), wglu_ref[...], preferred_element_type=F32)
    h1 = h + m[2:3] * (z[:, :D_MODEL] * _sigmoid(z[:, D_MODEL:]))
    h2 = _swiglu_residual(h1, m, g2_ref[...], wg_ref, wu_ref, wd_ref, acc_ref)
    o_ref[0] = (h2 * lax.rsqrt(jnp.mean(h2 * h2, axis=-1, keepdims=True) + RMS_EPS)) * gf_ref[...]


def _resident(shape):
    zeros = (0,) * len(shape)
    return pl.BlockSpec(shape, lambda i, t: zeros, pipeline_mode=pl.Buffered(1))


def _post_call(kernel, name, h, mods, row_inputs, vec_inputs, weights, rows, out_dtypes):
    b, length, d = h.shape
    row_map = lambda i, t: (i, t, 0)
    mod_spec = lambda m: pl.BlockSpec((1, N_MOD, d), lambda i, t: (i if m.shape[0] > 1 else 0, 0, 0))
    in_specs = [pl.BlockSpec((1, rows, d), row_map)]
    in_specs += [mod_spec(m) for m in mods]
    in_specs += [pl.BlockSpec((1, d), lambda i, t: (0, 0)) for _ in vec_inputs]
    in_specs += [pl.BlockSpec((1, rows, a.shape[-1]), row_map) for a in row_inputs]
    in_specs += [_resident(w.shape) for w in weights]
    return pl.pallas_call(
        kernel,
        grid=(b, length // rows),
        in_specs=in_specs,
        out_specs=[pl.BlockSpec((1, rows, d), row_map) for _ in out_dtypes],
        out_shape=[jax.ShapeDtypeStruct((b, length, d), dt) for dt in out_dtypes],
        scratch_shapes=[pltpu.VMEM((rows, d), F32)],
        compiler_params=_cparams("parallel", "parallel"),
        name=name,
    )(h, *mods, *vec_inputs, *row_inputs, *weights)


def _disc_kernel(are_ref, aim_ref, ldt_ref, bre_ref, bim_ref, oar_ref, oai_ref, obr_ref, obi_ref):
    a_re, a_im = are_ref[0], aim_ref[0]
    dt = jnp.exp(ldt_ref[0])
    mag = jnp.exp(a_re * dt)
    abar_re, abar_im = mag * jnp.cos(a_im * dt), mag * jnp.sin(a_im * dt)
    nr, ni = abar_re - 1.0, abar_im
    den = a_re * a_re + a_im * a_im
    f_re = (nr * a_re + ni * a_im) / den
    f_im = (ni * a_re - nr * a_im) / den
    oar_ref[0] = abar_re
    oai_ref[0] = abar_im
    obr_ref[0] = f_re[None] * bre_ref[0] - f_im[None] * bim_ref[0]
    obi_ref[0] = f_re[None] * bim_ref[0] + f_im[None] * bre_ref[0]


def _discretize(a_re, a_im, log_dt, b_re, b_im):
    nd = a_re.shape[0]
    rows = SSM_WIDTH // LANES
    flat = lambda a: a.reshape(nd, rows, LANES)
    ldt = jnp.broadcast_to(log_dt[:, :, None], (nd, SSM_GROUPS, SSM_STATE))
    chan = lambda a: a.reshape(nd, SSM_WIDTH, SSM_GROUP_W).transpose(0, 2, 1).reshape(nd, SSM_GROUP_W, rows, LANES)
    aspec = pl.BlockSpec((1, rows, LANES), lambda i: (i, 0, 0))
    bspec = pl.BlockSpec((1, SSM_GROUP_W, rows, LANES), lambda i: (i, 0, 0, 0))
    ashape = jax.ShapeDtypeStruct((nd, rows, LANES), F32)
    bshape = jax.ShapeDtypeStruct((nd, SSM_GROUP_W, rows, LANES), F32)
    return pl.pallas_call(
        _disc_kernel,
        grid=(nd,),
        in_specs=[aspec, aspec, aspec, bspec, bspec],
        out_specs=[aspec, aspec, bspec, bspec],
        out_shape=[ashape, ashape, bshape, bshape],
        compiler_params=_cparams("parallel"),
        name="s5_discretize",
    )(flat(a_re), flat(a_im), flat(ldt), chan(b_re), chan(b_im))


def _s5_kernel(uc_ref, ul_ref, wd_ref, wr_ref, are_ref, aim_ref, y_ref, xre, xim, hre, him, *, reverse):
    step_idx = pl.program_id(1)
    steps = uc_ref.shape[1]

    @pl.when(step_idx == 0)
    def _():
        hre[...] = jnp.zeros_like(hre)
        him[...] = jnp.zeros_like(him)

    def plane_of(col):
        return col // SUBLANES, col % SUBLANES

    def run(u, readout):
        cols_per_slab = S5_SLAB_STATES // LANES
        for kk in range(S5_SLABS):
            bu = jnp.dot(u[:, kk * LANES:(kk + 1) * LANES], wd_ref[kk], preferred_element_type=F32)
            for c in range(cols_per_slab):
                pln, sub = plane_of(kk * cols_per_slab + c)
                xre[pln, pl.ds(sub, steps, stride=SUBLANES), :] = bu[:, c * LANES:(c + 1) * LANES]
                xim[pln, pl.ds(sub, steps, stride=SUBLANES), :] = (
                    bu[:, S5_SLAB_STATES + c * LANES:S5_SLAB_STATES + (c + 1) * LANES])

        a_r = [are_ref[p] for p in range(S5_PLANES)]
        a_i = [aim_ref[p] for p in range(S5_PLANES)]

        def step(i, carry):
            h_r, h_i = carry
            t = (steps - 1 - i) if reverse else i
            r0 = pl.multiple_of(t * SUBLANES, SUBLANES)
            new_r, new_i = [], []
            for p in range(S5_PLANES):
                n_r = a_r[p] * h_r[p] - a_i[p] * h_i[p] + xre[p, pl.ds(r0, SUBLANES), :]
                n_i = a_r[p] * h_i[p] + a_i[p] * h_r[p] + xim[p, pl.ds(r0, SUBLANES), :]
                xre[p, pl.ds(r0, SUBLANES), :] = n_r
                xim[p, pl.ds(r0, SUBLANES), :] = n_i
                new_r.append(n_r)
                new_i.append(n_i)
            return tuple(new_r), tuple(new_i)

        init = (tuple(hre[p] for p in range(S5_PLANES)), tuple(him[p] for p in range(S5_PLANES)))
        h_r, h_i = lax.fori_loop(0, steps, step, init, unroll=8)
        for p in range(S5_PLANES):
            hre[p] = h_r[p]
            him[p] = h_i[p]

        if readout:
            for kk in range(S5_SLABS):
                parts = []
                for src in (xre, xim):
                    for c in range(cols_per_slab):
                        pln, sub = plane_of(kk * cols_per_slab + c)
                        parts.append(src[pln, pl.ds(sub, steps, stride=SUBLANES), :].astype(BF16))
                lhs = jnp.concatenate(parts, axis=1)
                y_ref[0, :, kk * LANES:(kk + 1) * LANES] = jnp.dot(
                    lhs, wr_ref[kk], preferred_element_type=F32)

    @pl.when(step_idx == 0)
    def _():
        run(uc_ref[0], readout=False)

    @pl.when(step_idx > 0)
    def _():
        run(ul_ref[0], readout=True)


def _s5_direction(u_ctx, u_lat, w_drive, w_read, a_re, a_im, *, reverse):
    b, length, d = u_lat.shape
    n_ctx = u_ctx.shape[1]
    assert n_ctx == S5_CHUNK and length % S5_CHUNK == 0
    n_chunks = length // S5_CHUNK
    if reverse:
        lat_map = lambda i, n: (i, jnp.minimum(n_chunks - n, n_chunks - 1), 0)
    else:
        lat_map = lambda i, n: (i, jnp.maximum(n - 1, 0), 0)
    whole = lambda shape: pl.BlockSpec(shape, lambda i, n: (0,) * len(shape))
    return pl.pallas_call(
        functools.partial(_s5_kernel, reverse=reverse),
        grid=(b, n_chunks + 1),
        in_specs=[
            pl.BlockSpec((1, n_ctx, d), lambda i, n: (i, 0, 0)),
            pl.BlockSpec((1, S5_CHUNK, d), lat_map),
            whole(w_drive.shape), whole(w_read.shape), whole(a_re.shape), whole(a_im.shape),
        ],
        out_specs=pl.BlockSpec((1, S5_CHUNK, d), lat_map),
        out_shape=jax.ShapeDtypeStruct((b, length, d), F32),
        scratch_shapes=[
            pltpu.VMEM((S5_PLANES, S5_CHUNK * SUBLANES, LANES), F32),
            pltpu.VMEM((S5_PLANES, S5_CHUNK * SUBLANES, LANES), F32),
            pltpu.VMEM((S5_PLANES, SUBLANES, LANES), F32),
            pltpu.VMEM((S5_PLANES, SUBLANES, LANES), F32),
        ],
        compiler_params=_cparams("parallel", "arbitrary"),
        name="s5_scan_bwd" if reverse else "s5_scan_fwd",
    )(u_ctx, u_lat, w_drive, w_read, a_re, a_im)


def _rope_tables(length):
    rows = length // GRID_W
    t_row = jnp.repeat(jnp.arange(rows, dtype=F32), GRID_W)
    t_col = jnp.tile(jnp.arange(GRID_W, dtype=F32), rows)
    n_freq = HEAD_DIM // 4
    inv_freq = ROPE_BASE ** (-jnp.arange(n_freq, dtype=F32) / n_freq)
    ang = jnp.concatenate([t_row[:, None] * inv_freq, t_col[:, None] * inv_freq], axis=-1)
    cos, sin = jnp.cos(ang), jnp.sin(ang)
    reps = LANES // HEAD_DIM
    return (jnp.tile(jnp.concatenate([cos, cos], axis=-1), (1, reps)),
            jnp.tile(jnp.concatenate([-sin, sin], axis=-1), (1, reps)))


def _half_split_perm():
    perm = np.arange(MIX_IN_WIDTH)
    head = np.concatenate([np.arange(0, HEAD_DIM, 2), np.arange(1, HEAD_DIM, 2)])
    for start in range(FOURIER_WIDTH, KV_START + KV_WIDTH, HEAD_DIM):
        perm[start:start + HEAD_DIM] = start + head
    return perm


def _channel_dft():
    c = np.arange(FOURIER_GROUP_W)
    ang = 2.0 * np.pi * np.outer(c, c) / FOURIER_GROUP_W
    eye = np.eye(FOURIER_GROUPS)
    scale = FOURIER_GROUP_W ** -0.5
    table = np.concatenate([np.kron(eye, np.cos(ang) * scale), np.kron(eye, np.sin(ang) * scale)], axis=1)
    return jnp.asarray(table, F32).astype(BF16)


def _time_dft_matrices(t):
    r1 = 1 << ((t.bit_length() - 1 + 1) // 2)
    r2 = t // r1
    assert r1 * r2 == t and r2 % SUBLANES == 0
    n = jnp.arange(t, dtype=jnp.int32)[None, :]
    unit = 2.0 * math.pi / t
    ang_a = (((jnp.arange(r1, dtype=jnp.int32) * r2)[:, None] * n) % t).astype(F32) * unit
    ang_b = ((jnp.arange(r2, dtype=jnp.int32)[:, None] * n) % t).astype(F32) * unit
    scale = t ** -0.5
    ca, sa = jnp.cos(ang_a)[:, None, :], jnp.sin(ang_a)[:, None, :]
    cb, sb = (jnp.cos(ang_b) * scale)[None, :, :], (jnp.sin(ang_b) * scale)[None, :, :]
    cmat = (ca * cb - sa * sb).reshape(t, t).astype(BF16)
    smat = (-(sa * cb + ca * sb)).reshape(t, t).astype(BF16)
    return cmat, smat


def _ffn_weights(w_gate, w_up, w_down):
    return w_gate.astype(BF16), w_up.astype(BF16), w_down.astype(BF16)


def _s5_weights(bbar_re, bbar_im, c_re, c_im):
    nd = bbar_re.shape[0]
    eye = jnp.eye(S5_SLAB_GROUPS, dtype=F32)

    def drive(bb):
        bb = bb.reshape(nd, SSM_GROUP_W, S5_SLABS, S5_SLAB_GROUPS, SSM_STATE)
        w = jnp.einsum("dhkgp,gG->dkghGp", bb, eye)
        return w.reshape(nd, S5_SLABS, LANES, S5_SLAB_STATES)

    def read(cc):
        cc = cc.astype(F32).reshape(nd, S5_SLABS, S5_SLAB_GROUPS, SSM_GROUP_W, SSM_STATE)
        w = jnp.einsum("dkghp,gG->dkgpGh", cc, eye)
        return w.reshape(nd, S5_SLABS, S5_SLAB_STATES, LANES)

    w_drive = jnp.concatenate([drive(bbar_re), drive(bbar_im)], axis=-1).astype(BF16)
    w_read = jnp.concatenate([read(c_re), -read(c_im)], axis=2).astype(BF16)
    return w_drive, w_read


def kernel(x, c, ctx, c_ctx, mod_w, mod_b, norm_g, ffn_w_gate, ffn_w_up, ffn_w_down, mix_w_in, mix_w_out,
           attn_sink, ssm_a_re, ssm_a_im, ssm_log_dt, ssm_b_re, ssm_b_im, ssm_c_re, ssm_c_im, ssm_d,
           ssm_glu_w, final_g):
    bn, length, d = x.shape
    n_ctx = ctx.shape[1]
    depth = mod_w.shape[0]
    assert depth == 2 and d == D_MODEL

    cond = jnp.zeros((SUBLANES, d), F32).at[:bn].set(c).at[bn].set(c_ctx)
    mods = _modulations(cond, mod_w, mod_b).reshape(depth, SUBLANES, N_MOD, d)
    mods_lat, mods_ctx = mods[:, :bn], mods[:, bn:bn + 1]
    vec = lambda a: a.reshape(1, d)

    w_in = mix_w_in[0][:, _half_split_perm()].astype(BF16)
    w_out = mix_w_out[0].astype(BF16)
    bcs = _channel_dft()
    cos_t, sin_t = _rope_tables(length)
    ffn0 = _ffn_weights(ffn_w_gate[0], ffn_w_up[0], ffn_w_down[0])
    g_mix, g_ffn = vec(norm_g[0, 0]), vec(norm_g[0, 1])

    uv, q, k, v = _pre0(x, mods_lat[0], g_mix, w_in, bcs, cos_t, sin_t, rope=True, rows=TOKEN_TILE)
    uvc, qc, kc, vc = _pre0(ctx, mods_ctx[0], g_mix, w_in, bcs, cos_t, sin_t, rope=False, rows=n_ctx)
    four = _time_dft(*_time_dft_matrices(length), uv)
    four_c = _time_dft(*_time_dft_matrices(n_ctx), uvc)
    att = _window_attention(attn_sink[0], q, k, v, kc, vc)
    att_c = _context_attention(attn_sink[0], qc, kc, vc)
    g_mix1 = vec(norm_g[1, 0])
    h, u_lat = _post_call(_post0_kernel, "mix_out_ffn", x, [mods_lat[0], mods_lat[1]], [four, att],
                          [g_ffn, g_mix1], [w_out, *ffn0], TOKEN_TILE, [F32, BF16])
    _, u_ctx = _post_call(_post0_kernel, "mix_out_ffn_ctx", ctx, [mods_ctx[0], mods_ctx[1]], [four_c, att_c],
                          [g_ffn, g_mix1], [w_out, *ffn0], n_ctx, [F32, BF16])

    g_mix, g_ffn = g_mix1, vec(norm_g[1, 1])
    abar_re, abar_im, bbar_re, bbar_im = _discretize(
        ssm_a_re[0], ssm_a_im[0], ssm_log_dt[0], ssm_b_re[0], ssm_b_im[0])
    w_drive, w_read = _s5_weights(bbar_re, bbar_im, ssm_c_re[0], ssm_c_im[0])
    planes = lambda a: a.reshape(2, S5_PLANES, SUBLANES, LANES)
    abar_re, abar_im = planes(abar_re), planes(abar_im)
    y_dir = [_s5_direction(u_ctx, u_lat, w_drive[i], w_read[i], abar_re[i], abar_im[i], reverse=bool(i))
             for i in range(2)]
    ffn1 = _ffn_weights(ffn_w_gate[1], ffn_w_up[1], ffn_w_down[1])
    return _post_call(_post1_kernel, "s5_glu_ffn_norm", h, [mods_lat[1]], y_dir,
                      [g_mix, g_ffn, vec(final_g), vec(ssm_d[0])],
                      [ssm_glu_w[0].astype(BF16), *ffn1], TOKEN_TILE, [F32])[0]
```

```python
import functools
import math

import numpy as np
import jax
import jax.numpy as jnp
from jax import lax
from jax.experimental import pallas as pl
from jax.experimental.pallas import tpu as pltpu

F32 = jnp.float32
BF16 = jnp.bfloat16

D_MODEL = 1024
GRID_W = 64
N_MOD = 6
N_Q_HEADS = 8
N_KV_HEADS = 2
HEAD_DIM = 64
ATTN_WIDTH = N_Q_HEADS * HEAD_DIM
KV_WIDTH = N_KV_HEADS * HEAD_DIM
FOURIER_GROUPS = 8
FOURIER_GROUP_W = 64
FOURIER_WIDTH = FOURIER_GROUPS * FOURIER_GROUP_W
KV_START = FOURIER_WIDTH + ATTN_WIDTH
MIX_IN_WIDTH = KV_START + 2 * KV_WIDTH
WINDOW = 128
BLOCK = 128
ATTN_SCALE = HEAD_DIM ** -0.5
ROPE_BASE = 10000.0
NEG_INF = -1e30
SSM_GROUP_W = 16
SSM_GROUPS = D_MODEL // SSM_GROUP_W
SSM_STATE = 64
SSM_WIDTH = SSM_GROUPS * SSM_STATE
RMS_EPS = 1e-6

LANES = 128
SUBLANES = 8
VMEM_LIMIT_BYTES = 56 * 1024 * 1024

TOKEN_TILE = 512
FFN_CHUNK = 256
MOD_TILE = 1536
DFT_ROW_TILE = 1024
DFT_K_TILE = 2048
S5_CHUNK = 256
S5_SLAB_GROUPS = LANES // SSM_GROUP_W
S5_SLABS = SSM_GROUPS // S5_SLAB_GROUPS
S5_SLAB_STATES = S5_SLAB_GROUPS * SSM_STATE
S5_COLS = SSM_WIDTH // LANES
S5_PLANES = S5_COLS // SUBLANES


def _cparams(*sem):
    return pltpu.CompilerParams(dimension_semantics=sem, vmem_limit_bytes=VMEM_LIMIT_BYTES)


def _sigmoid(x):
    return 1.0 / (1.0 + jnp.exp(-x))


def _norm_mod(x, g, scale, shift):
    y = x * lax.rsqrt(jnp.mean(x * x, axis=-1, keepdims=True) + RMS_EPS)
    return (y * g) * (1.0 + scale) + shift


def _split_bf16(x):
    hi = x.astype(BF16)
    lo = (x - hi.astype(F32)).astype(BF16)
    return hi, lo


def _mod_kernel(cond_ref, w_ref, b_ref, o_ref):
    x = cond_ref[...]
    s_hi, s_lo = _split_bf16(x * _sigmoid(x))
    w_hi, w_lo = _split_bf16(w_ref[0])
    acc = jnp.dot(s_hi, w_hi, preferred_element_type=F32)
    acc += jnp.dot(s_lo, w_hi, preferred_element_type=F32)
    acc += jnp.dot(s_hi, w_lo, preferred_element_type=F32)
    o_ref[0] = acc + b_ref[0]


def _modulations(cond, mod_w, mod_b):
    depth, d, n = mod_w.shape
    rows = cond.shape[0]
    return pl.pallas_call(
        _mod_kernel,
        grid=(depth, n // MOD_TILE),
        in_specs=[
            pl.BlockSpec((rows, d), lambda l, j: (0, 0)),
            pl.BlockSpec((1, d, MOD_TILE), lambda l, j: (l, 0, j)),
            pl.BlockSpec((1, 1, MOD_TILE), lambda l, j: (l, 0, j)),
        ],
        out_specs=pl.BlockSpec((1, rows, MOD_TILE), lambda l, j: (l, 0, j)),
        out_shape=jax.ShapeDtypeStruct((depth, rows, n), F32),
        compiler_params=_cparams("parallel", "parallel"),
        name="adaln_mod",
    )(cond, mod_w, mod_b.reshape(depth, 1, n))


def _pre0_kernel(x_ref, mod_ref, g_ref, w_ref, bcs_ref, cos_ref, sin_ref,
                 uv_ref, q_ref, k_ref, v_ref, *, rope):
    m = mod_ref[0]
    n = _norm_mod(x_ref[0], g_ref[...], m[1:2], m[0:1])
    p = jnp.dot(n.astype(BF16), w_ref[...], preferred_element_type=F32)
    uv = jnp.dot(p[:, :FOURIER_WIDTH].astype(BF16), bcs_ref[...], preferred_element_type=F32)
    uv_ref[0, 0] = uv[:, :FOURIER_WIDTH].astype(BF16)
    uv_ref[0, 1] = uv[:, FOURIER_WIDTH:].astype(BF16)

    def rot(x):
        if not rope:
            return x
        lane = lax.broadcasted_iota(jnp.int32, x.shape, 1)
        first = (lane & (HEAD_DIM // 2)) == 0
        partner = jnp.where(first, pltpu.roll(x, LANES - HEAD_DIM // 2, 1),
                            pltpu.roll(x, HEAD_DIM // 2, 1))
        return x * cos_ref[...] + partner * sin_ref[...]

    for j in range(ATTN_WIDTH // LANES):
        lo = FOURIER_WIDTH + j * LANES
        q_ref[0, :, j * LANES:(j + 1) * LANES] = (rot(p[:, lo:lo + LANES]) * ATTN_SCALE).astype(BF16)
    k_ref[0] = rot(p[:, KV_START:KV_START + KV_WIDTH]).astype(BF16)
    v_ref[0] = p[:, KV_START + KV_WIDTH:].astype(BF16)


def _pre0(x, mods, g, w_in, bcs, cos_t, sin_t, *, rope, rows):
    b, length, d = x.shape
    grid = (b, length // rows)
    row_map = lambda i, t: (i, t, 0)
    const2 = lambda i, t: (0, 0)
    return pl.pallas_call(
        functools.partial(_pre0_kernel, rope=rope),
        grid=grid,
        in_specs=[
            pl.BlockSpec((1, rows, d), row_map),
            pl.BlockSpec((1, N_MOD, d), lambda i, t: (i if mods.shape[0] > 1 else 0, 0, 0)),
            pl.BlockSpec((1, d), const2),
            pl.BlockSpec((d, MIX_IN_WIDTH), const2),
            pl.BlockSpec((FOURIER_WIDTH, 2 * FOURIER_WIDTH), const2),
            pl.BlockSpec((rows, LANES), lambda i, t: (t, 0)),
            pl.BlockSpec((rows, LANES), lambda i, t: (t, 0)),
        ],
        out_specs=[
            pl.BlockSpec((1, 2, rows, FOURIER_WIDTH), lambda i, t: (i, 0, t, 0)),
            pl.BlockSpec((1, rows, ATTN_WIDTH), row_map),
            pl.BlockSpec((1, rows, KV_WIDTH), row_map),
            pl.BlockSpec((1, rows, KV_WIDTH), row_map),
        ],
        out_shape=[
            jax.ShapeDtypeStruct((b, 2, length, FOURIER_WIDTH), BF16),
            jax.ShapeDtypeStruct((b, length, ATTN_WIDTH), BF16),
            jax.ShapeDtypeStruct((b, length, KV_WIDTH), BF16),
            jax.ShapeDtypeStruct((b, length, KV_WIDTH), BF16),
        ],
        compiler_params=_cparams("parallel", "parallel"),
        name="mix_in_rope" if rope else "mix_in_ctx",
    )(x, mods, g, w_in, bcs, cos_t, sin_t)


def _dft_kernel(c_ref, s_ref, u_ref, v_ref, o_ref, acc_ref):
    k = pl.program_id(2)

    @pl.when(k == 0)
    def _():
        acc_ref[...] = jnp.zeros_like(acc_ref)

    acc_ref[...] += (jnp.dot(c_ref[...], u_ref[0, 0], preferred_element_type=F32)
                     + jnp.dot(s_ref[...], v_ref[0, 0], preferred_element_type=F32))

    @pl.when(k == pl.num_programs(2) - 1)
    def _():
        o_ref[0] = acc_ref[...].astype(BF16)


def _time_dft(cmat, smat, uv):
    b, _, t, w = uv.shape
    tm = min(DFT_ROW_TILE, t)
    tk = min(DFT_K_TILE, t)
    mat_spec = pl.BlockSpec((tm, tk), lambda i, r, k: (r, k))
    return pl.pallas_call(
        _dft_kernel,
        grid=(b, t // tm, t // tk),
        in_specs=[
            mat_spec, mat_spec,
            pl.BlockSpec((1, 1, tk, w), lambda i, r, k: (i, 0, k, 0)),
            pl.BlockSpec((1, 1, tk, w), lambda i, r, k: (i, 1, k, 0)),
        ],
        out_specs=pl.BlockSpec((1, tm, w), lambda i, r, k: (i, r, 0)),
        out_shape=jax.ShapeDtypeStruct((b, t, w), BF16),
        scratch_shapes=[pltpu.VMEM((tm, w), F32)],
        compiler_params=_cparams("parallel", "parallel", "arbitrary"),
        name="time_dft",
    )(cmat, smat, uv, uv)


def _attn_kernel(sink_ref, q_ref, *refs, window):
    if window:
        kp_ref, kc_ref, kn_ref, vp_ref, vc_ref, vn_ref, kx_ref, vx_ref, o_ref, s_scr, p_scr = refs
        kcat = jnp.concatenate([kx_ref[0], kp_ref[0], kc_ref[0], kn_ref[0]], axis=0)
        vcat = jnp.concatenate([vx_ref[0], vp_ref[0], vc_ref[0], vn_ref[0]], axis=0)
    else:
        kx_ref, vx_ref, o_ref, s_scr, p_scr = refs
        kcat, vcat = kx_ref[0], vx_ref[0]
    n_keys = kcat.shape[0]
    n_q = q_ref.shape[1]

    lane = lax.broadcasted_iota(jnp.int32, kcat.shape, 1)
    low = lane < HEAD_DIM
    zero = jnp.zeros_like(kcat)
    kswap = pltpu.roll(kcat.astype(F32), HEAD_DIM, 1).astype(BF16)
    vswap = pltpu.roll(vcat.astype(F32), HEAD_DIM, 1).astype(BF16)
    k_lo = (jnp.where(low, kcat, zero), jnp.where(low, kswap, zero))
    k_hi = (jnp.where(low, zero, kswap), jnp.where(low, zero, kcat))
    v_lo = (jnp.where(low, vcat, zero), jnp.where(low, vswap, zero))
    v_hi = (jnp.where(low, zero, vswap), jnp.where(low, zero, vcat))

    if window:
        blk = pl.program_id(1)
        last = pl.num_programs(1) - 1
        n_ctx = kx_ref.shape[1]
        row = lax.broadcasted_iota(jnp.int32, (n_q, BLOCK), 0)
        col = lax.broadcasted_iota(jnp.int32, (n_q, BLOCK), 1)
        off_prev = jnp.where(blk > 0, 0, BLOCK)
        off_next = jnp.where(blk < last, 0, BLOCK)
        ok_prev = col >= row + off_prev
        ok_next = col <= row - off_next

    q_per_kv = N_Q_HEADS // N_KV_HEADS
    n_pairs = ATTN_WIDTH // LANES
    for j in range(n_pairs):
        g = (2 * j) // q_per_kv
        qp = q_ref[0, :, j * LANES:(j + 1) * LANES]
        for half, k_ext in enumerate((k_lo[g], k_hi[g])):
            s = lax.dot_general(qp, k_ext, (((1,), (1,)), ((), ())), preferred_element_type=F32)
            head = 2 * j + half
            if window:
                s_scr[head, :, :n_ctx] = s[:, :n_ctx]
                s_scr[head, :, n_ctx:n_ctx + BLOCK] = jnp.where(ok_prev, s[:, n_ctx:n_ctx + BLOCK], NEG_INF)
                s_scr[head, :, n_ctx + BLOCK:n_ctx + 2 * BLOCK] = s[:, n_ctx + BLOCK:n_ctx + 2 * BLOCK]
                s_scr[head, :, n_ctx + 2 * BLOCK:] = jnp.where(ok_next, s[:, n_ctx + 2 * BLOCK:], NEG_INF)
            else:
                s_scr[head] = s
    inv = []
    for head in range(N_Q_HEADS):
        s = s_scr[head]
        sink = sink_ref[head]
        mx = jnp.maximum(jnp.max(s, axis=1, keepdims=True), sink)
        p = jnp.exp(s - mx)
        inv.append(1.0 / (jnp.sum(p, axis=1, keepdims=True) + jnp.exp(sink - mx)))
        p_scr[head] = p.astype(BF16)
    for j in range(n_pairs):
        g = (2 * j) // q_per_kv
        out = jnp.dot(p_scr[2 * j], v_lo[g], preferred_element_type=F32) * inv[2 * j]
        out += jnp.dot(p_scr[2 * j + 1], v_hi[g], preferred_element_type=F32) * inv[2 * j + 1]
        o_ref[0, :, j * LANES:(j + 1) * LANES] = out.astype(BF16)


def _attn_scratch(n_q, n_keys):
    return [pltpu.VMEM((N_Q_HEADS, n_q, n_keys), F32), pltpu.VMEM((N_Q_HEADS, n_q, n_keys), BF16)]


def _window_attention(sink, q, k, v, kx, vx):
    b, length, _ = q.shape
    nb = length // BLOCK
    n_ctx = kx.shape[1]
    kv_spec = lambda f: pl.BlockSpec((1, BLOCK, KV_WIDTH), f)
    prev = lambda i, n: (i, jnp.maximum(n - 1, 0), 0)
    cur = lambda i, n: (i, n, 0)
    nxt = lambda i, n: (i, jnp.minimum(n + 1, nb - 1), 0)
    ctx_spec = pl.BlockSpec((1, n_ctx, KV_WIDTH), lambda i, n: (i, 0, 0))
    return pl.pallas_call(
        functools.partial(_attn_kernel, window=True),
        grid=(b, nb),
        in_specs=[
            pl.BlockSpec(memory_space=pltpu.SMEM),
            pl.BlockSpec((1, BLOCK, ATTN_WIDTH), cur),
            kv_spec(prev), kv_spec(cur), kv_spec(nxt),
            kv_spec(prev), kv_spec(cur), kv_spec(nxt),
            ctx_spec, ctx_spec,
        ],
        out_specs=pl.BlockSpec((1, BLOCK, ATTN_WIDTH), cur),
        out_shape=jax.ShapeDtypeStruct((b, length, ATTN_WIDTH), BF16),
        scratch_shapes=_attn_scratch(BLOCK, n_ctx + 3 * BLOCK),
        compiler_params=_cparams("parallel", "parallel"),
        name="window_attn",
    )(sink, q, k, k, k, v, v, v, kx, vx)


def _context_attention(sink, q, kx, vx):
    b, n_ctx, _ = q.shape
    spec = lambda w: pl.BlockSpec((1, n_ctx, w), lambda i: (i, 0, 0))
    return pl.pallas_call(
        functools.partial(_attn_kernel, window=False),
        grid=(b,),
        in_specs=[pl.BlockSpec(memory_space=pltpu.SMEM), spec(ATTN_WIDTH), spec(KV_WIDTH), spec(KV_WIDTH)],
        out_specs=spec(ATTN_WIDTH),
        out_shape=jax.ShapeDtypeStruct((b, n_ctx, ATTN_WIDTH), BF16),
        scratch_shapes=_attn_scratch(n_ctx, n_ctx),
        compiler_params=_cparams("parallel"),
        name="context_attn",
    )(sink, q, kx, vx)


def _swiglu_residual(h1, m, g2, wg_ref, wu_ref, wd_ref, acc_ref):
    n2 = _norm_mod(h1, g2, m[4:5], m[3:4]).astype(BF16)
    for c in range(wg_ref.shape[1] // FFN_CHUNK):
        cols = slice(c * FFN_CHUNK, (c + 1) * FFN_CHUNK)
        gate = jnp.dot(n2, wg_ref[:, cols], preferred_element_type=F32)
        up = jnp.dot(n2, wu_ref[:, cols], preferred_element_type=F32)
        act = (gate * _sigmoid(gate)) * up
        part = jnp.dot(act.astype(BF16), wd_ref[cols, :], preferred_element_type=F32)
        if c == 0:
            acc_ref[...] = part
        else:
            acc_ref[...] += part
    return h1 + m[5:6] * acc_ref[...]


def _post0_kernel(h_ref, mod_ref, nmod_ref, g2_ref, gn_ref, f_ref, a_ref, wo_ref, wg_ref, wu_ref, wd_ref,
                  o_ref, u_ref, acc_ref):
    m = mod_ref[0]
    mix = jnp.dot(f_ref[0], wo_ref[:FOURIER_WIDTH, :], preferred_element_type=F32)
    mix += jnp.dot(a_ref[0], wo_ref[FOURIER_WIDTH:, :], preferred_element_type=F32)
    h1 = h_ref[0] + m[2:3] * mix
    h2 = _swiglu_residual(h1, m, g2_ref[...], wg_ref, wu_ref, wd_ref, acc_ref)
    o_ref[0] = h2
    nm = nmod_ref[0]
    u_ref[0] = _norm_mod(h2, gn_ref[...], nm[1:2], nm[0:1]).astype(BF16)


def _gelu_tanh(x):
    sqrt_2_over_pi = np.sqrt(2 / np.pi).astype(np.float32)
    cdf = 0.5 * (1.0 + jnp.tanh(sqrt_2_over_pi * (x + 0.044715 * (x ** 3))))
    return x * cdf


def _post1_kernel(h_ref, mod_ref, g1_ref, g2_ref, gf_ref, dskip_ref, yf_ref, yb_ref, wglu_ref,
                  wg_ref, wu_ref, wd_ref, o_ref, acc_ref):
    m = mod_ref[0]
    h = h_ref[0]
    u = _norm_mod(h, g1_ref[...], m[1:2], m[0:1])
    y = dskip_ref[...] * u + yf_ref[0] + yb_ref[0]
    z = jnp.dot(_gelu_tanh(y).astype(BF16), wglu_ref[...], preferred_element_type=F32)
    h1 = h + m[2:3] * (z[:, :D_MODEL] * _sigmoid(z[:, D_MODEL:]))
    h2 = _swiglu_residual(h1, m, g2_ref[...], wg_ref, wu_ref, wd_ref, acc_ref)
    o_ref[0] = (h2 * lax.rsqrt(jnp.mean(h2 * h2, axis=-1, keepdims=True) + RMS_EPS)) * gf_ref[...]


def _resident(shape):
    zeros = (0,) * len(shape)
    return pl.BlockSpec(shape, lambda i, t: zeros, pipeline_mode=pl.Buffered(1))


def _post_call(kernel, name, h, mods, row_inputs, vec_inputs, weights, rows, out_dtypes):
    b, length, d = h.shape
    row_map = lambda i, t: (i, t, 0)
    mod_spec = lambda m: pl.BlockSpec((1, N_MOD, d), lambda i, t: (i if m.shape[0] > 1 else 0, 0, 0))
    in_specs = [pl.BlockSpec((1, rows, d), row_map)]
    in_specs += [mod_spec(m) for m in mods]
    in_specs += [pl.BlockSpec((1, d), lambda i, t: (0, 0)) for _ in vec_inputs]
    in_specs += [pl.BlockSpec((1, rows, a.shape[-1]), row_map) for a in row_inputs]
    in_specs += [_resident(w.shape) for w in weights]
    return pl.pallas_call(
        kernel,
        grid=(b, length // rows),
        in_specs=in_specs,
        out_specs=[pl.BlockSpec((1, rows, d), row_map) for _ in out_dtypes],
        out_shape=[jax.ShapeDtypeStruct((b, length, d), dt) for dt in out_dtypes],
        scratch_shapes=[pltpu.VMEM((rows, d), F32)],
        compiler_params=_cparams("parallel", "parallel"),
        name=name,
    )(h, *mods, *vec_inputs, *row_inputs, *weights)


def _disc_kernel(are_ref, aim_ref, ldt_ref, bre_ref, bim_ref, oar_ref, oai_ref, obr_ref, obi_ref):
    a_re, a_im = are_ref[0], aim_ref[0]
    dt = jnp.exp(ldt_ref[0])
    mag = jnp.exp(a_re * dt)
    abar_re, abar_im = mag * jnp.cos(a_im * dt), mag * jnp.sin(a_im * dt)
    nr, ni = abar_re - 1.0, abar_im
    den = a_re * a_re + a_im * a_im
    f_re = (nr * a_re + ni * a_im) / den
    f_im = (ni * a_re - nr * a_im) / den
    oar_ref[0] = abar_re
    oai_ref[0] = abar_im
    obr_ref[0] = f_re[None] * bre_ref[0] - f_im[None] * bim_ref[0]
    obi_ref[0] = f_re[None] * bim_ref[0] + f_im[None] * bre_ref[0]


def _discretize(a_re, a_im, log_dt, b_re, b_im):
    nd = a_re.shape[0]
    rows = SSM_WIDTH // LANES
    flat = lambda a: a.reshape(nd, rows, LANES)
    ldt = jnp.broadcast_to(log_dt[:, :, None], (nd, SSM_GROUPS, SSM_STATE))
    chan = lambda a: a.reshape(nd, SSM_WIDTH, SSM_GROUP_W).transpose(0, 2, 1).reshape(nd, SSM_GROUP_W, rows, LANES)
    aspec = pl.BlockSpec((1, rows, LANES), lambda i: (i, 0, 0))
    bspec = pl.BlockSpec((1, SSM_GROUP_W, rows, LANES), lambda i: (i, 0, 0, 0))
    ashape = jax.ShapeDtypeStruct((nd, rows, LANES), F32)
    bshape = jax.ShapeDtypeStruct((nd, SSM_GROUP_W, rows, LANES), F32)
    return pl.pallas_call(
        _disc_kernel,
        grid=(nd,),
        in_specs=[aspec, aspec, aspec, bspec, bspec],
        out_specs=[aspec, aspec, bspec, bspec],
        out_shape=[ashape, ashape, bshape, bshape],
        compiler_params=_cparams("parallel"),
        name="s5_discretize",
    )(flat(a_re), flat(a_im), flat(ldt), chan(b_re), chan(b_im))


def _s5_kernel(uc_ref, ul_ref, wd_ref, wr_ref, are_ref, aim_ref, y_ref,
               xa_re, xa_im, xb_re, xb_im, hre, him, *, reverse):
    step_idx = pl.program_id(1)
    steps = S5_CHUNK
    cols_per_slab = S5_SLAB_STATES // LANES
    seg = steps // S5_SLABS

    @pl.when(step_idx == 0)
    def _():
        hre[...] = jnp.zeros_like(hre)
        him[...] = jnp.zeros_like(him)

    def plane_of(col):
        return col // SUBLANES, col % SUBLANES

    def drive_slab(u, kk, xre, xim):
        bu = jnp.dot(u(kk * LANES, (kk + 1) * LANES), wd_ref[kk], preferred_element_type=F32)
        for c in range(cols_per_slab):
            pln, sub = plane_of(kk * cols_per_slab + c)
            xre[pln, pl.ds(sub, steps, stride=SUBLANES), :] = bu[:, c * LANES:(c + 1) * LANES]
            xim[pln, pl.ds(sub, steps, stride=SUBLANES), :] = (
                bu[:, S5_SLAB_STATES + c * LANES:S5_SLAB_STATES + (c + 1) * LANES])

    def read_slab(kk, xre, xim):
        parts = []
        for src in (xre, xim):
            for c in range(cols_per_slab):
                pln, sub = plane_of(kk * cols_per_slab + c)
                parts.append(src[pln, pl.ds(sub, steps, stride=SUBLANES), :].astype(BF16))
        return jnp.dot(jnp.concatenate(parts, axis=1), wr_ref[kk], preferred_element_type=F32)

    a_r = [are_ref[p] for p in range(S5_PLANES)]
    a_i = [aim_ref[p] for p in range(S5_PLANES)]

    def one_step(t, h, xre, xim):
        h_r, h_i = h
        r0 = t * SUBLANES if isinstance(t, int) else pl.multiple_of(t * SUBLANES, SUBLANES)
        new_r, new_i = [], []
        for p in range(S5_PLANES):
            n_r = a_r[p] * h_r[p] - a_i[p] * h_i[p] + xre[p, pl.ds(r0, SUBLANES), :]
            n_i = a_r[p] * h_i[p] + a_i[p] * h_r[p] + xim[p, pl.ds(r0, SUBLANES), :]
            xre[p, pl.ds(r0, SUBLANES), :] = n_r
            xim[p, pl.ds(r0, SUBLANES), :] = n_i
            new_r.append(n_r)
            new_i.append(n_i)
        return tuple(new_r), tuple(new_i)

    def time_of(i):
        return (steps - 1 - i) if reverse else i

    def scan_segment(k, h, xre, xim):
        for i in range(k * seg, (k + 1) * seg):
            h = one_step(time_of(i), h, xre, xim)
        return h

    def load_state():
        return (tuple(hre[p] for p in range(S5_PLANES)), tuple(him[p] for p in range(S5_PLANES)))

    def store_state(h):
        for p in range(S5_PLANES):
            hre[p] = h[0][p]
            him[p] = h[1][p]

    @pl.when(step_idx == 0)
    def _():
        u = lambda lo, hi: uc_ref[0, :, lo:hi]
        for kk in range(S5_SLABS):
            drive_slab(u, kk, xa_re, xa_im)
        h = lax.fori_loop(0, steps, lambda i, h: one_step(time_of(i), h, xa_re, xa_im), load_state(), unroll=8)
        store_state(h)

    @pl.when(step_idx > 0)
    def _():
        first, second = slice(0, steps), slice(steps, 2 * steps)
        rows_a, rows_b = (second, first) if reverse else (first, second)
        u_a = lambda lo, hi: ul_ref[0, rows_a, lo:hi]
        u_b = lambda lo, hi: ul_ref[0, rows_b, lo:hi]
        for kk in range(S5_SLABS):
            drive_slab(u_a, kk, xa_re, xa_im)
        h = load_state()
        for kk in range(S5_SLABS):
            drive_slab(u_b, kk, xb_re, xb_im)
            h = scan_segment(kk, h, xa_re, xa_im)
        for kk in range(S5_SLABS):
            h = scan_segment(kk, h, xb_re, xb_im)
            y_ref[0, rows_a, kk * LANES:(kk + 1) * LANES] = read_slab(kk, xa_re, xa_im)
        store_state(h)
        for kk in range(S5_SLABS):
            y_ref[0, rows_b, kk * LANES:(kk + 1) * LANES] = read_slab(kk, xb_re, xb_im)


def _s5_direction(u_ctx, u_lat, w_drive, w_read, a_re, a_im, *, reverse):
    b, length, d = u_lat.shape
    n_ctx = u_ctx.shape[1]
    pair = 2 * S5_CHUNK
    assert n_ctx == S5_CHUNK and length % pair == 0
    n_pairs = length // pair
    if reverse:
        lat_map = lambda i, n: (i, jnp.minimum(n_pairs - n, n_pairs - 1), 0)
    else:
        lat_map = lambda i, n: (i, jnp.maximum(n - 1, 0), 0)
    whole = lambda shape: pl.BlockSpec(shape, lambda i, n: (0,) * len(shape))
    plane_buf = pltpu.VMEM((S5_PLANES, S5_CHUNK * SUBLANES, LANES), F32)
    state_buf = pltpu.VMEM((S5_PLANES, SUBLANES, LANES), F32)
    return pl.pallas_call(
        functools.partial(_s5_kernel, reverse=reverse),
        grid=(b, n_pairs + 1),
        in_specs=[
            pl.BlockSpec((1, n_ctx, d), lambda i, n: (i, 0, 0)),
            pl.BlockSpec((1, pair, d), lat_map),
            whole(w_drive.shape), whole(w_read.shape), whole(a_re.shape), whole(a_im.shape),
        ],
        out_specs=pl.BlockSpec((1, pair, d), lat_map),
        out_shape=jax.ShapeDtypeStruct((b, length, d), F32),
        scratch_shapes=[plane_buf, plane_buf, plane_buf, plane_buf, state_buf, state_buf],
        compiler_params=_cparams("parallel", "arbitrary"),
        name="s5_scan_bwd" if reverse else "s5_scan_fwd",
    )(u_ctx, u_lat, w_drive, w_read, a_re, a_im)


def _rope_tables(length):
    rows = length // GRID_W
    t_row = jnp.repeat(jnp.arange(rows, dtype=F32), GRID_W)
    t_col = jnp.tile(jnp.arange(GRID_W, dtype=F32), rows)
    n_freq = HEAD_DIM // 4
    inv_freq = ROPE_BASE ** (-jnp.arange(n_freq, dtype=F32) / n_freq)
    ang = jnp.concatenate([t_row[:, None] * inv_freq, t_col[:, None] * inv_freq], axis=-1)
    cos, sin = jnp.cos(ang), jnp.sin(ang)
    reps = LANES // HEAD_DIM
    return (jnp.tile(jnp.concatenate([cos, cos], axis=-1), (1, reps)),
            jnp.tile(jnp.concatenate([-sin, sin], axis=-1), (1, reps)))


def _half_split_perm():
    perm = np.arange(MIX_IN_WIDTH)
    head = np.concatenate([np.arange(0, HEAD_DIM, 2), np.arange(1, HEAD_DIM, 2)])
    for start in range(FOURIER_WIDTH, KV_START + KV_WIDTH, HEAD_DIM):
        perm[start:start + HEAD_DIM] = start + head
    return perm


def _channel_dft():
    c = np.arange(FOURIER_GROUP_W)
    ang = 2.0 * np.pi * np.outer(c, c) / FOURIER_GROUP_W
    eye = np.eye(FOURIER_GROUPS)
    scale = FOURIER_GROUP_W ** -0.5
    table = np.concatenate([np.kron(eye, np.cos(ang) * scale), np.kron(eye, np.sin(ang) * scale)], axis=1)
    return jnp.asarray(table, F32).astype(BF16)


def _time_dft_matrices(t):
    r1 = 1 << ((t.bit_length() - 1 + 1) // 2)
    r2 = t // r1
    assert r1 * r2 == t and r2 % SUBLANES == 0
    n = jnp.arange(t, dtype=jnp.int32)[None, :]
    unit = 2.0 * math.pi / t
    ang_a = (((jnp.arange(r1, dtype=jnp.int32) * r2)[:, None] * n) % t).astype(F32) * unit
    ang_b = ((jnp.arange(r2, dtype=jnp.int32)[:, None] * n) % t).astype(F32) * unit
    scale = t ** -0.5
    ca, sa = jnp.cos(ang_a)[:, None, :], jnp.sin(ang_a)[:, None, :]
    cb, sb = (jnp.cos(ang_b) * scale)[None, :, :], (jnp.sin(ang_b) * scale)[None, :, :]
    cmat = (ca * cb - sa * sb).reshape(t, t).astype(BF16)
    smat = (-(sa * cb + ca * sb)).reshape(t, t).astype(BF16)
    return cmat, smat


def _ffn_weights(w_gate, w_up, w_down):
    return w_gate.astype(BF16), w_up.astype(BF16), w_down.astype(BF16)


def _s5_weights(bbar_re, bbar_im, c_re, c_im):
    nd = bbar_re.shape[0]
    eye = jnp.eye(S5_SLAB_GROUPS, dtype=F32)

    def drive(bb):
        bb = bb.reshape(nd, SSM_GROUP_W, S5_SLABS, S5_SLAB_GROUPS, SSM_STATE)
        w = jnp.einsum("dhkgp,gG->dkghGp", bb, eye)
        return w.reshape(nd, S5_SLABS, LANES, S5_SLAB_STATES)

    def read(cc):
        cc = cc.astype(F32).reshape(nd, S5_SLABS, S5_SLAB_GROUPS, SSM_GROUP_W, SSM_STATE)
        w = jnp.einsum("dkghp,gG->dkgpGh", cc, eye)
        return w.reshape(nd, S5_SLABS, S5_SLAB_STATES, LANES)

    w_drive = jnp.concatenate([drive(bbar_re), drive(bbar_im)], axis=-1).astype(BF16)
    w_read = jnp.concatenate([read(c_re), -read(c_im)], axis=2).astype(BF16)
    return w_drive, w_read


def kernel(x, c, ctx, c_ctx, mod_w, mod_b, norm_g, ffn_w_gate, ffn_w_up, ffn_w_down, mix_w_in, mix_w_out,
           attn_sink, ssm_a_re, ssm_a_im, ssm_log_dt, ssm_b_re, ssm_b_im, ssm_c_re, ssm_c_im, ssm_d,
           ssm_glu_w, final_g):
    bn, length, d = x.shape
    n_ctx = ctx.shape[1]
    depth = mod_w.shape[0]
    assert depth == 2 and d == D_MODEL

    cond = jnp.zeros((SUBLANES, d), F32).at[:bn].set(c).at[bn].set(c_ctx)
    mods = _modulations(cond, mod_w, mod_b).reshape(depth, SUBLANES, N_MOD, d)
    mods_lat, mods_ctx = mods[:, :bn], mods[:, bn:bn + 1]
    vec = lambda a: a.reshape(1, d)

    w_in = mix_w_in[0][:, _half_split_perm()].astype(BF16)
    w_out = mix_w_out[0].astype(BF16)
    bcs = _channel_dft()
    cos_t, sin_t = _rope_tables(length)
    ffn0 = _ffn_weights(ffn_w_gate[0], ffn_w_up[0], ffn_w_down[0])
    g_mix, g_ffn = vec(norm_g[0, 0]), vec(norm_g[0, 1])

    uv, q, k, v = _pre0(x, mods_lat[0], g_mix, w_in, bcs, cos_t, sin_t, rope=True, rows=TOKEN_TILE)
    uvc, qc, kc, vc = _pre0(ctx, mods_ctx[0], g_mix, w_in, bcs, cos_t, sin_t, rope=False, rows=n_ctx)
    four = _time_dft(*_time_dft_matrices(length), uv)
    four_c = _time_dft(*_time_dft_matrices(n_ctx), uvc)
    att = _window_attention(attn_sink[0], q, k, v, kc, vc)
    att_c = _context_attention(attn_sink[0], qc, kc, vc)
    g_mix1 = vec(norm_g[1, 0])
    h, u_lat = _post_call(_post0_kernel, "mix_out_ffn", x, [mods_lat[0], mods_lat[1]], [four, att],
                          [g_ffn, g_mix1], [w_out, *ffn0], TOKEN_TILE, [F32, BF16])
    _, u_ctx = _post_call(_post0_kernel, "mix_out_ffn_ctx", ctx, [mods_ctx[0], mods_ctx[1]], [four_c, att_c],
                          [g_ffn, g_mix1], [w_out, *ffn0], n_ctx, [F32, BF16])

    g_mix, g_ffn = g_mix1, vec(norm_g[1, 1])
    abar_re, abar_im, bbar_re, bbar_im = _discretize(
        ssm_a_re[0], ssm_a_im[0], ssm_log_dt[0], ssm_b_re[0], ssm_b_im[0])
    w_drive, w_read = _s5_weights(bbar_re, bbar_im, ssm_c_re[0], ssm_c_im[0])
    planes = lambda a: a.reshape(2, S5_PLANES, SUBLANES, LANES)
    abar_re, abar_im = planes(abar_re), planes(abar_im)
    y_dir = [_s5_direction(u_ctx, u_lat, w_drive[i], w_read[i], abar_re[i], abar_im[i], reverse=bool(i))
             for i in range(2)]
    ffn1 = _ffn_weights(ffn_w_gate[1], ffn_w_up[1], ffn_w_down[1])
    return _post_call(_post1_kernel, "s5_glu_ffn_norm", h, [mods_lat[1]], y_dir,
                      [g_mix, g_ffn, vec(final_g), vec(ssm_d[0])],
                      [ssm_glu_w[0].astype(BF16), *ffn1], TOKEN_TILE, [F32])[0]
```

```python
import functools
import math

import numpy as np
import jax
import jax.numpy as jnp
from jax import lax
from jax.experimental import pallas as pl
from jax.experimental.pallas import tpu as pltpu

F32 = jnp.float32
BF16 = jnp.bfloat16

D_MODEL = 1024
GRID_W = 64
N_MOD = 6
N_Q_HEADS = 8
N_KV_HEADS = 2
HEAD_DIM = 64
ATTN_WIDTH = N_Q_HEADS * HEAD_DIM
KV_WIDTH = N_KV_HEADS * HEAD_DIM
FOURIER_GROUPS = 8
FOURIER_GROUP_W = 64
FOURIER_WIDTH = FOURIER_GROUPS * FOURIER_GROUP_W
KV_START = FOURIER_WIDTH + ATTN_WIDTH
MIX_IN_WIDTH = KV_START + 2 * KV_WIDTH
WINDOW = 128
BLOCK = 128
ATTN_SCALE = HEAD_DIM ** -0.5
ROPE_BASE = 10000.0
NEG_INF = -1e30
SSM_GROUP_W = 16
SSM_GROUPS = D_MODEL // SSM_GROUP_W
SSM_STATE = 64
SSM_WIDTH = SSM_GROUPS * SSM_STATE
RMS_EPS = 1e-6

LANES = 128
SUBLANES = 8
VMEM_LIMIT_BYTES = 56 * 1024 * 1024

TOKEN_TILE = 512
FFN_CHUNK = 256
MOD_TILE = 1536
DFT_ROW_TILE = 1024
DFT_K_TILE = 2048
S5_CHUNK = 256
S5_SLAB_GROUPS = LANES // SSM_GROUP_W
S5_SLABS = SSM_GROUPS // S5_SLAB_GROUPS
S5_SLAB_STATES = S5_SLAB_GROUPS * SSM_STATE
S5_COLS = SSM_WIDTH // LANES
S5_PLANES = S5_COLS // SUBLANES


def _cparams(*sem):
    return pltpu.CompilerParams(dimension_semantics=sem, vmem_limit_bytes=VMEM_LIMIT_BYTES)


def _sigmoid(x):
    return 1.0 / (1.0 + jnp.exp(-x))


def _norm_mod(x, g, scale, shift):
    y = x * lax.rsqrt(jnp.mean(x * x, axis=-1, keepdims=True) + RMS_EPS)
    return (y * g) * (1.0 + scale) + shift


def _split_bf16(x):
    hi = x.astype(BF16)
    lo = (x - hi.astype(F32)).astype(BF16)
    return hi, lo


def _mod_kernel(cond_ref, w_ref, b_ref, o_ref):
    x = cond_ref[...]
    s_hi, s_lo = _split_bf16(x * _sigmoid(x))
    w_hi, w_lo = _split_bf16(w_ref[0])
    acc = jnp.dot(s_hi, w_hi, preferred_element_type=F32)
    acc += jnp.dot(s_lo, w_hi, preferred_element_type=F32)
    acc += jnp.dot(s_hi, w_lo, preferred_element_type=F32)
    o_ref[0] = acc + b_ref[0]


def _modulations(cond, mod_w, mod_b):
    depth, d, n = mod_w.shape
    rows = cond.shape[0]
    return pl.pallas_call(
        _mod_kernel,
        grid=(depth, n // MOD_TILE),
        in_specs=[
            pl.BlockSpec((rows, d), lambda l, j: (0, 0)),
            pl.BlockSpec((1, d, MOD_TILE), lambda l, j: (l, 0, j)),
            pl.BlockSpec((1, 1, MOD_TILE), lambda l, j: (l, 0, j)),
        ],
        out_specs=pl.BlockSpec((1, rows, MOD_TILE), lambda l, j: (l, 0, j)),
        out_shape=jax.ShapeDtypeStruct((depth, rows, n), F32),
        compiler_params=_cparams("parallel", "parallel"),
        name="adaln_mod",
    )(cond, mod_w, mod_b.reshape(depth, 1, n))


def _pre0_kernel(x_ref, mod_ref, g_ref, w_ref, bcs_ref, cos_ref, sin_ref,
                 uv_ref, q_ref, k_ref, v_ref, *, rope):
    m = mod_ref[0]
    n = _norm_mod(x_ref[0], g_ref[...], m[1:2], m[0:1])
    p = jnp.dot(n.astype(BF16), w_ref[...], preferred_element_type=F32)
    uv = jnp.dot(p[:, :FOURIER_WIDTH].astype(BF16), bcs_ref[...], preferred_element_type=F32)
    uv_ref[0, 0] = uv[:, :FOURIER_WIDTH].astype(BF16)
    uv_ref[0, 1] = uv[:, FOURIER_WIDTH:].astype(BF16)

    def rot(x):
        if not rope:
            return x
        lane = lax.broadcasted_iota(jnp.int32, x.shape, 1)
        first = (lane & (HEAD_DIM // 2)) == 0
        partner = jnp.where(first, pltpu.roll(x, LANES - HEAD_DIM // 2, 1),
                            pltpu.roll(x, HEAD_DIM // 2, 1))
        return x * cos_ref[...] + partner * sin_ref[...]

    for j in range(ATTN_WIDTH // LANES):
        lo = FOURIER_WIDTH + j * LANES
        q_ref[0, :, j * LANES:(j + 1) * LANES] = (rot(p[:, lo:lo + LANES]) * ATTN_SCALE).astype(BF16)
    k_ref[0] = rot(p[:, KV_START:KV_START + KV_WIDTH]).astype(BF16)
    v_ref[0] = p[:, KV_START + KV_WIDTH:].astype(BF16)


def _pre0(x, mods, g, w_in, bcs, cos_t, sin_t, *, rope, rows):
    b, length, d = x.shape
    grid = (b, length // rows)
    row_map = lambda i, t: (i, t, 0)
    const2 = lambda i, t: (0, 0)
    return pl.pallas_call(
        functools.partial(_pre0_kernel, rope=rope),
        grid=grid,
        in_specs=[
            pl.BlockSpec((1, rows, d), row_map),
            pl.BlockSpec((1, N_MOD, d), lambda i, t: (i if mods.shape[0] > 1 else 0, 0, 0)),
            pl.BlockSpec((1, d), const2),
            pl.BlockSpec((d, MIX_IN_WIDTH), const2),
            pl.BlockSpec((FOURIER_WIDTH, 2 * FOURIER_WIDTH), const2),
            pl.BlockSpec((rows, LANES), lambda i, t: (t, 0)),
            pl.BlockSpec((rows, LANES), lambda i, t: (t, 0)),
        ],
        out_specs=[
            pl.BlockSpec((1, 2, rows, FOURIER_WIDTH), lambda i, t: (i, 0, t, 0)),
            pl.BlockSpec((1, rows, ATTN_WIDTH), row_map),
            pl.BlockSpec((1, rows, KV_WIDTH), row_map),
            pl.BlockSpec((1, rows, KV_WIDTH), row_map),
        ],
        out_shape=[
            jax.ShapeDtypeStruct((b, 2, length, FOURIER_WIDTH), BF16),
            jax.ShapeDtypeStruct((b, length, ATTN_WIDTH), BF16),
            jax.ShapeDtypeStruct((b, length, KV_WIDTH), BF16),
            jax.ShapeDtypeStruct((b, length, KV_WIDTH), BF16),
        ],
        compiler_params=_cparams("parallel", "parallel"),
        name="mix_in_rope" if rope else "mix_in_ctx",
    )(x, mods, g, w_in, bcs, cos_t, sin_t)


def _dft_kernel(c_ref, s_ref, u_ref, v_ref, o_ref, acc_ref):
    k = pl.program_id(2)

    @pl.when(k == 0)
    def _():
        acc_ref[...] = jnp.zeros_like(acc_ref)

    acc_ref[...] += (jnp.dot(c_ref[...], u_ref[0, 0], preferred_element_type=F32)
                     + jnp.dot(s_ref[...], v_ref[0, 0], preferred_element_type=F32))

    @pl.when(k == pl.num_programs(2) - 1)
    def _():
        o_ref[0] = acc_ref[...].astype(BF16)


def _time_dft(cmat, smat, uv):
    b, _, t, w = uv.shape
    tm = min(DFT_ROW_TILE, t)
    tk = min(DFT_K_TILE, t)
    mat_spec = pl.BlockSpec((tm, tk), lambda i, r, k: (r, k))
    return pl.pallas_call(
        _dft_kernel,
        grid=(b, t // tm, t // tk),
        in_specs=[
            mat_spec, mat_spec,
            pl.BlockSpec((1, 1, tk, w), lambda i, r, k: (i, 0, k, 0)),
            pl.BlockSpec((1, 1, tk, w), lambda i, r, k: (i, 1, k, 0)),
        ],
        out_specs=pl.BlockSpec((1, tm, w), lambda i, r, k: (i, r, 0)),
        out_shape=jax.ShapeDtypeStruct((b, t, w), BF16),
        scratch_shapes=[pltpu.VMEM((tm, w), F32)],
        compiler_params=_cparams("parallel", "parallel", "arbitrary"),
        name="time_dft",
    )(cmat, smat, uv, uv)


def _dft_half_kernel(c_ref, s_ref, u_ref, v_ref, lo_ref, mir_ref, p_acc, q_acc):
    k = pl.program_id(2)

    @pl.when(k == 0)
    def _():
        p_acc[...] = jnp.zeros_like(p_acc)
        q_acc[...] = jnp.zeros_like(q_acc)

    p_acc[...] += jnp.dot(c_ref[...], u_ref[0, 0], preferred_element_type=F32)
    q_acc[...] += jnp.dot(s_ref[...], v_ref[0, 0], preferred_element_type=F32)

    @pl.when(k == pl.num_programs(2) - 1)
    def _():
        lo_ref[0] = (p_acc[...] + q_acc[...]).astype(BF16)
        mir_ref[0] = (p_acc[...] - q_acc[...]).astype(BF16)


def _dft_mirror_kernel(lo_ref, ma_ref, mb_ref, o_ref):
    i = pl.program_id(1)
    half_blocks = pl.num_programs(1) // 2

    @pl.when(i < half_blocks)
    def _():
        o_ref[0] = lo_ref[0]

    @pl.when(i >= half_blocks)
    def _():
        r = lax.broadcasted_iota(jnp.int32, (BLOCK, BLOCK), 0)
        c = lax.broadcasted_iota(jnp.int32, (BLOCK, BLOCK), 1)
        rev = jnp.where(c == BLOCK - r, 1.0, 0.0).astype(BF16)
        first = jnp.where((r == 0) & (c == 0), 1.0, 0.0).astype(BF16)
        z = jnp.dot(rev, ma_ref[0], preferred_element_type=F32)
        z += jnp.dot(first, mb_ref[0], preferred_element_type=F32)
        o_ref[0] = z.astype(BF16)


def _time_dft_symmetric(cmat, smat, uv):
    b, _, t, w = uv.shape
    rows = cmat.shape[0]
    assert rows == t // 2 + BLOCK and rows % (2 * SUBLANES) == 0
    tm = rows // 2
    tk = min(DFT_K_TILE, t)
    mat_spec = pl.BlockSpec((tm, tk), lambda i, r, k: (r, k))
    out_spec = pl.BlockSpec((1, tm, w), lambda i, r, k: (i, r, 0))
    lo, mir = pl.pallas_call(
        _dft_half_kernel,
        grid=(b, rows // tm, t // tk),
        in_specs=[
            mat_spec, mat_spec,
            pl.BlockSpec((1, 1, tk, w), lambda i, r, k: (i, 0, k, 0)),
            pl.BlockSpec((1, 1, tk, w), lambda i, r, k: (i, 1, k, 0)),
        ],
        out_specs=[out_spec, out_spec],
        out_shape=[jax.ShapeDtypeStruct((b, rows, w), BF16)] * 2,
        scratch_shapes=[pltpu.VMEM((tm, w), F32), pltpu.VMEM((tm, w), F32)],
        compiler_params=_cparams("parallel", "parallel", "arbitrary"),
        name="time_dft_half",
    )(cmat, smat, uv, uv)
    nb = t // BLOCK
    half = nb // 2
    blk = lambda f: pl.BlockSpec((1, BLOCK, w), f)
    return pl.pallas_call(
        _dft_mirror_kernel,
        grid=(b, nb),
        in_specs=[
            blk(lambda i, n: (i, jnp.minimum(n, half - 1), 0)),
            blk(lambda i, n: (i, jnp.clip(nb - 1 - n, 0, half), 0)),
            blk(lambda i, n: (i, jnp.clip(nb - n, 0, half), 0)),
        ],
        out_specs=blk(lambda i, n: (i, n, 0)),
        out_shape=jax.ShapeDtypeStruct((b, t, w), BF16),
        compiler_params=_cparams("parallel", "parallel"),
        name="time_dft_mirror",
    )(lo, mir, mir)


def _attn_kernel(sink_ref, q_ref, *refs, window):
    if window:
        kp_ref, kc_ref, kn_ref, vp_ref, vc_ref, vn_ref, kx_ref, vx_ref, o_ref, s_scr, p_scr = refs
        kcat = jnp.concatenate([kx_ref[0], kp_ref[0], kc_ref[0], kn_ref[0]], axis=0)
        vcat = jnp.concatenate([vx_ref[0], vp_ref[0], vc_ref[0], vn_ref[0]], axis=0)
    else:
        kx_ref, vx_ref, o_ref, s_scr, p_scr = refs
        kcat, vcat = kx_ref[0], vx_ref[0]
    n_keys = kcat.shape[0]
    n_q = q_ref.shape[1]

    lane = lax.broadcasted_iota(jnp.int32, kcat.shape, 1)
    low = lane < HEAD_DIM
    zero = jnp.zeros_like(kcat)
    kswap = pltpu.roll(kcat.astype(F32), HEAD_DIM, 1).astype(BF16)
    vswap = pltpu.roll(vcat.astype(F32), HEAD_DIM, 1).astype(BF16)
    k_lo = (jnp.where(low, kcat, zero), jnp.where(low, kswap, zero))
    k_hi = (jnp.where(low, zero, kswap), jnp.where(low, zero, kcat))
    v_lo = (jnp.where(low, vcat, zero), jnp.where(low, vswap, zero))
    v_hi = (jnp.where(low, zero, vswap), jnp.where(low, zero, vcat))

    if window:
        blk = pl.program_id(1)
        last = pl.num_programs(1) - 1
        n_ctx = kx_ref.shape[1]
        row = lax.broadcasted_iota(jnp.int32, (n_q, BLOCK), 0)
        col = lax.broadcasted_iota(jnp.int32, (n_q, BLOCK), 1)
        off_prev = jnp.where(blk > 0, 0, BLOCK)
        off_next = jnp.where(blk < last, 0, BLOCK)
        ok_prev = col >= row + off_prev
        ok_next = col <= row - off_next

    q_per_kv = N_Q_HEADS // N_KV_HEADS
    n_pairs = ATTN_WIDTH // LANES
    for j in range(n_pairs):
        g = (2 * j) // q_per_kv
        qp = q_ref[0, :, j * LANES:(j + 1) * LANES]
        for half, k_ext in enumerate((k_lo[g], k_hi[g])):
            s = lax.dot_general(qp, k_ext, (((1,), (1,)), ((), ())), preferred_element_type=F32)
            head = 2 * j + half
            if window:
                s_scr[head, :, :n_ctx] = s[:, :n_ctx]
                s_scr[head, :, n_ctx:n_ctx + BLOCK] = jnp.where(ok_prev, s[:, n_ctx:n_ctx + BLOCK], NEG_INF)
                s_scr[head, :, n_ctx + BLOCK:n_ctx + 2 * BLOCK] = s[:, n_ctx + BLOCK:n_ctx + 2 * BLOCK]
                s_scr[head, :, n_ctx + 2 * BLOCK:] = jnp.where(ok_next, s[:, n_ctx + 2 * BLOCK:], NEG_INF)
            else:
                s_scr[head] = s
    inv = []
    for head in range(N_Q_HEADS):
        s = s_scr[head]
        sink = sink_ref[head]
        mx = jnp.maximum(jnp.max(s, axis=1, keepdims=True), sink)
        p = jnp.exp(s - mx)
        inv.append(1.0 / (jnp.sum(p, axis=1, keepdims=True) + jnp.exp(sink - mx)))
        p_scr[head] = p.astype(BF16)
    for j in range(n_pairs):
        g = (2 * j) // q_per_kv
        out = jnp.dot(p_scr[2 * j], v_lo[g], preferred_element_type=F32) * inv[2 * j]
        out += jnp.dot(p_scr[2 * j + 1], v_hi[g], preferred_element_type=F32) * inv[2 * j + 1]
        o_ref[0, :, j * LANES:(j + 1) * LANES] = out.astype(BF16)


def _attn_scratch(n_q, n_keys):
    return [pltpu.VMEM((N_Q_HEADS, n_q, n_keys), F32), pltpu.VMEM((N_Q_HEADS, n_q, n_keys), BF16)]


def _window_attention(sink, q, k, v, kx, vx):
    b, length, _ = q.shape
    nb = length // BLOCK
    n_ctx = kx.shape[1]
    kv_spec = lambda f: pl.BlockSpec((1, BLOCK, KV_WIDTH), f)
    prev = lambda i, n: (i, jnp.maximum(n - 1, 0), 0)
    cur = lambda i, n: (i, n, 0)
    nxt = lambda i, n: (i, jnp.minimum(n + 1, nb - 1), 0)
    ctx_spec = pl.BlockSpec((1, n_ctx, KV_WIDTH), lambda i, n: (i, 0, 0))
    return pl.pallas_call(
        functools.partial(_attn_kernel, window=True),
        grid=(b, nb),
        in_specs=[
            pl.BlockSpec(memory_space=pltpu.SMEM),
            pl.BlockSpec((1, BLOCK, ATTN_WIDTH), cur),
            kv_spec(prev), kv_spec(cur), kv_spec(nxt),
            kv_spec(prev), kv_spec(cur), kv_spec(nxt),
            ctx_spec, ctx_spec,
        ],
        out_specs=pl.BlockSpec((1, BLOCK, ATTN_WIDTH), cur),
        out_shape=jax.ShapeDtypeStruct((b, length, ATTN_WIDTH), BF16),
        scratch_shapes=_attn_scratch(BLOCK, n_ctx + 3 * BLOCK),
        compiler_params=_cparams("parallel", "parallel"),
        name="window_attn",
    )(sink, q, k, k, k, v, v, v, kx, vx)


def _context_attention(sink, q, kx, vx):
    b, n_ctx, _ = q.shape
    spec = lambda w: pl.BlockSpec((1, n_ctx, w), lambda i: (i, 0, 0))
    return pl.pallas_call(
        functools.partial(_attn_kernel, window=False),
        grid=(b,),
        in_specs=[pl.BlockSpec(memory_space=pltpu.SMEM), spec(ATTN_WIDTH), spec(KV_WIDTH), spec(KV_WIDTH)],
        out_specs=spec(ATTN_WIDTH),
        out_shape=jax.ShapeDtypeStruct((b, n_ctx, ATTN_WIDTH), BF16),
        scratch_shapes=_attn_scratch(n_ctx, n_ctx),
        compiler_params=_cparams("parallel"),
        name="context_attn",
    )(sink, q, kx, vx)


def _swiglu_residual(h1, m, g2, wg_ref, wu_ref, wd_ref, acc_ref):
    n2 = _norm_mod(h1, g2, m[4:5], m[3:4]).astype(BF16)
    for c in range(wg_ref.shape[1] // FFN_CHUNK):
        cols = slice(c * FFN_CHUNK, (c + 1) * FFN_CHUNK)
        gate = jnp.dot(n2, wg_ref[:, cols], preferred_element_type=F32)
        up = jnp.dot(n2, wu_ref[:, cols], preferred_element_type=F32)
        act = (gate * _sigmoid(gate)) * up
        part = jnp.dot(act.astype(BF16), wd_ref[cols, :], preferred_element_type=F32)
        if c == 0:
            acc_ref[...] = part
        else:
            acc_ref[...] += part
    return h1 + m[5:6] * acc_ref[...]


def _post0_kernel(h_ref, mod_ref, nmod_ref, g2_ref, gn_ref, f_ref, a_ref, wo_ref, wg_ref, wu_ref, wd_ref,
                  o_ref, u_ref, acc_ref):
    m = mod_ref[0]
    mix = jnp.dot(f_ref[0], wo_ref[:FOURIER_WIDTH, :], preferred_element_type=F32)
    mix += jnp.dot(a_ref[0], wo_ref[FOURIER_WIDTH:, :], preferred_element_type=F32)
    h1 = h_ref[0] + m[2:3] * mix
    h2 = _swiglu_residual(h1, m, g2_ref[...], wg_ref, wu_ref, wd_ref, acc_ref)
    o_ref[0] = h2
    nm = nmod_ref[0]
    u_ref[0] = _norm_mod(h2, gn_ref[...], nm[1:2], nm[0:1]).astype(BF16)


def _gelu_tanh(x):
    sqrt_2_over_pi = np.sqrt(2 / np.pi).astype(np.float32)
    cdf = 0.5 * (1.0 + jnp.tanh(sqrt_2_over_pi * (x + 0.044715 * (x ** 3))))
    return x * cdf


def _post1_kernel(h_ref, mod_ref, g1_ref, g2_ref, gf_ref, dskip_ref, yf_ref, yb_ref, wglu_ref,
                  wg_ref, wu_ref, wd_ref, o_ref, acc_ref):
    m = mod_ref[0]
    h = h_ref[0]
    u = _norm_mod(h, g1_ref[...], m[1:2], m[0:1])
    y = dskip_ref[...] * u + yf_ref[0] + yb_ref[0]
    z = jnp.dot(_gelu_tanh(y).astype(BF16), wglu_ref[...], preferred_element_type=F32)
    h1 = h + m[2:3] * (z[:, :D_MODEL] * _sigmoid(z[:, D_MODEL:]))
    h2 = _swiglu_residual(h1, m, g2_ref[...], wg_ref, wu_ref, wd_ref, acc_ref)
    o_ref[0] = (h2 * lax.rsqrt(jnp.mean(h2 * h2, axis=-1, keepdims=True) + RMS_EPS)) * gf_ref[...]


def _resident(shape):
    zeros = (0,) * len(shape)
    return pl.BlockSpec(shape, lambda i, t: zeros, pipeline_mode=pl.Buffered(1))


def _post_call(kernel, name, h, mods, row_inputs, vec_inputs, weights, rows, out_dtypes):
    b, length, d = h.shape
    row_map = lambda i, t: (i, t, 0)
    mod_spec = lambda m: pl.BlockSpec((1, N_MOD, d), lambda i, t: (i if m.shape[0] > 1 else 0, 0, 0))
    in_specs = [pl.BlockSpec((1, rows, d), row_map)]
    in_specs += [mod_spec(m) for m in mods]
    in_specs += [pl.BlockSpec((1, d), lambda i, t: (0, 0)) for _ in vec_inputs]
    in_specs += [pl.BlockSpec((1, rows, a.shape[-1]), row_map) for a in row_inputs]
    in_specs += [_resident(w.shape) for w in weights]
    return pl.pallas_call(
        kernel,
        grid=(b, length // rows),
        in_specs=in_specs,
        out_specs=[pl.BlockSpec((1, rows, d), row_map) for _ in out_dtypes],
        out_shape=[jax.ShapeDtypeStruct((b, length, d), dt) for dt in out_dtypes],
        scratch_shapes=[pltpu.VMEM((rows, d), F32)],
        compiler_params=_cparams("parallel", "parallel"),
        name=name,
    )(h, *mods, *vec_inputs, *row_inputs, *weights)


def _disc_kernel(are_ref, aim_ref, ldt_ref, bre_ref, bim_ref, oar_ref, oai_ref, obr_ref, obi_ref):
    a_re, a_im = are_ref[0], aim_ref[0]
    dt = jnp.exp(ldt_ref[0])
    mag = jnp.exp(a_re * dt)
    abar_re, abar_im = mag * jnp.cos(a_im * dt), mag * jnp.sin(a_im * dt)
    nr, ni = abar_re - 1.0, abar_im
    den = a_re * a_re + a_im * a_im
    f_re = (nr * a_re + ni * a_im) / den
    f_im = (ni * a_re - nr * a_im) / den
    oar_ref[0] = abar_re
    oai_ref[0] = abar_im
    obr_ref[0] = f_re[None] * bre_ref[0] - f_im[None] * bim_ref[0]
    obi_ref[0] = f_re[None] * bim_ref[0] + f_im[None] * bre_ref[0]


def _discretize(a_re, a_im, log_dt, b_re, b_im):
    nd = a_re.shape[0]
    rows = SSM_WIDTH // LANES
    flat = lambda a: a.reshape(nd, rows, LANES)
    ldt = jnp.broadcast_to(log_dt[:, :, None], (nd, SSM_GROUPS, SSM_STATE))
    chan = lambda a: a.reshape(nd, SSM_WIDTH, SSM_GROUP_W).transpose(0, 2, 1).reshape(nd, SSM_GROUP_W, rows, LANES)
    aspec = pl.BlockSpec((1, rows, LANES), lambda i: (i, 0, 0))
    bspec = pl.BlockSpec((1, SSM_GROUP_W, rows, LANES), lambda i: (i, 0, 0, 0))
    ashape = jax.ShapeDtypeStruct((nd, rows, LANES), F32)
    bshape = jax.ShapeDtypeStruct((nd, SSM_GROUP_W, rows, LANES), F32)
    return pl.pallas_call(
        _disc_kernel,
        grid=(nd,),
        in_specs=[aspec, aspec, aspec, bspec, bspec],
        out_specs=[aspec, aspec, bspec, bspec],
        out_shape=[ashape, ashape, bshape, bshape],
        compiler_params=_cparams("parallel"),
        name="s5_discretize",
    )(flat(a_re), flat(a_im), flat(ldt), chan(b_re), chan(b_im))


def _s5_kernel(uc_ref, ul_ref, wd_ref, wr_ref, are_ref, aim_ref, y_ref,
               xa_re, xa_im, xb_re, xb_im, hre, him, *, reverse):
    step_idx = pl.program_id(1)
    steps = S5_CHUNK
    cols_per_slab = S5_SLAB_STATES // LANES
    seg = steps // S5_SLABS

    @pl.when(step_idx == 0)
    def _():
        hre[...] = jnp.zeros_like(hre)
        him[...] = jnp.zeros_like(him)

    def plane_of(col):
        return col // SUBLANES, col % SUBLANES

    def drive_slab(u, kk, xre, xim):
        bu = jnp.dot(u(kk * LANES, (kk + 1) * LANES), wd_ref[kk], preferred_element_type=F32)
        for c in range(cols_per_slab):
            pln, sub = plane_of(kk * cols_per_slab + c)
            xre[pln, pl.ds(sub, steps, stride=SUBLANES), :] = bu[:, c * LANES:(c + 1) * LANES]
            xim[pln, pl.ds(sub, steps, stride=SUBLANES), :] = (
                bu[:, S5_SLAB_STATES + c * LANES:S5_SLAB_STATES + (c + 1) * LANES])

    def read_slab(kk, xre, xim):
        parts = []
        for src in (xre, xim):
            for c in range(cols_per_slab):
                pln, sub = plane_of(kk * cols_per_slab + c)
                parts.append(src[pln, pl.ds(sub, steps, stride=SUBLANES), :].astype(BF16))
        return jnp.dot(jnp.concatenate(parts, axis=1), wr_ref[kk], preferred_element_type=F32)

    a_r = [are_ref[p] for p in range(S5_PLANES)]
    a_i = [aim_ref[p] for p in range(S5_PLANES)]

    def one_step(t, h, xre, xim):
        h_r, h_i = h
        r0 = t * SUBLANES if isinstance(t, int) else pl.multiple_of(t * SUBLANES, SUBLANES)
        new_r, new_i = [], []
        for p in range(S5_PLANES):
            n_r = a_r[p] * h_r[p] - a_i[p] * h_i[p] + xre[p, pl.ds(r0, SUBLANES), :]
            n_i = a_r[p] * h_i[p] + a_i[p] * h_r[p] + xim[p, pl.ds(r0, SUBLANES), :]
            xre[p, pl.ds(r0, SUBLANES), :] = n_r
            xim[p, pl.ds(r0, SUBLANES), :] = n_i
            new_r.append(n_r)
            new_i.append(n_i)
        return tuple(new_r), tuple(new_i)

    def time_of(i):
        return (steps - 1 - i) if reverse else i

    def scan_segment(k, h, xre, xim):
        for i in range(k * seg, (k + 1) * seg):
            h = one_step(time_of(i), h, xre, xim)
        return h

    def load_state():
        return (tuple(hre[p] for p in range(S5_PLANES)), tuple(him[p] for p in range(S5_PLANES)))

    def store_state(h):
        for p in range(S5_PLANES):
            hre[p] = h[0][p]
            him[p] = h[1][p]

    @pl.when(step_idx == 0)
    def _():
        u = lambda lo, hi: uc_ref[0, :, lo:hi]
        for kk in range(S5_SLABS):
            drive_slab(u, kk, xa_re, xa_im)
        h = lax.fori_loop(0, steps, lambda i, h: one_step(time_of(i), h, xa_re, xa_im), load_state(), unroll=8)
        store_state(h)

    @pl.when(step_idx > 0)
    def _():
        first, second = slice(0, steps), slice(steps, 2 * steps)
        rows_a, rows_b = (second, first) if reverse else (first, second)
        u_a = lambda lo, hi: ul_ref[0, rows_a, lo:hi]
        u_b = lambda lo, hi: ul_ref[0, rows_b, lo:hi]
        for kk in range(S5_SLABS):
            drive_slab(u_a, kk, xa_re, xa_im)
        h = load_state()
        for kk in range(S5_SLABS):
            drive_slab(u_b, kk, xb_re, xb_im)
            h = scan_segment(kk, h, xa_re, xa_im)
        for kk in range(S5_SLABS):
            h = scan_segment(kk, h, xb_re, xb_im)
            y_ref[0, rows_a, kk * LANES:(kk + 1) * LANES] = read_slab(kk, xa_re, xa_im)
        store_state(h)
        for kk in range(S5_SLABS):
            y_ref[0, rows_b, kk * LANES:(kk + 1) * LANES] = read_slab(kk, xb_re, xb_im)


def _s5_direction(u_ctx, u_lat, w_drive, w_read, a_re, a_im, *, reverse):
    b, length, d = u_lat.shape
    n_ctx = u_ctx.shape[1]
    pair = 2 * S5_CHUNK
    assert n_ctx == S5_CHUNK and length % pair == 0
    n_pairs = length // pair
    if reverse:
        lat_map = lambda i, n: (i, jnp.minimum(n_pairs - n, n_pairs - 1), 0)
    else:
        lat_map = lambda i, n: (i, jnp.maximum(n - 1, 0), 0)
    whole = lambda shape: pl.BlockSpec(shape, lambda i, n: (0,) * len(shape))
    plane_buf = pltpu.VMEM((S5_PLANES, S5_CHUNK * SUBLANES, LANES), F32)
    state_buf = pltpu.VMEM((S5_PLANES, SUBLANES, LANES), F32)
    return pl.pallas_call(
        functools.partial(_s5_kernel, reverse=reverse),
        grid=(b, n_pairs + 1),
        in_specs=[
            pl.BlockSpec((1, n_ctx, d), lambda i, n: (i, 0, 0)),
            pl.BlockSpec((1, pair, d), lat_map),
            whole(w_drive.shape), whole(w_read.shape), whole(a_re.shape), whole(a_im.shape),
        ],
        out_specs=pl.BlockSpec((1, pair, d), lat_map),
        out_shape=jax.ShapeDtypeStruct((b, length, d), F32),
        scratch_shapes=[plane_buf, plane_buf, plane_buf, plane_buf, state_buf, state_buf],
        compiler_params=_cparams("parallel", "arbitrary"),
        name="s5_scan_bwd" if reverse else "s5_scan_fwd",
    )(u_ctx, u_lat, w_drive, w_read, a_re, a_im)


def _rope_tables(length):
    rows = length // GRID_W
    t_row = jnp.repeat(jnp.arange(rows, dtype=F32), GRID_W)
    t_col = jnp.tile(jnp.arange(GRID_W, dtype=F32), rows)
    n_freq = HEAD_DIM // 4
    inv_freq = ROPE_BASE ** (-jnp.arange(n_freq, dtype=F32) / n_freq)
    ang = jnp.concatenate([t_row[:, None] * inv_freq, t_col[:, None] * inv_freq], axis=-1)
    cos, sin = jnp.cos(ang), jnp.sin(ang)
    reps = LANES // HEAD_DIM
    return (jnp.tile(jnp.concatenate([cos, cos], axis=-1), (1, reps)),
            jnp.tile(jnp.concatenate([-sin, sin], axis=-1), (1, reps)))


def _half_split_perm():
    perm = np.arange(MIX_IN_WIDTH)
    head = np.concatenate([np.arange(0, HEAD_DIM, 2), np.arange(1, HEAD_DIM, 2)])
    for start in range(FOURIER_WIDTH, KV_START + KV_WIDTH, HEAD_DIM):
        perm[start:start + HEAD_DIM] = start + head
    return perm


def _channel_dft():
    c = np.arange(FOURIER_GROUP_W)
    ang = 2.0 * np.pi * np.outer(c, c) / FOURIER_GROUP_W
    eye = np.eye(FOURIER_GROUPS)
    scale = FOURIER_GROUP_W ** -0.5
    table = np.concatenate([np.kron(eye, np.cos(ang) * scale), np.kron(eye, np.sin(ang) * scale)], axis=1)
    return jnp.asarray(table, F32).astype(BF16)


def _time_dft_matrices(t, rows=None):
    r1 = 1 << ((t.bit_length() - 1 + 1) // 2)
    r2 = t // r1
    assert r1 * r2 == t and r2 % SUBLANES == 0
    rows = t if rows is None else rows
    assert rows % r2 == 0
    n = jnp.arange(t, dtype=jnp.int32)[None, :]
    unit = 2.0 * math.pi / t
    ang_a = (((jnp.arange(rows // r2, dtype=jnp.int32) * r2)[:, None] * n) % t).astype(F32) * unit
    ang_b = ((jnp.arange(r2, dtype=jnp.int32)[:, None] * n) % t).astype(F32) * unit
    scale = t ** -0.5
    ca, sa = jnp.cos(ang_a)[:, None, :], jnp.sin(ang_a)[:, None, :]
    cb, sb = (jnp.cos(ang_b) * scale)[None, :, :], (jnp.sin(ang_b) * scale)[None, :, :]
    cmat = (ca * cb - sa * sb).reshape(rows, t).astype(BF16)
    smat = (-(sa * cb + ca * sb)).reshape(rows, t).astype(BF16)
    return cmat, smat


def _ffn_weights(w_gate, w_up, w_down):
    return w_gate.astype(BF16), w_up.astype(BF16), w_down.astype(BF16)


def _s5_weights(bbar_re, bbar_im, c_re, c_im):
    nd = bbar_re.shape[0]
    eye = jnp.eye(S5_SLAB_GROUPS, dtype=F32)

    def drive(bb):
        bb = bb.reshape(nd, SSM_GROUP_W, S5_SLABS, S5_SLAB_GROUPS, SSM_STATE)
        w = jnp.einsum("dhkgp,gG->dkghGp", bb, eye)
        return w.reshape(nd, S5_SLABS, LANES, S5_SLAB_STATES)

    def read(cc):
        cc = cc.astype(F32).reshape(nd, S5_SLABS, S5_SLAB_GROUPS, SSM_GROUP_W, SSM_STATE)
        w = jnp.einsum("dkghp,gG->dkgpGh", cc, eye)
        return w.reshape(nd, S5_SLABS, S5_SLAB_STATES, LANES)

    w_drive = jnp.concatenate([drive(bbar_re), drive(bbar_im)], axis=-1).astype(BF16)
    w_read = jnp.concatenate([read(c_re), -read(c_im)], axis=2).astype(BF16)
    return w_drive, w_read


def kernel(x, c, ctx, c_ctx, mod_w, mod_b, norm_g, ffn_w_gate, ffn_w_up, ffn_w_down, mix_w_in, mix_w_out,
           attn_sink, ssm_a_re, ssm_a_im, ssm_log_dt, ssm_b_re, ssm_b_im, ssm_c_re, ssm_c_im, ssm_d,
           ssm_glu_w, final_g):
    bn, length, d = x.shape
    n_ctx = ctx.shape[1]
    depth = mod_w.shape[0]
    assert depth == 2 and d == D_MODEL

    cond = jnp.zeros((SUBLANES, d), F32).at[:bn].set(c).at[bn].set(c_ctx)
    mods = _modulations(cond, mod_w, mod_b).reshape(depth, SUBLANES, N_MOD, d)
    mods_lat, mods_ctx = mods[:, :bn], mods[:, bn:bn + 1]
    vec = lambda a: a.reshape(1, d)

    w_in = mix_w_in[0][:, _half_split_perm()].astype(BF16)
    w_out = mix_w_out[0].astype(BF16)
    bcs = _channel_dft()
    cos_t, sin_t = _rope_tables(length)
    ffn0 = _ffn_weights(ffn_w_gate[0], ffn_w_up[0], ffn_w_down[0])
    g_mix, g_ffn = vec(norm_g[0, 0]), vec(norm_g[0, 1])

    uv, q, k, v = _pre0(x, mods_lat[0], g_mix, w_in, bcs, cos_t, sin_t, rope=True, rows=TOKEN_TILE)
    uvc, qc, kc, vc = _pre0(ctx, mods_ctx[0], g_mix, w_in, bcs, cos_t, sin_t, rope=False, rows=n_ctx)
    four = _time_dft_symmetric(*_time_dft_matrices(length, length // 2 + BLOCK), uv)
    four_c = _time_dft(*_time_dft_matrices(n_ctx), uvc)
    att = _window_attention(attn_sink[0], q, k, v, kc, vc)
    att_c = _context_attention(attn_sink[0], qc, kc, vc)
    g_mix1 = vec(norm_g[1, 0])
    h, u_lat = _post_call(_post0_kernel, "mix_out_ffn", x, [mods_lat[0], mods_lat[1]], [four, att],
                          [g_ffn, g_mix1], [w_out, *ffn0], TOKEN_TILE, [F32, BF16])
    _, u_ctx = _post_call(_post0_kernel, "mix_out_ffn_ctx", ctx, [mods_ctx[0], mods_ctx[1]], [four_c, att_c],
                          [g_ffn, g_mix1], [w_out, *ffn0], n_ctx, [F32, BF16])

    g_mix, g_ffn = g_mix1, vec(norm_g[1, 1])
    abar_re, abar_im, bbar_re, bbar_im = _discretize(
        ssm_a_re[0], ssm_a_im[0], ssm_log_dt[0], ssm_b_re[0], ssm_b_im[0])
    w_drive, w_read = _s5_weights(bbar_re, bbar_im, ssm_c_re[0], ssm_c_im[0])
    planes = lambda a: a.reshape(2, S5_PLANES, SUBLANES, LANES)
    abar_re, abar_im = planes(abar_re), planes(abar_im)
    y_dir = [_s5_direction(u_ctx, u_lat, w_drive[i], w_read[i], abar_re[i], abar_im[i], reverse=bool(i))
             for i in range(2)]
    ffn1 = _ffn_weights(ffn_w_gate[1], ffn_w_up[1], ffn_w_down[1])
    return _post_call(_post1_kernel, "s5_glu_ffn_norm", h, [mods_lat[1]], y_dir,
                      [g_mix, g_ffn, vec(final_g), vec(ssm_d[0])],
                      [ssm_glu_w[0].astype(BF16), *ffn1], TOKEN_TILE, [F32])[0]
```

```python
import functools
import math

import numpy as np
import jax
import jax.numpy as jnp
from jax import lax
from jax.experimental import pallas as pl
from jax.experimental.pallas import tpu as pltpu

F32 = jnp.float32
BF16 = jnp.bfloat16

D_MODEL = 1024
GRID_W = 64
N_MOD = 6
N_Q_HEADS = 8
N_KV_HEADS = 2
HEAD_DIM = 64
ATTN_WIDTH = N_Q_HEADS * HEAD_DIM
KV_WIDTH = N_KV_HEADS * HEAD_DIM
FOURIER_GROUPS = 8
FOURIER_GROUP_W = 64
FOURIER_WIDTH = FOURIER_GROUPS * FOURIER_GROUP_W
KV_START = FOURIER_WIDTH + ATTN_WIDTH
MIX_IN_WIDTH = KV_START + 2 * KV_WIDTH
WINDOW = 128
BLOCK = 128
ATTN_SCALE = HEAD_DIM ** -0.5
ROPE_BASE = 10000.0
NEG_INF = -1e30
SSM_GROUP_W = 16
SSM_GROUPS = D_MODEL // SSM_GROUP_W
SSM_STATE = 64
SSM_WIDTH = SSM_GROUPS * SSM_STATE
RMS_EPS = 1e-6

LANES = 128
SUBLANES = 8
VMEM_LIMIT_BYTES = 56 * 1024 * 1024

TOKEN_TILE = 512
FFN_CHUNK = 256
MOD_TILE = 1536
DFT_ROW_TILE = 1024
DFT_K_TILE = 2048
S5_CHUNK = 256
S5_SLAB_GROUPS = LANES // SSM_GROUP_W
S5_SLABS = SSM_GROUPS // S5_SLAB_GROUPS
S5_SLAB_STATES = S5_SLAB_GROUPS * SSM_STATE
S5_COLS = SSM_WIDTH // LANES
S5_PLANES = S5_COLS // SUBLANES


def _cparams(*sem):
    return pltpu.CompilerParams(dimension_semantics=sem, vmem_limit_bytes=VMEM_LIMIT_BYTES)


def _sigmoid(x):
    return 1.0 / (1.0 + jnp.exp(-x))


def _norm_mod(x, g, scale, shift):
    y = x * lax.rsqrt(jnp.mean(x * x, axis=-1, keepdims=True) + RMS_EPS)
    return (y * g) * (1.0 + scale) + shift


def _split_bf16(x):
    hi = x.astype(BF16)
    lo = (x - hi.astype(F32)).astype(BF16)
    return hi, lo


def _mod_kernel(cond_ref, w_ref, b_ref, o_ref):
    x = cond_ref[...]
    s_hi, s_lo = _split_bf16(x * _sigmoid(x))
    w_hi, w_lo = _split_bf16(w_ref[0])
    acc = jnp.dot(s_hi, w_hi, preferred_element_type=F32)
    acc += jnp.dot(s_lo, w_hi, preferred_element_type=F32)
    acc += jnp.dot(s_hi, w_lo, preferred_element_type=F32)
    o_ref[0] = acc + b_ref[0]


def _modulations(cond, mod_w, mod_b):
    depth, d, n = mod_w.shape
    rows = cond.shape[0]
    return pl.pallas_call(
        _mod_kernel,
        grid=(depth, n // MOD_TILE),
        in_specs=[
            pl.BlockSpec((rows, d), lambda l, j: (0, 0)),
            pl.BlockSpec((1, d, MOD_TILE), lambda l, j: (l, 0, j)),
            pl.BlockSpec((1, 1, MOD_TILE), lambda l, j: (l, 0, j)),
        ],
        out_specs=pl.BlockSpec((1, rows, MOD_TILE), lambda l, j: (l, 0, j)),
        out_shape=jax.ShapeDtypeStruct((depth, rows, n), F32),
        compiler_params=_cparams("parallel", "parallel"),
        name="adaln_mod",
    )(cond, mod_w, mod_b.reshape(depth, 1, n))


def _pre0_kernel(x_ref, mod_ref, g_ref, w_ref, bcs_ref, cos_ref, sin_ref,
                 uv_ref, q_ref, k_ref, v_ref, *, rope):
    m = mod_ref[0]
    n = _norm_mod(x_ref[0], g_ref[...], m[1:2], m[0:1])
    p = jnp.dot(n.astype(BF16), w_ref[...], preferred_element_type=F32)
    uv = jnp.dot(p[:, :FOURIER_WIDTH].astype(BF16), bcs_ref[...], preferred_element_type=F32)
    uv_ref[0, 0] = uv[:, :FOURIER_WIDTH].astype(BF16)
    uv_ref[0, 1] = uv[:, FOURIER_WIDTH:].astype(BF16)

    def rot(x):
        if not rope:
            return x
        lane = lax.broadcasted_iota(jnp.int32, x.shape, 1)
        first = (lane & (HEAD_DIM // 2)) == 0
        partner = jnp.where(first, pltpu.roll(x, LANES - HEAD_DIM // 2, 1),
                            pltpu.roll(x, HEAD_DIM // 2, 1))
        return x * cos_ref[...] + partner * sin_ref[...]

    for j in range(ATTN_WIDTH // LANES):
        lo = FOURIER_WIDTH + j * LANES
        q_ref[0, :, j * LANES:(j + 1) * LANES] = (rot(p[:, lo:lo + LANES]) * ATTN_SCALE).astype(BF16)
    k_ref[0] = rot(p[:, KV_START:KV_START + KV_WIDTH]).astype(BF16)
    v_ref[0] = p[:, KV_START + KV_WIDTH:].astype(BF16)


def _pre0(x, mods, g, w_in, bcs, cos_t, sin_t, *, rope, rows):
    b, length, d = x.shape
    grid = (b, length // rows)
    row_map = lambda i, t: (i, t, 0)
    const2 = lambda i, t: (0, 0)
    return pl.pallas_call(
        functools.partial(_pre0_kernel, rope=rope),
        grid=grid,
        in_specs=[
            pl.BlockSpec((1, rows, d), row_map),
            pl.BlockSpec((1, N_MOD, d), lambda i, t: (i if mods.shape[0] > 1 else 0, 0, 0)),
            pl.BlockSpec((1, d), const2),
            pl.BlockSpec((d, MIX_IN_WIDTH), const2),
            pl.BlockSpec((FOURIER_WIDTH, 2 * FOURIER_WIDTH), const2),
            pl.BlockSpec((rows, LANES), lambda i, t: (t, 0)),
            pl.BlockSpec((rows, LANES), lambda i, t: (t, 0)),
        ],
        out_specs=[
            pl.BlockSpec((1, 2, rows, FOURIER_WIDTH), lambda i, t: (i, 0, t, 0)),
            pl.BlockSpec((1, rows, ATTN_WIDTH), row_map),
            pl.BlockSpec((1, rows, KV_WIDTH), row_map),
            pl.BlockSpec((1, rows, KV_WIDTH), row_map),
        ],
        out_shape=[
            jax.ShapeDtypeStruct((b, 2, length, FOURIER_WIDTH), BF16),
            jax.ShapeDtypeStruct((b, length, ATTN_WIDTH), BF16),
            jax.ShapeDtypeStruct((b, length, KV_WIDTH), BF16),
            jax.ShapeDtypeStruct((b, length, KV_WIDTH), BF16),
        ],
        compiler_params=_cparams("parallel", "parallel"),
        name="mix_in_rope" if rope else "mix_in_ctx",
    )(x, mods, g, w_in, bcs, cos_t, sin_t)


def _dft_kernel(c_ref, s_ref, u_ref, v_ref, o_ref, acc_ref):
    k = pl.program_id(2)

    @pl.when(k == 0)
    def _():
        acc_ref[...] = jnp.zeros_like(acc_ref)

    acc_ref[...] += (jnp.dot(c_ref[...], u_ref[0, 0], preferred_element_type=F32)
                     + jnp.dot(s_ref[...], v_ref[0, 0], preferred_element_type=F32))

    @pl.when(k == pl.num_programs(2) - 1)
    def _():
        o_ref[0] = acc_ref[...].astype(BF16)


def _time_dft(cmat, smat, uv):
    b, _, t, w = uv.shape
    tm = min(DFT_ROW_TILE, t)
    tk = min(DFT_K_TILE, t)
    mat_spec = pl.BlockSpec((tm, tk), lambda i, r, k: (r, k))
    return pl.pallas_call(
        _dft_kernel,
        grid=(b, t // tm, t // tk),
        in_specs=[
            mat_spec, mat_spec,
            pl.BlockSpec((1, 1, tk, w), lambda i, r, k: (i, 0, k, 0)),
            pl.BlockSpec((1, 1, tk, w), lambda i, r, k: (i, 1, k, 0)),
        ],
        out_specs=pl.BlockSpec((1, tm, w), lambda i, r, k: (i, r, 0)),
        out_shape=jax.ShapeDtypeStruct((b, t, w), BF16),
        scratch_shapes=[pltpu.VMEM((tm, w), F32)],
        compiler_params=_cparams("parallel", "parallel", "arbitrary"),
        name="time_dft",
    )(cmat, smat, uv, uv)


def _dft_half_kernel(c_ref, s_ref, u_ref, v_ref, lo_ref, mir_ref, p_acc, q_acc):
    k = pl.program_id(2)

    @pl.when(k == 0)
    def _():
        p_acc[...] = jnp.zeros_like(p_acc)
        q_acc[...] = jnp.zeros_like(q_acc)

    p_acc[...] += jnp.dot(c_ref[...], u_ref[0, 0], preferred_element_type=F32)
    q_acc[...] += jnp.dot(s_ref[...], v_ref[0, 0], preferred_element_type=F32)

    @pl.when(k == pl.num_programs(2) - 1)
    def _():
        lo_ref[0] = (p_acc[...] + q_acc[...]).astype(BF16)
        mir_ref[0] = (p_acc[...] - q_acc[...]).astype(BF16)


def _dft_mirror_kernel(lo_ref, mir_ref, o_ref):
    i = pl.program_id(1)
    half = o_ref.shape[1]

    @pl.when(i == 0)
    def _():
        o_ref[0] = lo_ref[0, :half, :]

    @pl.when(i == 1)
    def _():
        r = lax.broadcasted_iota(jnp.int32, (BLOCK, BLOCK), 0)
        c = lax.broadcasted_iota(jnp.int32, (BLOCK, BLOCK), 1)
        rev = jnp.where(c == BLOCK - r, 1.0, 0.0).astype(BF16)
        first = jnp.where((r == 0) & (c == 0), 1.0, 0.0).astype(BF16)
        for a in range(half // BLOCK):
            top = half - a * BLOCK
            z = jnp.dot(rev, mir_ref[0, top - BLOCK:top, :], preferred_element_type=F32)
            z += jnp.dot(first, mir_ref[0, top:top + BLOCK, :], preferred_element_type=F32)
            o_ref[0, a * BLOCK:(a + 1) * BLOCK, :] = z.astype(BF16)


def _time_dft_symmetric(cmat, smat, uv):
    b, _, t, w = uv.shape
    rows = cmat.shape[0]
    assert rows == t // 2 + BLOCK and rows % (2 * SUBLANES) == 0
    tm = rows // 2
    tk = min(DFT_K_TILE, t)
    mat_spec = pl.BlockSpec((tm, tk), lambda i, r, k: (r, k))
    out_spec = pl.BlockSpec((1, tm, w), lambda i, r, k: (i, r, 0))
    lo, mir = pl.pallas_call(
        _dft_half_kernel,
        grid=(b, rows // tm, t // tk),
        in_specs=[
            mat_spec, mat_spec,
            pl.BlockSpec((1, 1, tk, w), lambda i, r, k: (i, 0, k, 0)),
            pl.BlockSpec((1, 1, tk, w), lambda i, r, k: (i, 1, k, 0)),
        ],
        out_specs=[out_spec, out_spec],
        out_shape=[jax.ShapeDtypeStruct((b, rows, w), BF16)] * 2,
        scratch_shapes=[pltpu.VMEM((tm, w), F32), pltpu.VMEM((tm, w), F32)],
        compiler_params=_cparams("parallel", "parallel", "arbitrary"),
        name="time_dft_half",
    )(cmat, smat, uv, uv)
    whole = pl.BlockSpec((1, rows, w), lambda i, n: (i, 0, 0))
    return pl.pallas_call(
        _dft_mirror_kernel,
        grid=(b, 2),
        in_specs=[whole, whole],
        out_specs=pl.BlockSpec((1, t // 2, w), lambda i, n: (i, n, 0)),
        out_shape=jax.ShapeDtypeStruct((b, t, w), BF16),
        compiler_params=_cparams("parallel", "parallel"),
        name="time_dft_mirror",
    )(lo, mir)


def _attn_kernel(sink_ref, q_ref, *refs, window):
    if window:
        kp_ref, kc_ref, kn_ref, vp_ref, vc_ref, vn_ref, kx_ref, vx_ref, o_ref, s_scr, p_scr = refs
        kcat = jnp.concatenate([kx_ref[0], kp_ref[0], kc_ref[0], kn_ref[0]], axis=0)
        vcat = jnp.concatenate([vx_ref[0], vp_ref[0], vc_ref[0], vn_ref[0]], axis=0)
    else:
        kx_ref, vx_ref, o_ref, s_scr, p_scr = refs
        kcat, vcat = kx_ref[0], vx_ref[0]
    n_keys = kcat.shape[0]
    n_q = q_ref.shape[1]

    lane = lax.broadcasted_iota(jnp.int32, kcat.shape, 1)
    low = lane < HEAD_DIM
    zero = jnp.zeros_like(kcat)
    kswap = pltpu.roll(kcat.astype(F32), HEAD_DIM, 1).astype(BF16)
    vswap = pltpu.roll(vcat.astype(F32), HEAD_DIM, 1).astype(BF16)
    k_lo = (jnp.where(low, kcat, zero), jnp.where(low, kswap, zero))
    k_hi = (jnp.where(low, zero, kswap), jnp.where(low, zero, kcat))
    v_lo = (jnp.where(low, vcat, zero), jnp.where(low, vswap, zero))
    v_hi = (jnp.where(low, zero, vswap), jnp.where(low, zero, vcat))

    if window:
        blk = pl.program_id(1)
        last = pl.num_programs(1) - 1
        n_ctx = kx_ref.shape[1]
        row = lax.broadcasted_iota(jnp.int32, (n_q, BLOCK), 0)
        col = lax.broadcasted_iota(jnp.int32, (n_q, BLOCK), 1)
        off_prev = jnp.where(blk > 0, 0, BLOCK)
        off_next = jnp.where(blk < last, 0, BLOCK)
        ok_prev = col >= row + off_prev
        ok_next = col <= row - off_next

    q_per_kv = N_Q_HEADS // N_KV_HEADS
    n_pairs = ATTN_WIDTH // LANES
    for j in range(n_pairs):
        g = (2 * j) // q_per_kv
        qp = q_ref[0, :, j * LANES:(j + 1) * LANES]
        for half, k_ext in enumerate((k_lo[g], k_hi[g])):
            s = lax.dot_general(qp, k_ext, (((1,), (1,)), ((), ())), preferred_element_type=F32)
            head = 2 * j + half
            if window:
                s_scr[head, :, :n_ctx] = s[:, :n_ctx]
                s_scr[head, :, n_ctx:n_ctx + BLOCK] = jnp.where(ok_prev, s[:, n_ctx:n_ctx + BLOCK], NEG_INF)
                s_scr[head, :, n_ctx + BLOCK:n_ctx + 2 * BLOCK] = s[:, n_ctx + BLOCK:n_ctx + 2 * BLOCK]
                s_scr[head, :, n_ctx + 2 * BLOCK:] = jnp.where(ok_next, s[:, n_ctx + 2 * BLOCK:], NEG_INF)
            else:
                s_scr[head] = s
    inv = []
    for head in range(N_Q_HEADS):
        s = s_scr[head]
        sink = sink_ref[head]
        mx = jnp.maximum(jnp.max(s, axis=1, keepdims=True), sink)
        p = jnp.exp(s - mx)
        inv.append(1.0 / (jnp.sum(p, axis=1, keepdims=True) + jnp.exp(sink - mx)))
        p_scr[head] = p.astype(BF16)
    for j in range(n_pairs):
        g = (2 * j) // q_per_kv
        out = jnp.dot(p_scr[2 * j], v_lo[g], preferred_element_type=F32) * inv[2 * j]
        out += jnp.dot(p_scr[2 * j + 1], v_hi[g], preferred_element_type=F32) * inv[2 * j + 1]
        o_ref[0, :, j * LANES:(j + 1) * LANES] = out.astype(BF16)


def _attn_scratch(n_q, n_keys):
    return [pltpu.VMEM((N_Q_HEADS, n_q, n_keys), F32), pltpu.VMEM((N_Q_HEADS, n_q, n_keys), BF16)]


def _window_attention(sink, q, k, v, kx, vx):
    b, length, _ = q.shape
    nb = length // BLOCK
    n_ctx = kx.shape[1]
    kv_spec = lambda f: pl.BlockSpec((1, BLOCK, KV_WIDTH), f)
    prev = lambda i, n: (i, jnp.maximum(n - 1, 0), 0)
    cur = lambda i, n: (i, n, 0)
    nxt = lambda i, n: (i, jnp.minimum(n + 1, nb - 1), 0)
    ctx_spec = pl.BlockSpec((1, n_ctx, KV_WIDTH), lambda i, n: (i, 0, 0))
    return pl.pallas_call(
        functools.partial(_attn_kernel, window=True),
        grid=(b, nb),
        in_specs=[
            pl.BlockSpec(memory_space=pltpu.SMEM),
            pl.BlockSpec((1, BLOCK, ATTN_WIDTH), cur),
            kv_spec(prev), kv_spec(cur), kv_spec(nxt),
            kv_spec(prev), kv_spec(cur), kv_spec(nxt),
            ctx_spec, ctx_spec,
        ],
        out_specs=pl.BlockSpec((1, BLOCK, ATTN_WIDTH), cur),
        out_shape=jax.ShapeDtypeStruct((b, length, ATTN_WIDTH), BF16),
        scratch_shapes=_attn_scratch(BLOCK, n_ctx + 3 * BLOCK),
        compiler_params=_cparams("parallel", "parallel"),
        name="window_attn",
    )(sink, q, k, k, k, v, v, v, kx, vx)


def _context_attention(sink, q, kx, vx):
    b, n_ctx, _ = q.shape
    spec = lambda w: pl.BlockSpec((1, n_ctx, w), lambda i: (i, 0, 0))
    return pl.pallas_call(
        functools.partial(_attn_kernel, window=False),
        grid=(b,),
        in_specs=[pl.BlockSpec(memory_space=pltpu.SMEM), spec(ATTN_WIDTH), spec(KV_WIDTH), spec(KV_WIDTH)],
        out_specs=spec(ATTN_WIDTH),
        out_shape=jax.ShapeDtypeStruct((b, n_ctx, ATTN_WIDTH), BF16),
        scratch_shapes=_attn_scratch(n_ctx, n_ctx),
        compiler_params=_cparams("parallel"),
        name="context_attn",
    )(sink, q, kx, vx)


def _swiglu_residual(h1, m, g2, wg_ref, wu_ref, wd_ref, acc_ref):
    n2 = _norm_mod(h1, g2, m[4:5], m[3:4]).astype(BF16)
    for c in range(wg_ref.shape[1] // FFN_CHUNK):
        cols = slice(c * FFN_CHUNK, (c + 1) * FFN_CHUNK)
        gate = jnp.dot(n2, wg_ref[:, cols], preferred_element_type=F32)
        up = jnp.dot(n2, wu_ref[:, cols], preferred_element_type=F32)
        act = (gate * _sigmoid(gate)) * up
        part = jnp.dot(act.astype(BF16), wd_ref[cols, :], preferred_element_type=F32)
        if c == 0:
            acc_ref[...] = part
        else:
            acc_ref[...] += part
    return h1 + m[5:6] * acc_ref[...]


def _post0_kernel(h_ref, mod_ref, nmod_ref, g2_ref, gn_ref, f_ref, a_ref, wo_ref, wg_ref, wu_ref, wd_ref,
                  o_ref, u_ref, acc_ref):
    m = mod_ref[0]
    mix = jnp.dot(f_ref[0], wo_ref[:FOURIER_WIDTH, :], preferred_element_type=F32)
    mix += jnp.dot(a_ref[0], wo_ref[FOURIER_WIDTH:, :], preferred_element_type=F32)
    h1 = h_ref[0] + m[2:3] * mix
    h2 = _swiglu_residual(h1, m, g2_ref[...], wg_ref, wu_ref, wd_ref, acc_ref)
    o_ref[0] = h2
    nm = nmod_ref[0]
    u_ref[0] = _norm_mod(h2, gn_ref[...], nm[1:2], nm[0:1]).astype(BF16)


def _gelu_tanh(x):
    sqrt_2_over_pi = np.sqrt(2 / np.pi).astype(np.float32)
    cdf = 0.5 * (1.0 + jnp.tanh(sqrt_2_over_pi * (x + 0.044715 * (x ** 3))))
    return x * cdf


def _post1_kernel(h_ref, mod_ref, g1_ref, g2_ref, gf_ref, dskip_ref, yf_ref, yb_ref, wglu_ref,
                  wg_ref, wu_ref, wd_ref, o_ref, acc_ref):
    m = mod_ref[0]
    h = h_ref[0]
    u = _norm_mod(h, g1_ref[...], m[1:2], m[0:1])
    y = dskip_ref[...] * u + yf_ref[0] + yb_ref[0]
    z = jnp.dot(_gelu_tanh(y).astype(BF16), wglu_ref[...], preferred_element_type=F32)
    h1 = h + m[2:3] * (z[:, :D_MODEL] * _sigmoid(z[:, D_MODEL:]))
    h2 = _swiglu_residual(h1, m, g2_ref[...], wg_ref, wu_ref, wd_ref, acc_ref)
    o_ref[0] = (h2 * lax.rsqrt(jnp.mean(h2 * h2, axis=-1, keepdims=True) + RMS_EPS)) * gf_ref[...]


def _resident(shape):
    zeros = (0,) * len(shape)
    return pl.BlockSpec(shape, lambda i, t: zeros, pipeline_mode=pl.Buffered(1))


def _post_call(kernel, name, h, mods, row_inputs, vec_inputs, weights, rows, out_dtypes):
    b, length, d = h.shape
    row_map = lambda i, t: (i, t, 0)
    mod_spec = lambda m: pl.BlockSpec((1, N_MOD, d), lambda i, t: (i if m.shape[0] > 1 else 0, 0, 0))
    in_specs = [pl.BlockSpec((1, rows, d), row_map)]
    in_specs += [mod_spec(m) for m in mods]
    in_specs += [pl.BlockSpec((1, d), lambda i, t: (0, 0)) for _ in vec_inputs]
    in_specs += [pl.BlockSpec((1, rows, a.shape[-1]), row_map) for a in row_inputs]
    in_specs += [_resident(w.shape) for w in weights]
    return pl.pallas_call(
        kernel,
        grid=(b, length // rows),
        in_specs=in_specs,
        out_specs=[pl.BlockSpec((1, rows, d), row_map) for _ in out_dtypes],
        out_shape=[jax.ShapeDtypeStruct((b, length, d), dt) for dt in out_dtypes],
        scratch_shapes=[pltpu.VMEM((rows, d), F32)],
        compiler_params=_cparams("parallel", "parallel"),
        name=name,
    )(h, *mods, *vec_inputs, *row_inputs, *weights)


def _disc_kernel(are_ref, aim_ref, ldt_ref, bre_ref, bim_ref, oar_ref, oai_ref, obr_ref, obi_ref):
    a_re, a_im = are_ref[0], aim_ref[0]
    dt = jnp.exp(ldt_ref[0])
    mag = jnp.exp(a_re * dt)
    abar_re, abar_im = mag * jnp.cos(a_im * dt), mag * jnp.sin(a_im * dt)
    nr, ni = abar_re - 1.0, abar_im
    den = a_re * a_re + a_im * a_im
    f_re = (nr * a_re + ni * a_im) / den
    f_im = (ni * a_re - nr * a_im) / den
    oar_ref[0] = abar_re
    oai_ref[0] = abar_im
    obr_ref[0] = f_re[None] * bre_ref[0] - f_im[None] * bim_ref[0]
    obi_ref[0] = f_re[None] * bim_ref[0] + f_im[None] * bre_ref[0]


def _discretize(a_re, a_im, log_dt, b_re, b_im):
    nd = a_re.shape[0]
    rows = SSM_WIDTH // LANES
    flat = lambda a: a.reshape(nd, rows, LANES)
    ldt = jnp.broadcast_to(log_dt[:, :, None], (nd, SSM_GROUPS, SSM_STATE))
    chan = lambda a: a.reshape(nd, SSM_WIDTH, SSM_GROUP_W).transpose(0, 2, 1).reshape(nd, SSM_GROUP_W, rows, LANES)
    aspec = pl.BlockSpec((1, rows, LANES), lambda i: (i, 0, 0))
    bspec = pl.BlockSpec((1, SSM_GROUP_W, rows, LANES), lambda i: (i, 0, 0, 0))
    ashape = jax.ShapeDtypeStruct((nd, rows, LANES), F32)
    bshape = jax.ShapeDtypeStruct((nd, SSM_GROUP_W, rows, LANES), F32)
    return pl.pallas_call(
        _disc_kernel,
        grid=(nd,),
        in_specs=[aspec, aspec, aspec, bspec, bspec],
        out_specs=[aspec, aspec, bspec, bspec],
        out_shape=[ashape, ashape, bshape, bshape],
        compiler_params=_cparams("parallel"),
        name="s5_discretize",
    )(flat(a_re), flat(a_im), flat(ldt), chan(b_re), chan(b_im))


def _s5_kernel(uc_ref, ul_ref, wd_ref, wr_ref, are_ref, aim_ref, y_ref,
               xa_re, xa_im, xb_re, xb_im, hre, him, *, reverse):
    step_idx = pl.program_id(1)
    steps = S5_CHUNK
    cols_per_slab = S5_SLAB_STATES // LANES
    seg = steps // S5_SLABS

    @pl.when(step_idx == 0)
    def _():
        hre[...] = jnp.zeros_like(hre)
        him[...] = jnp.zeros_like(him)

    def plane_of(col):
        return col // SUBLANES, col % SUBLANES

    def drive_slab(u, kk, xre, xim):
        bu = jnp.dot(u(kk * LANES, (kk + 1) * LANES), wd_ref[kk], preferred_element_type=F32)
        for c in range(cols_per_slab):
            pln, sub = plane_of(kk * cols_per_slab + c)
            xre[pln, pl.ds(sub, steps, stride=SUBLANES), :] = bu[:, c * LANES:(c + 1) * LANES]
            xim[pln, pl.ds(sub, steps, stride=SUBLANES), :] = (
                bu[:, S5_SLAB_STATES + c * LANES:S5_SLAB_STATES + (c + 1) * LANES])

    def read_slab(kk, xre, xim):
        parts = []
        for src in (xre, xim):
            for c in range(cols_per_slab):
                pln, sub = plane_of(kk * cols_per_slab + c)
                parts.append(src[pln, pl.ds(sub, steps, stride=SUBLANES), :].astype(BF16))
        return jnp.dot(jnp.concatenate(parts, axis=1), wr_ref[kk], preferred_element_type=F32)

    a_r = [are_ref[p] for p in range(S5_PLANES)]
    a_i = [aim_ref[p] for p in range(S5_PLANES)]

    def one_step(t, h, xre, xim):
        h_r, h_i = h
        r0 = t * SUBLANES if isinstance(t, int) else pl.multiple_of(t * SUBLANES, SUBLANES)
        new_r, new_i = [], []
        for p in range(S5_PLANES):
            n_r = a_r[p] * h_r[p] - a_i[p] * h_i[p] + xre[p, pl.ds(r0, SUBLANES), :]
            n_i = a_r[p] * h_i[p] + a_i[p] * h_r[p] + xim[p, pl.ds(r0, SUBLANES), :]
            xre[p, pl.ds(r0, SUBLANES), :] = n_r
            xim[p, pl.ds(r0, SUBLANES), :] = n_i
            new_r.append(n_r)
            new_i.append(n_i)
        return tuple(new_r), tuple(new_i)

    def time_of(i):
        return (steps - 1 - i) if reverse else i

    def scan_segment(k, h, xre, xim):
        for i in range(k * seg, (k + 1) * seg):
            h = one_step(time_of(i), h, xre, xim)
        return h

    def load_state():
        return (tuple(hre[p] for p in range(S5_PLANES)), tuple(him[p] for p in range(S5_PLANES)))

    def store_state(h):
        for p in range(S5_PLANES):
            hre[p] = h[0][p]
            him[p] = h[1][p]

    @pl.when(step_idx == 0)
    def _():
        u = lambda lo, hi: uc_ref[0, :, lo:hi]
        for kk in range(S5_SLABS):
            drive_slab(u, kk, xa_re, xa_im)
        h = lax.fori_loop(0, steps, lambda i, h: one_step(time_of(i), h, xa_re, xa_im), load_state(), unroll=8)
        store_state(h)

    @pl.when(step_idx > 0)
    def _():
        first, second = slice(0, steps), slice(steps, 2 * steps)
        rows_a, rows_b = (second, first) if reverse else (first, second)
        u_a = lambda lo, hi: ul_ref[0, rows_a, lo:hi]
        u_b = lambda lo, hi: ul_ref[0, rows_b, lo:hi]
        for kk in range(S5_SLABS):
            drive_slab(u_a, kk, xa_re, xa_im)
        h = load_state()
        for kk in range(S5_SLABS):
            drive_slab(u_b, kk, xb_re, xb_im)
            h = scan_segment(kk, h, xa_re, xa_im)
        for kk in range(S5_SLABS):
            h = scan_segment(kk, h, xb_re, xb_im)
            y_ref[0, rows_a, kk * LANES:(kk + 1) * LANES] = read_slab(kk, xa_re, xa_im)
        store_state(h)
        for kk in range(S5_SLABS):
            y_ref[0, rows_b, kk * LANES:(kk + 1) * LANES] = read_slab(kk, xb_re, xb_im)


def _s5_direction(u_ctx, u_lat, w_drive, w_read, a_re, a_im, *, reverse):
    b, length, d = u_lat.shape
    n_ctx = u_ctx.shape[1]
    pair = 2 * S5_CHUNK
    assert n_ctx == S5_CHUNK and length % pair == 0
    n_pairs = length // pair
    if reverse:
        lat_map = lambda i, n: (i, jnp.minimum(n_pairs - n, n_pairs - 1), 0)
    else:
        lat_map = lambda i, n: (i, jnp.maximum(n - 1, 0), 0)
    whole = lambda shape: pl.BlockSpec(shape, lambda i, n: (0,) * len(shape))
    plane_buf = pltpu.VMEM((S5_PLANES, S5_CHUNK * SUBLANES, LANES), F32)
    state_buf = pltpu.VMEM((S5_PLANES, SUBLANES, LANES), F32)
    return pl.pallas_call(
        functools.partial(_s5_kernel, reverse=reverse),
        grid=(b, n_pairs + 1),
        in_specs=[
            pl.BlockSpec((1, n_ctx, d), lambda i, n: (i, 0, 0)),
            pl.BlockSpec((1, pair, d), lat_map),
            whole(w_drive.shape), whole(w_read.shape), whole(a_re.shape), whole(a_im.shape),
        ],
        out_specs=pl.BlockSpec((1, pair, d), lat_map),
        out_shape=jax.ShapeDtypeStruct((b, length, d), F32),
        scratch_shapes=[plane_buf, plane_buf, plane_buf, plane_buf, state_buf, state_buf],
        compiler_params=_cparams("parallel", "arbitrary"),
        name="s5_scan_bwd" if reverse else "s5_scan_fwd",
    )(u_ctx, u_lat, w_drive, w_read, a_re, a_im)


def _rope_tables(length):
    rows = length // GRID_W
    t_row = jnp.repeat(jnp.arange(rows, dtype=F32), GRID_W)
    t_col = jnp.tile(jnp.arange(GRID_W, dtype=F32), rows)
    n_freq = HEAD_DIM // 4
    inv_freq = ROPE_BASE ** (-jnp.arange(n_freq, dtype=F32) / n_freq)
    ang = jnp.concatenate([t_row[:, None] * inv_freq, t_col[:, None] * inv_freq], axis=-1)
    cos, sin = jnp.cos(ang), jnp.sin(ang)
    reps = LANES // HEAD_DIM
    return (jnp.tile(jnp.concatenate([cos, cos], axis=-1), (1, reps)),
            jnp.tile(jnp.concatenate([-sin, sin], axis=-1), (1, reps)))


def _half_split_perm():
    perm = np.arange(MIX_IN_WIDTH)
    head = np.concatenate([np.arange(0, HEAD_DIM, 2), np.arange(1, HEAD_DIM, 2)])
    for start in range(FOURIER_WIDTH, KV_START + KV_WIDTH, HEAD_DIM):
        perm[start:start + HEAD_DIM] = start + head
    return perm


def _channel_dft():
    c = np.arange(FOURIER_GROUP_W)
    ang = 2.0 * np.pi * np.outer(c, c) / FOURIER_GROUP_W
    eye = np.eye(FOURIER_GROUPS)
    scale = FOURIER_GROUP_W ** -0.5
    table = np.concatenate([np.kron(eye, np.cos(ang) * scale), np.kron(eye, np.sin(ang) * scale)], axis=1)
    return jnp.asarray(table, F32).astype(BF16)


def _time_dft_matrices(t, rows=None):
    r1 = 1 << ((t.bit_length() - 1 + 1) // 2)
    r2 = t // r1
    assert r1 * r2 == t and r2 % SUBLANES == 0
    rows = t if rows is None else rows
    assert rows % r2 == 0
    n = jnp.arange(t, dtype=jnp.int32)[None, :]
    unit = 2.0 * math.pi / t
    ang_a = (((jnp.arange(rows // r2, dtype=jnp.int32) * r2)[:, None] * n) % t).astype(F32) * unit
    ang_b = ((jnp.arange(r2, dtype=jnp.int32)[:, None] * n) % t).astype(F32) * unit
    scale = t ** -0.5
    ca, sa = jnp.cos(ang_a)[:, None, :], jnp.sin(ang_a)[:, None, :]
    cb, sb = (jnp.cos(ang_b) * scale)[None, :, :], (jnp.sin(ang_b) * scale)[None, :, :]
    cmat = (ca * cb - sa * sb).reshape(rows, t).astype(BF16)
    smat = (-(sa * cb + ca * sb)).reshape(rows, t).astype(BF16)
    return cmat, smat


def _ffn_weights(w_gate, w_up, w_down):
    return w_gate.astype(BF16), w_up.astype(BF16), w_down.astype(BF16)


def _s5_weights(bbar_re, bbar_im, c_re, c_im):
    nd = bbar_re.shape[0]
    eye = jnp.eye(S5_SLAB_GROUPS, dtype=F32)

    def drive(bb):
        bb = bb.reshape(nd, SSM_GROUP_W, S5_SLABS, S5_SLAB_GROUPS, SSM_STATE)
        w = jnp.einsum("dhkgp,gG->dkghGp", bb, eye)
        return w.reshape(nd, S5_SLABS, LANES, S5_SLAB_STATES)

    def read(cc):
        cc = cc.astype(F32).reshape(nd, S5_SLABS, S5_SLAB_GROUPS, SSM_GROUP_W, SSM_STATE)
        w = jnp.einsum("dkghp,gG->dkgpGh", cc, eye)
        return w.reshape(nd, S5_SLABS, S5_SLAB_STATES, LANES)

    w_drive = jnp.concatenate([drive(bbar_re), drive(bbar_im)], axis=-1).astype(BF16)
    w_read = jnp.concatenate([read(c_re), -read(c_im)], axis=2).astype(BF16)
    return w_drive, w_read


def kernel(x, c, ctx, c_ctx, mod_w, mod_b, norm_g, ffn_w_gate, ffn_w_up, ffn_w_down, mix_w_in, mix_w_out,
           attn_sink, ssm_a_re, ssm_a_im, ssm_log_dt, ssm_b_re, ssm_b_im, ssm_c_re, ssm_c_im, ssm_d,
           ssm_glu_w, final_g):
    bn, length, d = x.shape
    n_ctx = ctx.shape[1]
    depth = mod_w.shape[0]
    assert depth == 2 and d == D_MODEL

    cond = jnp.zeros((SUBLANES, d), F32).at[:bn].set(c).at[bn].set(c_ctx)
    mods = _modulations(cond, mod_w, mod_b).reshape(depth, SUBLANES, N_MOD, d)
    mods_lat, mods_ctx = mods[:, :bn], mods[:, bn:bn + 1]
    vec = lambda a: a.reshape(1, d)

    w_in = mix_w_in[0][:, _half_split_perm()].astype(BF16)
    w_out = mix_w_out[0].astype(BF16)
    bcs = _channel_dft()
    cos_t, sin_t = _rope_tables(length)
    ffn0 = _ffn_weights(ffn_w_gate[0], ffn_w_up[0], ffn_w_down[0])
    g_mix, g_ffn = vec(norm_g[0, 0]), vec(norm_g[0, 1])

    uv, q, k, v = _pre0(x, mods_lat[0], g_mix, w_in, bcs, cos_t, sin_t, rope=True, rows=TOKEN_TILE)
    uvc, qc, kc, vc = _pre0(ctx, mods_ctx[0], g_mix, w_in, bcs, cos_t, sin_t, rope=False, rows=n_ctx)
    four = _time_dft_symmetric(*_time_dft_matrices(length, length // 2 + BLOCK), uv)
    four_c = _time_dft(*_time_dft_matrices(n_ctx), uvc)
    att = _window_attention(attn_sink[0], q, k, v, kc, vc)
    att_c = _context_attention(attn_sink[0], qc, kc, vc)
    g_mix1 = vec(norm_g[1, 0])
    h, u_lat = _post_call(_post0_kernel, "mix_out_ffn", x, [mods_lat[0], mods_lat[1]], [four, att],
                          [g_ffn, g_mix1], [w_out, *ffn0], TOKEN_TILE, [F32, BF16])
    _, u_ctx = _post_call(_post0_kernel, "mix_out_ffn_ctx", ctx, [mods_ctx[0], mods_ctx[1]], [four_c, att_c],
                          [g_ffn, g_mix1], [w_out, *ffn0], n_ctx, [F32, BF16])

    g_mix, g_ffn = g_mix1, vec(norm_g[1, 1])
    abar_re, abar_im, bbar_re, bbar_im = _discretize(
        ssm_a_re[0], ssm_a_im[0], ssm_log_dt[0], ssm_b_re[0], ssm_b_im[0])
    w_drive, w_read = _s5_weights(bbar_re, bbar_im, ssm_c_re[0], ssm_c_im[0])
    planes = lambda a: a.reshape(2, S5_PLANES, SUBLANES, LANES)
    abar_re, abar_im = planes(abar_re), planes(abar_im)
    y_dir = [_s5_direction(u_ctx, u_lat, w_drive[i], w_read[i], abar_re[i], abar_im[i], reverse=bool(i))
             for i in range(2)]
    ffn1 = _ffn_weights(ffn_w_gate[1], ffn_w_up[1], ffn_w_down[1])
    return _post_call(_post1_kernel, "s5_glu_ffn_norm", h, [mods_lat[1]], y_dir,
                      [g_mix, g_ffn, vec(final_g), vec(ssm_d[0])],
                      [ssm_glu_w[0].astype(BF16), *ffn1], TOKEN_TILE, [F32])[0]
```

```python
import functools
import math

import numpy as np
import jax
import jax.numpy as jnp
from jax import lax
from jax.experimental import pallas as pl
from jax.experimental.pallas import tpu as pltpu

F32 = jnp.float32
BF16 = jnp.bfloat16

D_MODEL = 1024
GRID_W = 64
N_MOD = 6
N_Q_HEADS = 8
N_KV_HEADS = 2
HEAD_DIM = 64
ATTN_WIDTH = N_Q_HEADS * HEAD_DIM
KV_WIDTH = N_KV_HEADS * HEAD_DIM
FOURIER_GROUPS = 8
FOURIER_GROUP_W = 64
FOURIER_WIDTH = FOURIER_GROUPS * FOURIER_GROUP_W
KV_START = FOURIER_WIDTH + ATTN_WIDTH
MIX_IN_WIDTH = KV_START + 2 * KV_WIDTH
WINDOW = 128
BLOCK = 128
ATTN_SCALE = HEAD_DIM ** -0.5
ROPE_BASE = 10000.0
NEG_INF = -1e30
SSM_GROUP_W = 16
SSM_GROUPS = D_MODEL // SSM_GROUP_W
SSM_STATE = 64
SSM_WIDTH = SSM_GROUPS * SSM_STATE
RMS_EPS = 1e-6

LANES = 128
SUBLANES = 8
VMEM_LIMIT_BYTES = 56 * 1024 * 1024

TOKEN_TILE = 512
FFN_CHUNK = 256
MOD_TILE = 1536
CAST_ROW_TILE = 256
DFT_ROW_TILE = 1024
DFT_K_TILE = 2048
S5_CHUNK = 256
S5_SLAB_GROUPS = LANES // SSM_GROUP_W
S5_SLABS = SSM_GROUPS // S5_SLAB_GROUPS
S5_SLAB_STATES = S5_SLAB_GROUPS * SSM_STATE
S5_COLS = SSM_WIDTH // LANES
S5_PLANES = S5_COLS // SUBLANES


def _cparams(*sem):
    return pltpu.CompilerParams(dimension_semantics=sem, vmem_limit_bytes=VMEM_LIMIT_BYTES)


def _sigmoid(x):
    return 1.0 / (1.0 + jnp.exp(-x))


def _norm_mod(x, g, scale, shift):
    y = x * lax.rsqrt(jnp.mean(x * x, axis=-1, keepdims=True) + RMS_EPS)
    return (y * g) * (1.0 + scale) + shift


def _split_bf16(x):
    hi = x.astype(BF16)
    lo = (x - hi.astype(F32)).astype(BF16)
    return hi, lo


def _mod_kernel(cond_ref, w_ref, b_ref, o_ref):
    x = cond_ref[...]
    s_hi, s_lo = _split_bf16(x * _sigmoid(x))
    w_hi, w_lo = _split_bf16(w_ref[0])
    acc = jnp.dot(s_hi, w_hi, preferred_element_type=F32)
    acc += jnp.dot(s_lo, w_hi, preferred_element_type=F32)
    acc += jnp.dot(s_hi, w_lo, preferred_element_type=F32)
    o_ref[0] = acc + b_ref[0]


def _modulations(cond, mod_w, mod_b):
    depth, d, n = mod_w.shape
    rows = cond.shape[0]
    return pl.pallas_call(
        _mod_kernel,
        grid=(depth, n // MOD_TILE),
        in_specs=[
            pl.BlockSpec((rows, d), lambda l, j: (0, 0)),
            pl.BlockSpec((1, d, MOD_TILE), lambda l, j: (l, 0, j)),
            pl.BlockSpec((1, 1, MOD_TILE), lambda l, j: (l, 0, j)),
        ],
        out_specs=pl.BlockSpec((1, rows, MOD_TILE), lambda l, j: (l, 0, j)),
        out_shape=jax.ShapeDtypeStruct((depth, rows, n), F32),
        compiler_params=_cparams("parallel", "parallel"),
        name="adaln_mod",
    )(cond, mod_w, mod_b.reshape(depth, 1, n))


def _pre0_kernel(x_ref, mod_ref, g_ref, w_ref, bcs_ref, cos_ref, sin_ref,
                 uv_ref, q_ref, k_ref, v_ref, *, rope):
    m = mod_ref[0]
    n = _norm_mod(x_ref[0], g_ref[...], m[1:2], m[0:1])
    p = jnp.dot(n.astype(BF16), w_ref[...], preferred_element_type=F32)
    uv = jnp.dot(p[:, :FOURIER_WIDTH].astype(BF16), bcs_ref[...], preferred_element_type=F32)
    uv_ref[0, 0] = uv[:, :FOURIER_WIDTH].astype(BF16)
    uv_ref[0, 1] = uv[:, FOURIER_WIDTH:].astype(BF16)

    def rot(x):
        if not rope:
            return x
        lane = lax.broadcasted_iota(jnp.int32, x.shape, 1)
        first = (lane & (HEAD_DIM // 2)) == 0
        partner = jnp.where(first, pltpu.roll(x, LANES - HEAD_DIM // 2, 1),
                            pltpu.roll(x, HEAD_DIM // 2, 1))
        return x * cos_ref[...] + partner * sin_ref[...]

    for j in range(ATTN_WIDTH // LANES):
        lo = FOURIER_WIDTH + j * LANES
        q_ref[0, :, j * LANES:(j + 1) * LANES] = (rot(p[:, lo:lo + LANES]) * ATTN_SCALE).astype(BF16)
    k_ref[0] = rot(p[:, KV_START:KV_START + KV_WIDTH]).astype(BF16)
    v_ref[0] = p[:, KV_START + KV_WIDTH:].astype(BF16)


def _pre0(x, mods, g, w_in, bcs, cos_t, sin_t, *, rope, rows):
    b, length, d = x.shape
    grid = (b, length // rows)
    row_map = lambda i, t: (i, t, 0)
    const2 = lambda i, t: (0, 0)
    return pl.pallas_call(
        functools.partial(_pre0_kernel, rope=rope),
        grid=grid,
        in_specs=[
            pl.BlockSpec((1, rows, d), row_map),
            pl.BlockSpec((1, N_MOD, d), lambda i, t: (i if mods.shape[0] > 1 else 0, 0, 0)),
            pl.BlockSpec((1, d), const2),
            pl.BlockSpec((d, MIX_IN_WIDTH), const2),
            pl.BlockSpec((FOURIER_WIDTH, 2 * FOURIER_WIDTH), const2),
            pl.BlockSpec((rows, LANES), lambda i, t: (t, 0)),
            pl.BlockSpec((rows, LANES), lambda i, t: (t, 0)),
        ],
        out_specs=[
            pl.BlockSpec((1, 2, rows, FOURIER_WIDTH), lambda i, t: (i, 0, t, 0)),
            pl.BlockSpec((1, rows, ATTN_WIDTH), row_map),
            pl.BlockSpec((1, rows, KV_WIDTH), row_map),
            pl.BlockSpec((1, rows, KV_WIDTH), row_map),
        ],
        out_shape=[
            jax.ShapeDtypeStruct((b, 2, length, FOURIER_WIDTH), BF16),
            jax.ShapeDtypeStruct((b, length, ATTN_WIDTH), BF16),
            jax.ShapeDtypeStruct((b, length, KV_WIDTH), BF16),
            jax.ShapeDtypeStruct((b, length, KV_WIDTH), BF16),
        ],
        compiler_params=_cparams("parallel", "parallel"),
        name="mix_in_rope" if rope else "mix_in_ctx",
    )(x, mods, g, w_in, bcs, cos_t, sin_t)


def _dft_kernel(c_ref, s_ref, u_ref, v_ref, o_ref, acc_ref):
    k = pl.program_id(2)

    @pl.when(k == 0)
    def _():
        acc_ref[...] = jnp.zeros_like(acc_ref)

    acc_ref[...] += (jnp.dot(c_ref[...], u_ref[0, 0], preferred_element_type=F32)
                     + jnp.dot(s_ref[...], v_ref[0, 0], preferred_element_type=F32))

    @pl.when(k == pl.num_programs(2) - 1)
    def _():
        o_ref[0] = acc_ref[...].astype(BF16)


def _time_dft(cmat, smat, uv):
    b, _, t, w = uv.shape
    tm = min(DFT_ROW_TILE, t)
    tk = min(DFT_K_TILE, t)
    mat_spec = pl.BlockSpec((tm, tk), lambda i, r, k: (r, k))
    return pl.pallas_call(
        _dft_kernel,
        grid=(b, t // tm, t // tk),
        in_specs=[
            mat_spec, mat_spec,
            pl.BlockSpec((1, 1, tk, w), lambda i, r, k: (i, 0, k, 0)),
            pl.BlockSpec((1, 1, tk, w), lambda i, r, k: (i, 1, k, 0)),
        ],
        out_specs=pl.BlockSpec((1, tm, w), lambda i, r, k: (i, r, 0)),
        out_shape=jax.ShapeDtypeStruct((b, t, w), BF16),
        scratch_shapes=[pltpu.VMEM((tm, w), F32)],
        compiler_params=_cparams("parallel", "parallel", "arbitrary"),
        name="time_dft",
    )(cmat, smat, uv, uv)


def _dft_half_kernel(c_ref, s_ref, u_ref, v_ref, lo_ref, mir_ref, p_acc, q_acc):
    k = pl.program_id(2)

    @pl.when(k == 0)
    def _():
        p_acc[...] = jnp.zeros_like(p_acc)
        q_acc[...] = jnp.zeros_like(q_acc)

    p_acc[...] += jnp.dot(c_ref[...], u_ref[0, 0], preferred_element_type=F32)
    q_acc[...] += jnp.dot(s_ref[...], v_ref[0, 0], preferred_element_type=F32)

    @pl.when(k == pl.num_programs(2) - 1)
    def _():
        lo_ref[0] = (p_acc[...] + q_acc[...]).astype(BF16)
        mir_ref[0] = (p_acc[...] - q_acc[...]).astype(BF16)


def _dft_mirror_kernel(lo_ref, mir_ref, o_ref):
    i = pl.program_id(1)
    half = o_ref.shape[1]

    @pl.when(i == 0)
    def _():
        o_ref[0] = lo_ref[0, :half, :]

    @pl.when(i == 1)
    def _():
        r = lax.broadcasted_iota(jnp.int32, (BLOCK, BLOCK), 0)
        c = lax.broadcasted_iota(jnp.int32, (BLOCK, BLOCK), 1)
        rev = jnp.where(c == BLOCK - r, 1.0, 0.0).astype(BF16)
        first = jnp.where((r == 0) & (c == 0), 1.0, 0.0).astype(BF16)
        for a in range(half // BLOCK):
            top = half - a * BLOCK
            z = jnp.dot(rev, mir_ref[0, top - BLOCK:top, :], preferred_element_type=F32)
            z += jnp.dot(first, mir_ref[0, top:top + BLOCK, :], preferred_element_type=F32)
            o_ref[0, a * BLOCK:(a + 1) * BLOCK, :] = z.astype(BF16)


def _time_dft_symmetric(cmat, smat, uv):
    b, _, t, w = uv.shape
    rows = cmat.shape[0]
    assert rows == t // 2 + BLOCK and rows % (2 * SUBLANES) == 0
    tm = rows // 2
    tk = min(DFT_K_TILE, t)
    mat_spec = pl.BlockSpec((tm, tk), lambda i, r, k: (r, k))
    out_spec = pl.BlockSpec((1, tm, w), lambda i, r, k: (i, r, 0))
    lo, mir = pl.pallas_call(
        _dft_half_kernel,
        grid=(b, rows // tm, t // tk),
        in_specs=[
            mat_spec, mat_spec,
            pl.BlockSpec((1, 1, tk, w), lambda i, r, k: (i, 0, k, 0)),
            pl.BlockSpec((1, 1, tk, w), lambda i, r, k: (i, 1, k, 0)),
        ],
        out_specs=[out_spec, out_spec],
        out_shape=[jax.ShapeDtypeStruct((b, rows, w), BF16)] * 2,
        scratch_shapes=[pltpu.VMEM((tm, w), F32), pltpu.VMEM((tm, w), F32)],
        compiler_params=_cparams("parallel", "parallel", "arbitrary"),
        name="time_dft_half",
    )(cmat, smat, uv, uv)
    whole = pl.BlockSpec((1, rows, w), lambda i, n: (i, 0, 0))
    return pl.pallas_call(
        _dft_mirror_kernel,
        grid=(b, 2),
        in_specs=[whole, whole],
        out_specs=pl.BlockSpec((1, t // 2, w), lambda i, n: (i, n, 0)),
        out_shape=jax.ShapeDtypeStruct((b, t, w), BF16),
        compiler_params=_cparams("parallel", "parallel"),
        name="time_dft_mirror",
    )(lo, mir)


def _attn_kernel(sink_ref, q_ref, *refs, window):
    if window:
        kp_ref, kc_ref, kn_ref, vp_ref, vc_ref, vn_ref, kx_ref, vx_ref, o_ref, s_scr, p_scr = refs
        kcat = jnp.concatenate([kx_ref[0], kp_ref[0], kc_ref[0], kn_ref[0]], axis=0)
        vcat = jnp.concatenate([vx_ref[0], vp_ref[0], vc_ref[0], vn_ref[0]], axis=0)
    else:
        kx_ref, vx_ref, o_ref, s_scr, p_scr = refs
        kcat, vcat = kx_ref[0], vx_ref[0]
    n_keys = kcat.shape[0]
    n_q = q_ref.shape[1]

    lane = lax.broadcasted_iota(jnp.int32, kcat.shape, 1)
    low = lane < HEAD_DIM
    zero = jnp.zeros_like(kcat)
    kswap = pltpu.roll(kcat.astype(F32), HEAD_DIM, 1).astype(BF16)
    vswap = pltpu.roll(vcat.astype(F32), HEAD_DIM, 1).astype(BF16)
    k_lo = (jnp.where(low, kcat, zero), jnp.where(low, kswap, zero))
    k_hi = (jnp.where(low, zero, kswap), jnp.where(low, zero, kcat))
    v_lo = (jnp.where(low, vcat, zero), jnp.where(low, vswap, zero))
    v_hi = (jnp.where(low, zero, vswap), jnp.where(low, zero, vcat))

    if window:
        blk = pl.program_id(1)
        last = pl.num_programs(1) - 1
        n_ctx = kx_ref.shape[1]
        row = lax.broadcasted_iota(jnp.int32, (n_q, BLOCK), 0)
        col = lax.broadcasted_iota(jnp.int32, (n_q, BLOCK), 1)
        off_prev = jnp.where(blk > 0, 0, BLOCK)
        off_next = jnp.where(blk < last, 0, BLOCK)
        ok_prev = col >= row + off_prev
        ok_next = col <= row - off_next

    q_per_kv = N_Q_HEADS // N_KV_HEADS
    n_pairs = ATTN_WIDTH // LANES
    for j in range(n_pairs):
        g = (2 * j) // q_per_kv
        qp = q_ref[0, :, j * LANES:(j + 1) * LANES]
        for half, k_ext in enumerate((k_lo[g], k_hi[g])):
            s = lax.dot_general(qp, k_ext, (((1,), (1,)), ((), ())), preferred_element_type=F32)
            head = 2 * j + half
            if window:
                s_scr[head, :, :n_ctx] = s[:, :n_ctx]
                s_scr[head, :, n_ctx:n_ctx + BLOCK] = jnp.where(ok_prev, s[:, n_ctx:n_ctx + BLOCK], NEG_INF)
                s_scr[head, :, n_ctx + BLOCK:n_ctx + 2 * BLOCK] = s[:, n_ctx + BLOCK:n_ctx + 2 * BLOCK]
                s_scr[head, :, n_ctx + 2 * BLOCK:] = jnp.where(ok_next, s[:, n_ctx + 2 * BLOCK:], NEG_INF)
            else:
                s_scr[head] = s
    inv = []
    for head in range(N_Q_HEADS):
        s = s_scr[head]
        sink = sink_ref[head]
        mx = jnp.maximum(jnp.max(s, axis=1, keepdims=True), sink)
        p = jnp.exp(s - mx)
        inv.append(1.0 / (jnp.sum(p, axis=1, keepdims=True) + jnp.exp(sink - mx)))
        p_scr[head] = p.astype(BF16)
    for j in range(n_pairs):
        g = (2 * j) // q_per_kv
        out = jnp.dot(p_scr[2 * j], v_lo[g], preferred_element_type=F32) * inv[2 * j]
        out += jnp.dot(p_scr[2 * j + 1], v_hi[g], preferred_element_type=F32) * inv[2 * j + 1]
        o_ref[0, :, j * LANES:(j + 1) * LANES] = out.astype(BF16)


def _attn_scratch(n_q, n_keys):
    return [pltpu.VMEM((N_Q_HEADS, n_q, n_keys), F32), pltpu.VMEM((N_Q_HEADS, n_q, n_keys), BF16)]


def _window_attention(sink, q, k, v, kx, vx):
    b, length, _ = q.shape
    nb = length // BLOCK
    n_ctx = kx.shape[1]
    kv_spec = lambda f: pl.BlockSpec((1, BLOCK, KV_WIDTH), f)
    prev = lambda i, n: (i, jnp.maximum(n - 1, 0), 0)
    cur = lambda i, n: (i, n, 0)
    nxt = lambda i, n: (i, jnp.minimum(n + 1, nb - 1), 0)
    ctx_spec = pl.BlockSpec((1, n_ctx, KV_WIDTH), lambda i, n: (i, 0, 0))
    return pl.pallas_call(
        functools.partial(_attn_kernel, window=True),
        grid=(b, nb),
        in_specs=[
            pl.BlockSpec(memory_space=pltpu.SMEM),
            pl.BlockSpec((1, BLOCK, ATTN_WIDTH), cur),
            kv_spec(prev), kv_spec(cur), kv_spec(nxt),
            kv_spec(prev), kv_spec(cur), kv_spec(nxt),
            ctx_spec, ctx_spec,
        ],
        out_specs=pl.BlockSpec((1, BLOCK, ATTN_WIDTH), cur),
        out_shape=jax.ShapeDtypeStruct((b, length, ATTN_WIDTH), BF16),
        scratch_shapes=_attn_scratch(BLOCK, n_ctx + 3 * BLOCK),
        compiler_params=_cparams("parallel", "parallel"),
        name="window_attn",
    )(sink, q, k, k, k, v, v, v, kx, vx)


def _context_attention(sink, q, kx, vx):
    b, n_ctx, _ = q.shape
    spec = lambda w: pl.BlockSpec((1, n_ctx, w), lambda i: (i, 0, 0))
    return pl.pallas_call(
        functools.partial(_attn_kernel, window=False),
        grid=(b,),
        in_specs=[pl.BlockSpec(memory_space=pltpu.SMEM), spec(ATTN_WIDTH), spec(KV_WIDTH), spec(KV_WIDTH)],
        out_specs=spec(ATTN_WIDTH),
        out_shape=jax.ShapeDtypeStruct((b, n_ctx, ATTN_WIDTH), BF16),
        scratch_shapes=_attn_scratch(n_ctx, n_ctx),
        compiler_params=_cparams("parallel"),
        name="context_attn",
    )(sink, q, kx, vx)


def _swiglu_residual(h1, m, g2, wg_ref, wu_ref, wd_ref, acc_ref):
    n2 = _norm_mod(h1, g2, m[4:5], m[3:4]).astype(BF16)
    for c in range(wg_ref.shape[1] // FFN_CHUNK):
        cols = slice(c * FFN_CHUNK, (c + 1) * FFN_CHUNK)
        gate = jnp.dot(n2, wg_ref[:, cols], preferred_element_type=F32)
        up = jnp.dot(n2, wu_ref[:, cols], preferred_element_type=F32)
        act = (gate * _sigmoid(gate)) * up
        part = jnp.dot(act.astype(BF16), wd_ref[cols, :], preferred_element_type=F32)
        if c == 0:
            acc_ref[...] = part
        else:
            acc_ref[...] += part
    return h1 + m[5:6] * acc_ref[...]


def _post0_kernel(h_ref, mod_ref, nmod_ref, g2_ref, gn_ref, f_ref, a_ref, wo_ref, wg_ref, wu_ref, wd_ref,
                  o_ref, u_ref, acc_ref):
    m = mod_ref[0]
    mix = jnp.dot(f_ref[0], wo_ref[:FOURIER_WIDTH, :], preferred_element_type=F32)
    mix += jnp.dot(a_ref[0], wo_ref[FOURIER_WIDTH:, :], preferred_element_type=F32)
    h1 = h_ref[0] + m[2:3] * mix
    h2 = _swiglu_residual(h1, m, g2_ref[...], wg_ref, wu_ref, wd_ref, acc_ref)
    o_ref[0] = h2
    nm = nmod_ref[0]
    u_ref[0] = _norm_mod(h2, gn_ref[...], nm[1:2], nm[0:1]).astype(BF16)


def _gelu_tanh(x):
    sqrt_2_over_pi = np.sqrt(2 / np.pi).astype(np.float32)
    cdf = 0.5 * (1.0 + jnp.tanh(sqrt_2_over_pi * (x + 0.044715 * (x ** 3))))
    return x * cdf


def _post1_kernel(h_ref, mod_ref, g1_ref, g2_ref, gf_ref, dskip_ref, yf_ref, yb_ref, wglu_ref,
                  wg_ref, wu_ref, wd_ref, o_ref, acc_ref):
    m = mod_ref[0]
    h = h_ref[0]
    u = _norm_mod(h, g1_ref[...], m[1:2], m[0:1])
    y = dskip_ref[...] * u + yf_ref[0] + yb_ref[0]
    z = jnp.dot(_gelu_tanh(y).astype(BF16), wglu_ref[...], preferred_element_type=F32)
    h1 = h + m[2:3] * (z[:, :D_MODEL] * _sigmoid(z[:, D_MODEL:]))
    h2 = _swiglu_residual(h1, m, g2_ref[...], wg_ref, wu_ref, wd_ref, acc_ref)
    o_ref[0] = (h2 * lax.rsqrt(jnp.mean(h2 * h2, axis=-1, keepdims=True) + RMS_EPS)) * gf_ref[...]


def _resident(shape):
    zeros = (0,) * len(shape)
    return pl.BlockSpec(shape, lambda i, t: zeros, pipeline_mode=pl.Buffered(1))


def _post_call(kernel, name, h, mods, row_inputs, vec_inputs, weights, rows, out_dtypes):
    b, length, d = h.shape
    row_map = lambda i, t: (i, t, 0)
    mod_spec = lambda m: pl.BlockSpec((1, N_MOD, d), lambda i, t: (i if m.shape[0] > 1 else 0, 0, 0))
    in_specs = [pl.BlockSpec((1, rows, d), row_map)]
    in_specs += [mod_spec(m) for m in mods]
    in_specs += [pl.BlockSpec((1, d), lambda i, t: (0, 0)) for _ in vec_inputs]
    in_specs += [pl.BlockSpec((1, rows, a.shape[-1]), row_map) for a in row_inputs]
    in_specs += [_resident(w.shape) for w in weights]
    return pl.pallas_call(
        kernel,
        grid=(b, length // rows),
        in_specs=in_specs,
        out_specs=[pl.BlockSpec((1, rows, d), row_map) for _ in out_dtypes],
        out_shape=[jax.ShapeDtypeStruct((b, length, d), dt) for dt in out_dtypes],
        scratch_shapes=[pltpu.VMEM((rows, d), F32)],
        compiler_params=_cparams("parallel", "parallel"),
        name=name,
    )(h, *mods, *vec_inputs, *row_inputs, *weights)


def _disc_kernel(are_ref, aim_ref, ldt_ref, bre_ref, bim_ref, oar_ref, oai_ref, obr_ref, obi_ref):
    a_re, a_im = are_ref[0], aim_ref[0]
    dt = jnp.exp(ldt_ref[0])
    mag = jnp.exp(a_re * dt)
    abar_re, abar_im = mag * jnp.cos(a_im * dt), mag * jnp.sin(a_im * dt)
    nr, ni = abar_re - 1.0, abar_im
    den = a_re * a_re + a_im * a_im
    f_re = (nr * a_re + ni * a_im) / den
    f_im = (ni * a_re - nr * a_im) / den
    oar_ref[0] = abar_re
    oai_ref[0] = abar_im
    obr_ref[0] = f_re[None] * bre_ref[0] - f_im[None] * bim_ref[0]
    obi_ref[0] = f_re[None] * bim_ref[0] + f_im[None] * bre_ref[0]


def _discretize(a_re, a_im, log_dt, b_re, b_im):
    nd = a_re.shape[0]
    rows = SSM_WIDTH // LANES
    flat = lambda a: a.reshape(nd, rows, LANES)
    ldt = jnp.broadcast_to(log_dt[:, :, None], (nd, SSM_GROUPS, SSM_STATE))
    chan = lambda a: a.reshape(nd, SSM_WIDTH, SSM_GROUP_W).transpose(0, 2, 1).reshape(nd, SSM_GROUP_W, rows, LANES)
    aspec = pl.BlockSpec((1, rows, LANES), lambda i: (i, 0, 0))
    bspec = pl.BlockSpec((1, SSM_GROUP_W, rows, LANES), lambda i: (i, 0, 0, 0))
    ashape = jax.ShapeDtypeStruct((nd, rows, LANES), F32)
    bshape = jax.ShapeDtypeStruct((nd, SSM_GROUP_W, rows, LANES), F32)
    return pl.pallas_call(
        _disc_kernel,
        grid=(nd,),
        in_specs=[aspec, aspec, aspec, bspec, bspec],
        out_specs=[aspec, aspec, bspec, bspec],
        out_shape=[ashape, ashape, bshape, bshape],
        compiler_params=_cparams("parallel"),
        name="s5_discretize",
    )(flat(a_re), flat(a_im), flat(ldt), chan(b_re), chan(b_im))


def _s5_kernel(uc_ref, ul_ref, wd_ref, wr_ref, are_ref, aim_ref, y_ref,
               xa_re, xa_im, xb_re, xb_im, hre, him, *, reverse):
    step_idx = pl.program_id(1)
    steps = S5_CHUNK
    cols_per_slab = S5_SLAB_STATES // LANES
    seg = steps // S5_SLABS

    @pl.when(step_idx == 0)
    def _():
        hre[...] = jnp.zeros_like(hre)
        him[...] = jnp.zeros_like(him)

    def plane_of(col):
        return col // SUBLANES, col % SUBLANES

    def drive_slab(u, kk, xre, xim):
        bu = jnp.dot(u(kk * LANES, (kk + 1) * LANES), wd_ref[kk], preferred_element_type=F32)
        for c in range(cols_per_slab):
            pln, sub = plane_of(kk * cols_per_slab + c)
            xre[pln, pl.ds(sub, steps, stride=SUBLANES), :] = bu[:, c * LANES:(c + 1) * LANES]
            xim[pln, pl.ds(sub, steps, stride=SUBLANES), :] = (
                bu[:, S5_SLAB_STATES + c * LANES:S5_SLAB_STATES + (c + 1) * LANES])

    def read_slab(kk, xre, xim):
        parts = []
        for src in (xre, xim):
            for c in range(cols_per_slab):
                pln, sub = plane_of(kk * cols_per_slab + c)
                parts.append(src[pln, pl.ds(sub, steps, stride=SUBLANES), :].astype(BF16))
        return jnp.dot(jnp.concatenate(parts, axis=1), wr_ref[kk], preferred_element_type=F32)

    a_r = [are_ref[p] for p in range(S5_PLANES)]
    a_i = [aim_ref[p] for p in range(S5_PLANES)]

    def one_step(t, h, xre, xim):
        h_r, h_i = h
        r0 = t * SUBLANES if isinstance(t, int) else pl.multiple_of(t * SUBLANES, SUBLANES)
        new_r, new_i = [], []
        for p in range(S5_PLANES):
            n_r = a_r[p] * h_r[p] - a_i[p] * h_i[p] + xre[p, pl.ds(r0, SUBLANES), :]
            n_i = a_r[p] * h_i[p] + a_i[p] * h_r[p] + xim[p, pl.ds(r0, SUBLANES), :]
            xre[p, pl.ds(r0, SUBLANES), :] = n_r
            xim[p, pl.ds(r0, SUBLANES), :] = n_i
            new_r.append(n_r)
            new_i.append(n_i)
        return tuple(new_r), tuple(new_i)

    def time_of(i):
        return (steps - 1 - i) if reverse else i

    def scan_segment(k, h, xre, xim):
        for i in range(k * seg, (k + 1) * seg):
            h = one_step(time_of(i), h, xre, xim)
        return h

    def load_state():
        return (tuple(hre[p] for p in range(S5_PLANES)), tuple(him[p] for p in range(S5_PLANES)))

    def store_state(h):
        for p in range(S5_PLANES):
            hre[p] = h[0][p]
            him[p] = h[1][p]

    @pl.when(step_idx == 0)
    def _():
        u = lambda lo, hi: uc_ref[0, :, lo:hi]
        for kk in range(S5_SLABS):
            drive_slab(u, kk, xa_re, xa_im)
        h = lax.fori_loop(0, steps, lambda i, h: one_step(time_of(i), h, xa_re, xa_im), load_state(), unroll=8)
        store_state(h)

    @pl.when(step_idx > 0)
    def _():
        first, second = slice(0, steps), slice(steps, 2 * steps)
        rows_a, rows_b = (second, first) if reverse else (first, second)
        u_a = lambda lo, hi: ul_ref[0, rows_a, lo:hi]
        u_b = lambda lo, hi: ul_ref[0, rows_b, lo:hi]
        for kk in range(S5_SLABS):
            drive_slab(u_a, kk, xa_re, xa_im)
        h = load_state()
        for kk in range(S5_SLABS):
            drive_slab(u_b, kk, xb_re, xb_im)
            h = scan_segment(kk, h, xa_re, xa_im)
        for kk in range(S5_SLABS):
            h = scan_segment(kk, h, xb_re, xb_im)
            y_ref[0, rows_a, kk * LANES:(kk + 1) * LANES] = read_slab(kk, xa_re, xa_im)
        store_state(h)
        for kk in range(S5_SLABS):
            y_ref[0, rows_b, kk * LANES:(kk + 1) * LANES] = read_slab(kk, xb_re, xb_im)


def _s5_direction(u_ctx, u_lat, w_drive, w_read, a_re, a_im, *, reverse):
    b, length, d = u_lat.shape
    n_ctx = u_ctx.shape[1]
    pair = 2 * S5_CHUNK
    assert n_ctx == S5_CHUNK and length % pair == 0
    n_pairs = length // pair
    if reverse:
        lat_map = lambda i, n: (i, jnp.minimum(n_pairs - n, n_pairs - 1), 0)
    else:
        lat_map = lambda i, n: (i, jnp.maximum(n - 1, 0), 0)
    whole = lambda shape: pl.BlockSpec(shape, lambda i, n: (0,) * len(shape))
    plane_buf = pltpu.VMEM((S5_PLANES, S5_CHUNK * SUBLANES, LANES), F32)
    state_buf = pltpu.VMEM((S5_PLANES, SUBLANES, LANES), F32)
    return pl.pallas_call(
        functools.partial(_s5_kernel, reverse=reverse),
        grid=(b, n_pairs + 1),
        in_specs=[
            pl.BlockSpec((1, n_ctx, d), lambda i, n: (i, 0, 0)),
            pl.BlockSpec((1, pair, d), lat_map),
            whole(w_drive.shape), whole(w_read.shape), whole(a_re.shape), whole(a_im.shape),
        ],
        out_specs=pl.BlockSpec((1, pair, d), lat_map),
        out_shape=jax.ShapeDtypeStruct((b, length, d), F32),
        scratch_shapes=[plane_buf, plane_buf, plane_buf, plane_buf, state_buf, state_buf],
        compiler_params=_cparams("parallel", "arbitrary"),
        name="s5_scan_bwd" if reverse else "s5_scan_fwd",
    )(u_ctx, u_lat, w_drive, w_read, a_re, a_im)


def _rope_tables(length):
    rows = length // GRID_W
    t_row = jnp.repeat(jnp.arange(rows, dtype=F32), GRID_W)
    t_col = jnp.tile(jnp.arange(GRID_W, dtype=F32), rows)
    n_freq = HEAD_DIM // 4
    inv_freq = ROPE_BASE ** (-jnp.arange(n_freq, dtype=F32) / n_freq)
    ang = jnp.concatenate([t_row[:, None] * inv_freq, t_col[:, None] * inv_freq], axis=-1)
    cos, sin = jnp.cos(ang), jnp.sin(ang)
    reps = LANES // HEAD_DIM
    return (jnp.tile(jnp.concatenate([cos, cos], axis=-1), (1, reps)),
            jnp.tile(jnp.concatenate([-sin, sin], axis=-1), (1, reps)))


def _half_split_perm():
    perm = np.arange(MIX_IN_WIDTH)
    head = np.concatenate([np.arange(0, HEAD_DIM, 2), np.arange(1, HEAD_DIM, 2)])
    for start in range(FOURIER_WIDTH, KV_START + KV_WIDTH, HEAD_DIM):
        perm[start:start + HEAD_DIM] = start + head
    return perm


def _channel_dft():
    c = np.arange(FOURIER_GROUP_W)
    ang = 2.0 * np.pi * np.outer(c, c) / FOURIER_GROUP_W
    eye = np.eye(FOURIER_GROUPS)
    scale = FOURIER_GROUP_W ** -0.5
    table = np.concatenate([np.kron(eye, np.cos(ang) * scale), np.kron(eye, np.sin(ang) * scale)], axis=1)
    return jnp.asarray(table, F32).astype(BF16)


def _time_dft_matrices(t, rows=None):
    r1 = 1 << ((t.bit_length() - 1 + 1) // 2)
    r2 = t // r1
    assert r1 * r2 == t and r2 % SUBLANES == 0
    rows = t if rows is None else rows
    assert rows % r2 == 0
    n = jnp.arange(t, dtype=jnp.int32)[None, :]
    unit = 2.0 * math.pi / t
    ang_a = (((jnp.arange(rows // r2, dtype=jnp.int32) * r2)[:, None] * n) % t).astype(F32) * unit
    ang_b = ((jnp.arange(r2, dtype=jnp.int32)[:, None] * n) % t).astype(F32) * unit
    scale = t ** -0.5
    ca, sa = jnp.cos(ang_a)[:, None, :], jnp.sin(ang_a)[:, None, :]
    cb, sb = (jnp.cos(ang_b) * scale)[None, :, :], (jnp.sin(ang_b) * scale)[None, :, :]
    cmat = (ca * cb - sa * sb).reshape(rows, t).astype(BF16)
    smat = (-(sa * cb + ca * sb)).reshape(rows, t).astype(BF16)
    return cmat, smat


def _cast_kernel(x_ref, o_ref):
    o_ref[...] = x_ref[0].astype(BF16)


def _layer_bf16(w, layer):
    _, rows, cols = w.shape
    tr = CAST_ROW_TILE
    return pl.pallas_call(
        _cast_kernel,
        grid=(rows // tr,),
        in_specs=[pl.BlockSpec((1, tr, cols), lambda i: (layer, i, 0))],
        out_specs=pl.BlockSpec((tr, cols), lambda i: (i, 0)),
        out_shape=jax.ShapeDtypeStruct((rows, cols), BF16),
        compiler_params=_cparams("parallel"),
        name="cast_bf16",
    )(w)


def _ffn_weights(w_gate, w_up, w_down, layer):
    return _layer_bf16(w_gate, layer), _layer_bf16(w_up, layer), _layer_bf16(w_down, layer)


def _s5_weights(bbar_re, bbar_im, c_re, c_im):
    nd = bbar_re.shape[0]
    eye = jnp.eye(S5_SLAB_GROUPS, dtype=F32)

    def drive(bb):
        bb = bb.reshape(nd, SSM_GROUP_W, S5_SLABS, S5_SLAB_GROUPS, SSM_STATE)
        w = jnp.einsum("dhkgp,gG->dkghGp", bb, eye)
        return w.reshape(nd, S5_SLABS, LANES, S5_SLAB_STATES)

    def read(cc):
        cc = cc.astype(F32).reshape(nd, S5_SLABS, S5_SLAB_GROUPS, SSM_GROUP_W, SSM_STATE)
        w = jnp.einsum("dkghp,gG->dkgpGh", cc, eye)
        return w.reshape(nd, S5_SLABS, S5_SLAB_STATES, LANES)

    w_drive = jnp.concatenate([drive(bbar_re), drive(bbar_im)], axis=-1).astype(BF16)
    w_read = jnp.concatenate([read(c_re), -read(c_im)], axis=2).astype(BF16)
    return w_drive, w_read


def kernel(x, c, ctx, c_ctx, mod_w, mod_b, norm_g, ffn_w_gate, ffn_w_up, ffn_w_down, mix_w_in, mix_w_out,
           attn_sink, ssm_a_re, ssm_a_im, ssm_log_dt, ssm_b_re, ssm_b_im, ssm_c_re, ssm_c_im, ssm_d,
           ssm_glu_w, final_g):
    bn, length, d = x.shape
    n_ctx = ctx.shape[1]
    depth = mod_w.shape[0]
    assert depth == 2 and d == D_MODEL

    cond = jnp.zeros((SUBLANES, d), F32).at[:bn].set(c).at[bn].set(c_ctx)
    mods = _modulations(cond, mod_w, mod_b).reshape(depth, SUBLANES, N_MOD, d)
    mods_lat, mods_ctx = mods[:, :bn], mods[:, bn:bn + 1]
    vec = lambda a: a.reshape(1, d)

    w_in = mix_w_in[0][:, _half_split_perm()].astype(BF16)
    w_out = _layer_bf16(mix_w_out, 0)
    bcs = _channel_dft()
    cos_t, sin_t = _rope_tables(length)
    ffn0 = _ffn_weights(ffn_w_gate, ffn_w_up, ffn_w_down, 0)
    g_mix, g_ffn = vec(norm_g[0, 0]), vec(norm_g[0, 1])

    uv, q, k, v = _pre0(x, mods_lat[0], g_mix, w_in, bcs, cos_t, sin_t, rope=True, rows=TOKEN_TILE)
    uvc, qc, kc, vc = _pre0(ctx, mods_ctx[0], g_mix, w_in, bcs, cos_t, sin_t, rope=False, rows=n_ctx)
    four = _time_dft_symmetric(*_time_dft_matrices(length, length // 2 + BLOCK), uv)
    four_c = _time_dft(*_time_dft_matrices(n_ctx), uvc)
    att = _window_attention(attn_sink[0], q, k, v, kc, vc)
    att_c = _context_attention(attn_sink[0], qc, kc, vc)
    g_mix1 = vec(norm_g[1, 0])
    h, u_lat = _post_call(_post0_kernel, "mix_out_ffn", x, [mods_lat[0], mods_lat[1]], [four, att],
                          [g_ffn, g_mix1], [w_out, *ffn0], TOKEN_TILE, [F32, BF16])
    _, u_ctx = _post_call(_post0_kernel, "mix_out_ffn_ctx", ctx, [mods_ctx[0], mods_ctx[1]], [four_c, att_c],
                          [g_ffn, g_mix1], [w_out, *ffn0], n_ctx, [F32, BF16])

    g_mix, g_ffn = g_mix1, vec(norm_g[1, 1])
    abar_re, abar_im, bbar_re, bbar_im = _discretize(
        ssm_a_re[0], ssm_a_im[0], ssm_log_dt[0], ssm_b_re[0], ssm_b_im[0])
    w_drive, w_read = _s5_weights(bbar_re, bbar_im, ssm_c_re[0], ssm_c_im[0])
    planes = lambda a: a.reshape(2, S5_PLANES, SUBLANES, LANES)
    abar_re, abar_im = planes(abar_re), planes(abar_im)
    y_dir = [_s5_direction(u_ctx, u_lat, w_drive[i], w_read[i], abar_re[i], abar_im[i], reverse=bool(i))
             for i in range(2)]
    ffn1 = _ffn_weights(ffn_w_gate, ffn_w_up, ffn_w_down, 1)
    return _post_call(_post1_kernel, "s5_glu_ffn_norm", h, [mods_lat[1]], y_dir,
                      [g_mix, g_ffn, vec(final_g), vec(ssm_d[0])],
                      [_layer_bf16(ssm_glu_w, 0), *ffn1], TOKEN_TILE, [F32])[0]
```

```python
import functools
import math

import numpy as np
import jax
import jax.numpy as jnp
from jax import lax
from jax.experimental import pallas as pl
from jax.experimental.pallas import tpu as pltpu

F32 = jnp.float32
BF16 = jnp.bfloat16

D_MODEL = 1024
GRID_W = 64
N_MOD = 6
N_Q_HEADS = 8
N_KV_HEADS = 2
HEAD_DIM = 64
ATTN_WIDTH = N_Q_HEADS * HEAD_DIM
KV_WIDTH = N_KV_HEADS * HEAD_DIM
FOURIER_GROUPS = 8
FOURIER_GROUP_W = 64
FOURIER_WIDTH = FOURIER_GROUPS * FOURIER_GROUP_W
KV_START = FOURIER_WIDTH + ATTN_WIDTH
MIX_IN_WIDTH = KV_START + 2 * KV_WIDTH
WINDOW = 128
BLOCK = 128
ATTN_SCALE = HEAD_DIM ** -0.5
ROPE_BASE = 10000.0
NEG_INF = -1e30
SSM_GROUP_W = 16
SSM_GROUPS = D_MODEL // SSM_GROUP_W
SSM_STATE = 64
SSM_WIDTH = SSM_GROUPS * SSM_STATE
RMS_EPS = 1e-6

LANES = 128
SUBLANES = 8
VMEM_LIMIT_BYTES = 56 * 1024 * 1024

TOKEN_TILE = 512
FFN_CHUNK = 256
MOD_TILE = 1536
DFT_ROW_TILE = 1024
DFT_K_TILE = 2048
S5_CHUNK = 256
S5_SLAB_GROUPS = LANES // SSM_GROUP_W
S5_SLABS = SSM_GROUPS // S5_SLAB_GROUPS
S5_SLAB_STATES = S5_SLAB_GROUPS * SSM_STATE
S5_COLS = SSM_WIDTH // LANES
S5_PLANES = S5_COLS // SUBLANES


def _cparams(*sem):
    return pltpu.CompilerParams(dimension_semantics=sem, vmem_limit_bytes=VMEM_LIMIT_BYTES)


def _sigmoid(x):
    return 1.0 / (1.0 + jnp.exp(-x))


def _norm_mod(x, g, scale, shift):
    y = x * lax.rsqrt(jnp.mean(x * x, axis=-1, keepdims=True) + RMS_EPS)
    return (y * g) * (1.0 + scale) + shift


def _split_bf16(x):
    hi = x.astype(BF16)
    lo = (x - hi.astype(F32)).astype(BF16)
    return hi, lo


def _mod_kernel(cond_ref, w_ref, b_ref, o_ref):
    x = cond_ref[...]
    s_hi, s_lo = _split_bf16(x * _sigmoid(x))
    w_hi, w_lo = _split_bf16(w_ref[0])
    acc = jnp.dot(s_hi, w_hi, preferred_element_type=F32)
    acc += jnp.dot(s_lo, w_hi, preferred_element_type=F32)
    acc += jnp.dot(s_hi, w_lo, preferred_element_type=F32)
    o_ref[0] = acc + b_ref[0]


def _modulations(cond, mod_w, mod_b):
    depth, d, n = mod_w.shape
    rows = cond.shape[0]
    return pl.pallas_call(
        _mod_kernel,
        grid=(depth, n // MOD_TILE),
        in_specs=[
            pl.BlockSpec((rows, d), lambda l, j: (0, 0)),
            pl.BlockSpec((1, d, MOD_TILE), lambda l, j: (l, 0, j)),
            pl.BlockSpec((1, 1, MOD_TILE), lambda l, j: (l, 0, j)),
        ],
        out_specs=pl.BlockSpec((1, rows, MOD_TILE), lambda l, j: (l, 0, j)),
        out_shape=jax.ShapeDtypeStruct((depth, rows, n), F32),
        compiler_params=_cparams("parallel", "parallel"),
        name="adaln_mod",
    )(cond, mod_w, mod_b.reshape(depth, 1, n))


def _pre0_kernel(x_ref, mod_ref, g_ref, w_ref, bcs_ref, cos_ref, sin_ref,
                 uv_ref, q_ref, k_ref, v_ref, *, rope):
    m = mod_ref[0]
    n = _norm_mod(x_ref[0], g_ref[...], m[1:2], m[0:1])
    p = jnp.dot(n.astype(BF16), w_ref[...], preferred_element_type=F32)
    uv = jnp.dot(p[:, :FOURIER_WIDTH].astype(BF16), bcs_ref[...], preferred_element_type=F32)
    uv_ref[0, 0] = uv[:, :FOURIER_WIDTH].astype(BF16)
    uv_ref[0, 1] = uv[:, FOURIER_WIDTH:].astype(BF16)

    def rot(x):
        if not rope:
            return x
        lane = lax.broadcasted_iota(jnp.int32, x.shape, 1)
        first = (lane & (HEAD_DIM // 2)) == 0
        partner = jnp.where(first, pltpu.roll(x, LANES - HEAD_DIM // 2, 1),
                            pltpu.roll(x, HEAD_DIM // 2, 1))
        return x * cos_ref[...] + partner * sin_ref[...]

    for j in range(ATTN_WIDTH // LANES):
        lo = FOURIER_WIDTH + j * LANES
        q_ref[0, :, j * LANES:(j + 1) * LANES] = (rot(p[:, lo:lo + LANES]) * ATTN_SCALE).astype(BF16)
    k_ref[0] = rot(p[:, KV_START:KV_START + KV_WIDTH]).astype(BF16)
    v_ref[0] = p[:, KV_START + KV_WIDTH:].astype(BF16)


def _pre0(x, mods, g, w_in, bcs, cos_t, sin_t, *, rope, rows):
    b, length, d = x.shape
    grid = (b, length // rows)
    row_map = lambda i, t: (i, t, 0)
    const2 = lambda i, t: (0, 0)
    return pl.pallas_call(
        functools.partial(_pre0_kernel, rope=rope),
        grid=grid,
        in_specs=[
            pl.BlockSpec((1, rows, d), row_map),
            pl.BlockSpec((1, N_MOD, d), lambda i, t: (i if mods.shape[0] > 1 else 0, 0, 0)),
            pl.BlockSpec((1, d), const2),
            pl.BlockSpec((d, MIX_IN_WIDTH), const2),
            pl.BlockSpec((FOURIER_WIDTH, 2 * FOURIER_WIDTH), const2),
            pl.BlockSpec((rows, LANES), lambda i, t: (t, 0)),
            pl.BlockSpec((rows, LANES), lambda i, t: (t, 0)),
        ],
        out_specs=[
            pl.BlockSpec((1, 2, rows, FOURIER_WIDTH), lambda i, t: (i, 0, t, 0)),
            pl.BlockSpec((1, rows, ATTN_WIDTH), row_map),
            pl.BlockSpec((1, rows, KV_WIDTH), row_map),
            pl.BlockSpec((1, rows, KV_WIDTH), row_map),
        ],
        out_shape=[
            jax.ShapeDtypeStruct((b, 2, length, FOURIER_WIDTH), BF16),
            jax.ShapeDtypeStruct((b, length, ATTN_WIDTH), BF16),
            jax.ShapeDtypeStruct((b, length, KV_WIDTH), BF16),
            jax.ShapeDtypeStruct((b, length, KV_WIDTH), BF16),
        ],
        compiler_params=_cparams("parallel", "parallel"),
        name="mix_in_rope" if rope else "mix_in_ctx",
    )(x, mods, g, w_in, bcs, cos_t, sin_t)


def _dft_kernel(c_ref, s_ref, u_ref, v_ref, o_ref, acc_ref):
    k = pl.program_id(2)

    @pl.when(k == 0)
    def _():
        acc_ref[...] = jnp.zeros_like(acc_ref)

    acc_ref[...] += (jnp.dot(c_ref[...], u_ref[0, 0], preferred_element_type=F32)
                     + jnp.dot(s_ref[...], v_ref[0, 0], preferred_element_type=F32))

    @pl.when(k == pl.num_programs(2) - 1)
    def _():
        o_ref[0] = acc_ref[...].astype(BF16)


def _time_dft(cmat, smat, uv):
    b, _, t, w = uv.shape
    tm = min(DFT_ROW_TILE, t)
    tk = min(DFT_K_TILE, t)
    mat_spec = pl.BlockSpec((tm, tk), lambda i, r, k: (r, k))
    return pl.pallas_call(
        _dft_kernel,
        grid=(b, t // tm, t // tk),
        in_specs=[
            mat_spec, mat_spec,
            pl.BlockSpec((1, 1, tk, w), lambda i, r, k: (i, 0, k, 0)),
            pl.BlockSpec((1, 1, tk, w), lambda i, r, k: (i, 1, k, 0)),
        ],
        out_specs=pl.BlockSpec((1, tm, w), lambda i, r, k: (i, r, 0)),
        out_shape=jax.ShapeDtypeStruct((b, t, w), BF16),
        scratch_shapes=[pltpu.VMEM((tm, w), F32)],
        compiler_params=_cparams("parallel", "parallel", "arbitrary"),
        name="time_dft",
    )(cmat, smat, uv, uv)


def _dft_half_kernel(c_ref, s_ref, u_ref, v_ref, lo_ref, mir_ref, p_acc, q_acc):
    k = pl.program_id(2)

    @pl.when(k == 0)
    def _():
        p_acc[...] = jnp.zeros_like(p_acc)
        q_acc[...] = jnp.zeros_like(q_acc)

    p_acc[...] += jnp.dot(c_ref[...], u_ref[0, 0], preferred_element_type=F32)
    q_acc[...] += jnp.dot(s_ref[...], v_ref[0, 0], preferred_element_type=F32)

    @pl.when(k == pl.num_programs(2) - 1)
    def _():
        lo_ref[0] = (p_acc[...] + q_acc[...]).astype(BF16)
        mir_ref[0] = (p_acc[...] - q_acc[...]).astype(BF16)


def _dft_mirror_kernel(lo_ref, mir_ref, o_ref):
    i = pl.program_id(1)
    half = o_ref.shape[1]

    @pl.when(i == 0)
    def _():
        o_ref[0] = lo_ref[0, :half, :]

    @pl.when(i == 1)
    def _():
        r = lax.broadcasted_iota(jnp.int32, (BLOCK, BLOCK), 0)
        c = lax.broadcasted_iota(jnp.int32, (BLOCK, BLOCK), 1)
        rev = jnp.where(c == BLOCK - r, 1.0, 0.0).astype(BF16)
        first = jnp.where((r == 0) & (c == 0), 1.0, 0.0).astype(BF16)
        for a in range(half // BLOCK):
            top = half - a * BLOCK
            z = jnp.dot(rev, mir_ref[0, top - BLOCK:top, :], preferred_element_type=F32)
            z += jnp.dot(first, mir_ref[0, top:top + BLOCK, :], preferred_element_type=F32)
            o_ref[0, a * BLOCK:(a + 1) * BLOCK, :] = z.astype(BF16)


def _time_dft_symmetric(cmat, smat, uv):
    b, _, t, w = uv.shape
    rows = cmat.shape[0]
    assert rows == t // 2 + BLOCK and rows % (2 * SUBLANES) == 0
    tm = rows // 2
    tk = min(DFT_K_TILE, t)
    mat_spec = pl.BlockSpec((tm, tk), lambda i, r, k: (r, k))
    out_spec = pl.BlockSpec((1, tm, w), lambda i, r, k: (i, r, 0))
    lo, mir = pl.pallas_call(
        _dft_half_kernel,
        grid=(b, rows // tm, t // tk),
        in_specs=[
            mat_spec, mat_spec,
            pl.BlockSpec((1, 1, tk, w), lambda i, r, k: (i, 0, k, 0)),
            pl.BlockSpec((1, 1, tk, w), lambda i, r, k: (i, 1, k, 0)),
        ],
        out_specs=[out_spec, out_spec],
        out_shape=[jax.ShapeDtypeStruct((b, rows, w), BF16)] * 2,
        scratch_shapes=[pltpu.VMEM((tm, w), F32), pltpu.VMEM((tm, w), F32)],
        compiler_params=_cparams("parallel", "parallel", "arbitrary"),
        name="time_dft_half",
    )(cmat, smat, uv, uv)
    whole = pl.BlockSpec((1, rows, w), lambda i, n: (i, 0, 0))
    return pl.pallas_call(
        _dft_mirror_kernel,
        grid=(b, 2),
        in_specs=[whole, whole],
        out_specs=pl.BlockSpec((1, t // 2, w), lambda i, n: (i, n, 0)),
        out_shape=jax.ShapeDtypeStruct((b, t, w), BF16),
        compiler_params=_cparams("parallel", "parallel"),
        name="time_dft_mirror",
    )(lo, mir)


def _attn_kernel(sink_ref, q_ref, *refs, window):
    if window:
        kp_ref, km_ref, kn_ref, vp_ref, vm_ref, vn_ref, kx_ref, vx_ref, o_ref, s_scr, p_scr = refs
        k_pieces = [kx_ref[0], kp_ref[0], km_ref[0, :BLOCK, :], km_ref[0, BLOCK:, :], kn_ref[0]]
        v_pieces = [vx_ref[0], vp_ref[0], vm_ref[0, :BLOCK, :], vm_ref[0, BLOCK:, :], vn_ref[0]]
        subs = ((0, 1, 2, 3), (0, 2, 3, 4))
    else:
        kx_ref, vx_ref, o_ref, s_scr, p_scr = refs
        k_pieces, v_pieces = [kx_ref[0]], [vx_ref[0]]
        subs = ((0,),)
    n_sub = len(subs)
    n_q = q_ref.shape[1] // n_sub

    def split_heads(x):
        low = lax.broadcasted_iota(jnp.int32, x.shape, 1) < HEAD_DIM
        zero = jnp.zeros_like(x)
        swap = pltpu.roll(x.astype(F32), HEAD_DIM, 1).astype(BF16)
        return ((jnp.where(low, x, zero), jnp.where(low, swap, zero)),
                (jnp.where(low, zero, swap), jnp.where(low, zero, x)))

    k_ext = [split_heads(x) for x in k_pieces]
    v_ext = [split_heads(x) for x in v_pieces]
    cache = {}

    def operand(name, ext, sub, half, g):
        key = (name, sub, half, g)
        if key not in cache:
            parts = [ext[i][half][g] for i in subs[sub]]
            cache[key] = parts[0] if len(parts) == 1 else jnp.concatenate(parts, axis=0)
        return cache[key]

    if window:
        step = pl.program_id(1)
        last = pl.num_programs(1) - 1
        n_ctx = kx_ref.shape[1]
        row = lax.broadcasted_iota(jnp.int32, (n_q, BLOCK), 0)
        col = lax.broadcasted_iota(jnp.int32, (n_q, BLOCK), 1)
        ok_prev = (col >= row + jnp.where(step > 0, 0, BLOCK), col >= row)
        ok_next = (col <= row, col <= row - jnp.where(step < last, 0, BLOCK))

    q_per_kv = N_Q_HEADS // N_KV_HEADS
    n_pairs = ATTN_WIDTH // LANES
    for sub in range(n_sub):
        rows = slice(sub * n_q, (sub + 1) * n_q)
        for j in range(n_pairs):
            g = (2 * j) // q_per_kv
            qp = q_ref[0, rows, j * LANES:(j + 1) * LANES]
            for half in range(2):
                s = lax.dot_general(qp, operand("k", k_ext, sub, half, g), (((1,), (1,)), ((), ())),
                                    preferred_element_type=F32)
                item = sub * N_Q_HEADS + 2 * j + half
                if window:
                    s_scr[item, :, :n_ctx] = s[:, :n_ctx]
                    s_scr[item, :, n_ctx:n_ctx + BLOCK] = jnp.where(
                        ok_prev[sub], s[:, n_ctx:n_ctx + BLOCK], NEG_INF)
                    s_scr[item, :, n_ctx + BLOCK:n_ctx + 2 * BLOCK] = s[:, n_ctx + BLOCK:n_ctx + 2 * BLOCK]
                    s_scr[item, :, n_ctx + 2 * BLOCK:] = jnp.where(
                        ok_next[sub], s[:, n_ctx + 2 * BLOCK:], NEG_INF)
                else:
                    s_scr[item] = s
    inv = []
    n_keys = s_scr.shape[2]
    col_blocks = [slice(c, c + LANES) for c in range(0, n_keys, LANES)]
    for item in range(n_sub * N_Q_HEADS):
        sink = sink_ref[item % N_Q_HEADS]
        mx = s_scr[item, :, col_blocks[0]]
        for cb in col_blocks[1:]:
            mx = jnp.maximum(mx, s_scr[item, :, cb])
        mx = jnp.maximum(jnp.max(mx, axis=1, keepdims=True), sink)
        total = None
        for cb in col_blocks:
            p = jnp.exp(s_scr[item, :, cb] - mx)
            total = p if total is None else total + p
            p_scr[item, :, cb] = p.astype(BF16)
        inv.append(1.0 / (jnp.sum(total, axis=1, keepdims=True) + jnp.exp(sink - mx)))
    for sub in range(n_sub):
        rows = slice(sub * n_q, (sub + 1) * n_q)
        for j in range(n_pairs):
            g = (2 * j) // q_per_kv
            a = sub * N_Q_HEADS + 2 * j
            out = jnp.dot(p_scr[a], operand("v", v_ext, sub, 0, g), preferred_element_type=F32) * inv[a]
            out += jnp.dot(p_scr[a + 1], operand("v", v_ext, sub, 1, g), preferred_element_type=F32) * inv[a + 1]
            o_ref[0, rows, j * LANES:(j + 1) * LANES] = out.astype(BF16)


def _attn_scratch(n_items, n_q, n_keys):
    return [pltpu.VMEM((n_items, n_q, n_keys), F32), pltpu.VMEM((n_items, n_q, n_keys), BF16)]


def _window_attention(sink, q, k, v, kx, vx):
    b, length, _ = q.shape
    nb = length // BLOCK
    n_ctx = kx.shape[1]
    assert nb % 2 == 0
    one = lambda f: pl.BlockSpec((1, BLOCK, KV_WIDTH), f)
    two = pl.BlockSpec((1, 2 * BLOCK, KV_WIDTH), lambda i, m: (i, m, 0))
    prev = lambda i, m: (i, jnp.maximum(2 * m - 1, 0), 0)
    nxt = lambda i, m: (i, jnp.minimum(2 * m + 2, nb - 1), 0)
    ctx_spec = pl.BlockSpec((1, n_ctx, KV_WIDTH), lambda i, m: (i, 0, 0))
    q_spec = pl.BlockSpec((1, 2 * BLOCK, ATTN_WIDTH), lambda i, m: (i, m, 0))
    return pl.pallas_call(
        functools.partial(_attn_kernel, window=True),
        grid=(b, nb // 2),
        in_specs=[
            pl.BlockSpec(memory_space=pltpu.SMEM),
            q_spec,
            one(prev), two, one(nxt),
            one(prev), two, one(nxt),
            ctx_spec, ctx_spec,
        ],
        out_specs=q_spec,
        out_shape=jax.ShapeDtypeStruct((b, length, ATTN_WIDTH), BF16),
        scratch_shapes=_attn_scratch(2 * N_Q_HEADS, BLOCK, n_ctx + 3 * BLOCK),
        compiler_params=_cparams("parallel", "parallel"),
        name="window_attn",
    )(sink, q, k, k, k, v, v, v, kx, vx)


def _context_attention(sink, q, kx, vx):
    b, n_ctx, _ = q.shape
    spec = lambda w: pl.BlockSpec((1, n_ctx, w), lambda i: (i, 0, 0))
    return pl.pallas_call(
        functools.partial(_attn_kernel, window=False),
        grid=(b,),
        in_specs=[pl.BlockSpec(memory_space=pltpu.SMEM), spec(ATTN_WIDTH), spec(KV_WIDTH), spec(KV_WIDTH)],
        out_specs=spec(ATTN_WIDTH),
        out_shape=jax.ShapeDtypeStruct((b, n_ctx, ATTN_WIDTH), BF16),
        scratch_shapes=_attn_scratch(N_Q_HEADS, n_ctx, n_ctx),
        compiler_params=_cparams("parallel"),
        name="context_attn",
    )(sink, q, kx, vx)


def _swiglu_residual(h1, m, g2, wg_ref, wu_ref, wd_ref, acc_ref):
    n2 = _norm_mod(h1, g2, m[4:5], m[3:4]).astype(BF16)
    for c in range(wg_ref.shape[1] // FFN_CHUNK):
        cols = slice(c * FFN_CHUNK, (c + 1) * FFN_CHUNK)
        gate = jnp.dot(n2, wg_ref[:, cols], preferred_element_type=F32)
        up = jnp.dot(n2, wu_ref[:, cols], preferred_element_type=F32)
        act = (gate * _sigmoid(gate)) * up
        part = jnp.dot(act.astype(BF16), wd_ref[cols, :], preferred_element_type=F32)
        if c == 0:
            acc_ref[...] = part
        else:
            acc_ref[...] += part
    return h1 + m[5:6] * acc_ref[...]


def _post0_kernel(h_ref, mod_ref, nmod_ref, g2_ref, gn_ref, f_ref, a_ref, wo_ref, wg_ref, wu_ref, wd_ref,
                  o_ref, u_ref, acc_ref):
    m = mod_ref[0]
    mix = jnp.dot(f_ref[0], wo_ref[:FOURIER_WIDTH, :], preferred_element_type=F32)
    mix += jnp.dot(a_ref[0], wo_ref[FOURIER_WIDTH:, :], preferred_element_type=F32)
    h1 = h_ref[0] + m[2:3] * mix
    h2 = _swiglu_residual(h1, m, g2_ref[...], wg_ref, wu_ref, wd_ref, acc_ref)
    o_ref[0] = h2
    nm = nmod_ref[0]
    u_ref[0] = _norm_mod(h2, gn_ref[...], nm[1:2], nm[0:1]).astype(BF16)


def _gelu_tanh(x):
    sqrt_2_over_pi = np.sqrt(2 / np.pi).astype(np.float32)
    cdf = 0.5 * (1.0 + jnp.tanh(sqrt_2_over_pi * (x + 0.044715 * (x ** 3))))
    return x * cdf


def _post1_kernel(h_ref, mod_ref, g1_ref, g2_ref, gf_ref, dskip_ref, yf_ref, yb_ref, wglu_ref,
                  wg_ref, wu_ref, wd_ref, o_ref, acc_ref):
    m = mod_ref[0]
    h = h_ref[0]
    u = _norm_mod(h, g1_ref[...], m[1:2], m[0:1])
    y = dskip_ref[...] * u + yf_ref[0] + yb_ref[0]
    z = jnp.dot(_gelu_tanh(y).astype(BF16), wglu_ref[...], preferred_element_type=F32)
    h1 = h + m[2:3] * (z[:, :D_MODEL] * _sigmoid(z[:, D_MODEL:]))
    h2 = _swiglu_residual(h1, m, g2_ref[...], wg_ref, wu_ref, wd_ref, acc_ref)
    o_ref[0] = (h2 * lax.rsqrt(jnp.mean(h2 * h2, axis=-1, keepdims=True) + RMS_EPS)) * gf_ref[...]


def _resident(shape):
    zeros = (0,) * len(shape)
    return pl.BlockSpec(shape, lambda i, t: zeros, pipeline_mode=pl.Buffered(1))


def _post_call(kernel, name, h, mods, row_inputs, vec_inputs, weights, rows, out_dtypes):
    b, length, d = h.shape
    row_map = lambda i, t: (i, t, 0)
    mod_spec = lambda m: pl.BlockSpec((1, N_MOD, d), lambda i, t: (i if m.shape[0] > 1 else 0, 0, 0))
    in_specs = [pl.BlockSpec((1, rows, d), row_map)]
    in_specs += [mod_spec(m) for m in mods]
    in_specs += [pl.BlockSpec((1, d), lambda i, t: (0, 0)) for _ in vec_inputs]
    in_specs += [pl.BlockSpec((1, rows, a.shape[-1]), row_map) for a in row_inputs]
    in_specs += [_resident(w.shape) for w in weights]
    return pl.pallas_call(
        kernel,
        grid=(b, length // rows),
        in_specs=in_specs,
        out_specs=[pl.BlockSpec((1, rows, d), row_map) for _ in out_dtypes],
        out_shape=[jax.ShapeDtypeStruct((b, length, d), dt) for dt in out_dtypes],
        scratch_shapes=[pltpu.VMEM((rows, d), F32)],
        compiler_params=_cparams("parallel", "parallel"),
        name=name,
    )(h, *mods, *vec_inputs, *row_inputs, *weights)


def _disc_kernel(are_ref, aim_ref, ldt_ref, bre_ref, bim_ref, oar_ref, oai_ref, obr_ref, obi_ref):
    a_re, a_im = are_ref[0], aim_ref[0]
    dt = jnp.exp(ldt_ref[0])
    mag = jnp.exp(a_re * dt)
    abar_re, abar_im = mag * jnp.cos(a_im * dt), mag * jnp.sin(a_im * dt)
    nr, ni = abar_re - 1.0, abar_im
    den = a_re * a_re + a_im * a_im
    f_re = (nr * a_re + ni * a_im) / den
    f_im = (ni * a_re - nr * a_im) / den
    oar_ref[0] = abar_re
    oai_ref[0] = abar_im
    obr_ref[0] = f_re[None] * bre_ref[0] - f_im[None] * bim_ref[0]
    obi_ref[0] = f_re[None] * bim_ref[0] + f_im[None] * bre_ref[0]


def _discretize(a_re, a_im, log_dt, b_re, b_im):
    nd = a_re.shape[0]
    rows = SSM_WIDTH // LANES
    flat = lambda a: a.reshape(nd, rows, LANES)
    ldt = jnp.broadcast_to(log_dt[:, :, None], (nd, SSM_GROUPS, SSM_STATE))
    chan = lambda a: a.reshape(nd, SSM_WIDTH, SSM_GROUP_W).transpose(0, 2, 1).reshape(nd, SSM_GROUP_W, rows, LANES)
    aspec = pl.BlockSpec((1, rows, LANES), lambda i: (i, 0, 0))
    bspec = pl.BlockSpec((1, SSM_GROUP_W, rows, LANES), lambda i: (i, 0, 0, 0))
    ashape = jax.ShapeDtypeStruct((nd, rows, LANES), F32)
    bshape = jax.ShapeDtypeStruct((nd, SSM_GROUP_W, rows, LANES), F32)
    return pl.pallas_call(
        _disc_kernel,
        grid=(nd,),
        in_specs=[aspec, aspec, aspec, bspec, bspec],
        out_specs=[aspec, aspec, bspec, bspec],
        out_shape=[ashape, ashape, bshape, bshape],
        compiler_params=_cparams("parallel"),
        name="s5_discretize",
    )(flat(a_re), flat(a_im), flat(ldt), chan(b_re), chan(b_im))


def _s5_kernel(uc_ref, ul_ref, wd_ref, wr_ref, are_ref, aim_ref, y_ref,
               xa_re, xa_im, xb_re, xb_im, hre, him, *, reverse):
    step_idx = pl.program_id(1)
    steps = S5_CHUNK
    cols_per_slab = S5_SLAB_STATES // LANES
    seg = steps // S5_SLABS

    @pl.when(step_idx == 0)
    def _():
        hre[...] = jnp.zeros_like(hre)
        him[...] = jnp.zeros_like(him)

    def plane_of(col):
        return col // SUBLANES, col % SUBLANES

    def drive_slab(u, kk, xre, xim):
        bu = jnp.dot(u(kk * LANES, (kk + 1) * LANES), wd_ref[kk], preferred_element_type=F32)
        for c in range(cols_per_slab):
            pln, sub = plane_of(kk * cols_per_slab + c)
            xre[pln, pl.ds(sub, steps, stride=SUBLANES), :] = bu[:, c * LANES:(c + 1) * LANES]
            xim[pln, pl.ds(sub, steps, stride=SUBLANES), :] = (
                bu[:, S5_SLAB_STATES + c * LANES:S5_SLAB_STATES + (c + 1) * LANES])

    def read_slab(kk, xre, xim):
        parts = []
        for src in (xre, xim):
            for c in range(cols_per_slab):
                pln, sub = plane_of(kk * cols_per_slab + c)
                parts.append(src[pln, pl.ds(sub, steps, stride=SUBLANES), :].astype(BF16))
        return jnp.dot(jnp.concatenate(parts, axis=1), wr_ref[kk], preferred_element_type=F32)

    a_r = [are_ref[p] for p in range(S5_PLANES)]
    a_i = [aim_ref[p] for p in range(S5_PLANES)]

    def one_step(t, h, xre, xim):
        h_r, h_i = h
        r0 = t * SUBLANES if isinstance(t, int) else pl.multiple_of(t * SUBLANES, SUBLANES)
        new_r, new_i = [], []
        for p in range(S5_PLANES):
            n_r = a_r[p] * h_r[p] - a_i[p] * h_i[p] + xre[p, pl.ds(r0, SUBLANES), :]
            n_i = a_r[p] * h_i[p] + a_i[p] * h_r[p] + xim[p, pl.ds(r0, SUBLANES), :]
            xre[p, pl.ds(r0, SUBLANES), :] = n_r
            xim[p, pl.ds(r0, SUBLANES), :] = n_i
            new_r.append(n_r)
            new_i.append(n_i)
        return tuple(new_r), tuple(new_i)

    def time_of(i):
        return (steps - 1 - i) if reverse else i

    def scan_segment(k, h, xre, xim):
        for i in range(k * seg, (k + 1) * seg):
            h = one_step(time_of(i), h, xre, xim)
        return h

    def load_state():
        return (tuple(hre[p] for p in range(S5_PLANES)), tuple(him[p] for p in range(S5_PLANES)))

    def store_state(h):
        for p in range(S5_PLANES):
            hre[p] = h[0][p]
            him[p] = h[1][p]

    @pl.when(step_idx == 0)
    def _():
        u = lambda lo, hi: uc_ref[0, :, lo:hi]
        for kk in range(S5_SLABS):
            drive_slab(u, kk, xa_re, xa_im)
        h = lax.fori_loop(0, steps, lambda i, h: one_step(time_of(i), h, xa_re, xa_im), load_state(), unroll=8)
        store_state(h)

    @pl.when(step_idx > 0)
    def _():
        first, second = slice(0, steps), slice(steps, 2 * steps)
        rows_a, rows_b = (second, first) if reverse else (first, second)
        u_a = lambda lo, hi: ul_ref[0, rows_a, lo:hi]
        u_b = lambda lo, hi: ul_ref[0, rows_b, lo:hi]
        for kk in range(S5_SLABS):
            drive_slab(u_a, kk, xa_re, xa_im)
        h = load_state()
        for kk in range(S5_SLABS):
            drive_slab(u_b, kk, xb_re, xb_im)
            h = scan_segment(kk, h, xa_re, xa_im)
        for kk in range(S5_SLABS):
            h = scan_segment(kk, h, xb_re, xb_im)
            y_ref[0, rows_a, kk * LANES:(kk + 1) * LANES] = read_slab(kk, xa_re, xa_im)
        store_state(h)
        for kk in range(S5_SLABS):
            y_ref[0, rows_b, kk * LANES:(kk + 1) * LANES] = read_slab(kk, xb_re, xb_im)


def _s5_direction(u_ctx, u_lat, w_drive, w_read, a_re, a_im, *, reverse):
    b, length, d = u_lat.shape
    n_ctx = u_ctx.shape[1]
    pair = 2 * S5_CHUNK
    assert n_ctx == S5_CHUNK and length % pair == 0
    n_pairs = length // pair
    if reverse:
        lat_map = lambda i, n: (i, jnp.minimum(n_pairs - n, n_pairs - 1), 0)
    else:
        lat_map = lambda i, n: (i, jnp.maximum(n - 1, 0), 0)
    whole = lambda shape: pl.BlockSpec(shape, lambda i, n: (0,) * len(shape))
    plane_buf = pltpu.VMEM((S5_PLANES, S5_CHUNK * SUBLANES, LANES), F32)
    state_buf = pltpu.VMEM((S5_PLANES, SUBLANES, LANES), F32)
    return pl.pallas_call(
        functools.partial(_s5_kernel, reverse=reverse),
        grid=(b, n_pairs + 1),
        in_specs=[
            pl.BlockSpec((1, n_ctx, d), lambda i, n: (i, 0, 0)),
            pl.BlockSpec((1, pair, d), lat_map),
            whole(w_drive.shape), whole(w_read.shape), whole(a_re.shape), whole(a_im.shape),
        ],
        out_specs=pl.BlockSpec((1, pair, d), lat_map),
        out_shape=jax.ShapeDtypeStruct((b, length, d), F32),
        scratch_shapes=[plane_buf, plane_buf, plane_buf, plane_buf, state_buf, state_buf],
        compiler_params=_cparams("parallel", "arbitrary"),
        name="s5_scan_bwd" if reverse else "s5_scan_fwd",
    )(u_ctx, u_lat, w_drive, w_read, a_re, a_im)


def _rope_tables(length):
    rows = length // GRID_W
    t_row = jnp.repeat(jnp.arange(rows, dtype=F32), GRID_W)
    t_col = jnp.tile(jnp.arange(GRID_W, dtype=F32), rows)
    n_freq = HEAD_DIM // 4
    inv_freq = ROPE_BASE ** (-jnp.arange(n_freq, dtype=F32) / n_freq)
    ang = jnp.concatenate([t_row[:, None] * inv_freq, t_col[:, None] * inv_freq], axis=-1)
    cos, sin = jnp.cos(ang), jnp.sin(ang)
    reps = LANES // HEAD_DIM
    return (jnp.tile(jnp.concatenate([cos, cos], axis=-1), (1, reps)),
            jnp.tile(jnp.concatenate([-sin, sin], axis=-1), (1, reps)))


def _half_split_perm():
    perm = np.arange(MIX_IN_WIDTH)
    head = np.concatenate([np.arange(0, HEAD_DIM, 2), np.arange(1, HEAD_DIM, 2)])
    for start in range(FOURIER_WIDTH, KV_START + KV_WIDTH, HEAD_DIM):
        perm[start:start + HEAD_DIM] = start + head
    return perm


def _channel_dft():
    c = np.arange(FOURIER_GROUP_W)
    ang = 2.0 * np.pi * np.outer(c, c) / FOURIER_GROUP_W
    eye = np.eye(FOURIER_GROUPS)
    scale = FOURIER_GROUP_W ** -0.5
    table = np.concatenate([np.kron(eye, np.cos(ang) * scale), np.kron(eye, np.sin(ang) * scale)], axis=1)
    return jnp.asarray(table, F32).astype(BF16)


def _time_dft_matrices(t, rows=None):
    r1 = 1 << ((t.bit_length() - 1 + 1) // 2)
    r2 = t // r1
    assert r1 * r2 == t and r2 % SUBLANES == 0
    rows = t if rows is None else rows
    assert rows % r2 == 0
    n = jnp.arange(t, dtype=jnp.int32)[None, :]
    unit = 2.0 * math.pi / t
    ang_a = (((jnp.arange(rows // r2, dtype=jnp.int32) * r2)[:, None] * n) % t).astype(F32) * unit
    ang_b = ((jnp.arange(r2, dtype=jnp.int32)[:, None] * n) % t).astype(F32) * unit
    scale = t ** -0.5
    ca, sa = jnp.cos(ang_a)[:, None, :], jnp.sin(ang_a)[:, None, :]
    cb, sb = (jnp.cos(ang_b) * scale)[None, :, :], (jnp.sin(ang_b) * scale)[None, :, :]
    cmat = (ca * cb - sa * sb).reshape(rows, t).astype(BF16)
    smat = (-(sa * cb + ca * sb)).reshape(rows, t).astype(BF16)
    return cmat, smat


def _ffn_weights(w_gate, w_up, w_down):
    return w_gate.astype(BF16), w_up.astype(BF16), w_down.astype(BF16)


def _s5_weights(bbar_re, bbar_im, c_re, c_im):
    nd = bbar_re.shape[0]
    eye = jnp.eye(S5_SLAB_GROUPS, dtype=F32)

    def drive(bb):
        bb = bb.reshape(nd, SSM_GROUP_W, S5_SLABS, S5_SLAB_GROUPS, SSM_STATE)
        w = jnp.einsum("dhkgp,gG->dkghGp", bb, eye)
        return w.reshape(nd, S5_SLABS, LANES, S5_SLAB_STATES)

    def read(cc):
        cc = cc.astype(F32).reshape(nd, S5_SLABS, S5_SLAB_GROUPS, SSM_GROUP_W, SSM_STATE)
        w = jnp.einsum("dkghp,gG->dkgpGh", cc, eye)
        return w.reshape(nd, S5_SLABS, S5_SLAB_STATES, LANES)

    w_drive = jnp.concatenate([drive(bbar_re), drive(bbar_im)], axis=-1).astype(BF16)
    w_read = jnp.concatenate([read(c_re), -read(c_im)], axis=2).astype(BF16)
    return w_drive, w_read


def kernel(x, c, ctx, c_ctx, mod_w, mod_b, norm_g, ffn_w_gate, ffn_w_up, ffn_w_down, mix_w_in, mix_w_out,
           attn_sink, ssm_a_re, ssm_a_im, ssm_log_dt, ssm_b_re, ssm_b_im, ssm_c_re, ssm_c_im, ssm_d,
           ssm_glu_w, final_g):
    bn, length, d = x.shape
    n_ctx = ctx.shape[1]
    depth = mod_w.shape[0]
    assert depth == 2 and d == D_MODEL

    cond = jnp.zeros((SUBLANES, d), F32).at[:bn].set(c).at[bn].set(c_ctx)
    mods = _modulations(cond, mod_w, mod_b).reshape(depth, SUBLANES, N_MOD, d)
    mods_lat, mods_ctx = mods[:, :bn], mods[:, bn:bn + 1]
    vec = lambda a: a.reshape(1, d)

    w_in = mix_w_in[0][:, _half_split_perm()].astype(BF16)
    w_out = mix_w_out[0].astype(BF16)
    bcs = _channel_dft()
    cos_t, sin_t = _rope_tables(length)
    ffn0 = _ffn_weights(ffn_w_gate[0], ffn_w_up[0], ffn_w_down[0])
    g_mix, g_ffn = vec(norm_g[0, 0]), vec(norm_g[0, 1])

    uv, q, k, v = _pre0(x, mods_lat[0], g_mix, w_in, bcs, cos_t, sin_t, rope=True, rows=TOKEN_TILE)
    uvc, qc, kc, vc = _pre0(ctx, mods_ctx[0], g_mix, w_in, bcs, cos_t, sin_t, rope=False, rows=n_ctx)
    four = _time_dft_symmetric(*_time_dft_matrices(length, length // 2 + BLOCK), uv)
    four_c = _time_dft(*_time_dft_matrices(n_ctx), uvc)
    att = _window_attention(attn_sink[0], q, k, v, kc, vc)
    att_c = _context_attention(attn_sink[0], qc, kc, vc)
    g_mix1 = vec(norm_g[1, 0])
    h, u_lat = _post_call(_post0_kernel, "mix_out_ffn", x, [mods_lat[0], mods_lat[1]], [four, att],
                          [g_ffn, g_mix1], [w_out, *ffn0], TOKEN_TILE, [F32, BF16])
    _, u_ctx = _post_call(_post0_kernel, "mix_out_ffn_ctx", ctx, [mods_ctx[0], mods_ctx[1]], [four_c, att_c],
                          [g_ffn, g_mix1], [w_out, *ffn0], n_ctx, [F32, BF16])

    g_mix, g_ffn = g_mix1, vec(norm_g[1, 1])
    abar_re, abar_im, bbar_re, bbar_im = _discretize(
        ssm_a_re[0], ssm_a_im[0], ssm_log_dt[0], ssm_b_re[0], ssm_b_im[0])
    w_drive, w_read = _s5_weights(bbar_re, bbar_im, ssm_c_re[0], ssm_c_im[0])
    planes = lambda a: a.reshape(2, S5_PLANES, SUBLANES, LANES)
    abar_re, abar_im = planes(abar_re), planes(abar_im)
    y_dir = [_s5_direction(u_ctx, u_lat, w_drive[i], w_read[i], abar_re[i], abar_im[i], reverse=bool(i))
             for i in range(2)]
    ffn1 = _ffn_weights(ffn_w_gate[1], ffn_w_up[1], ffn_w_down[1])
    return _post_call(_post1_kernel, "s5_glu_ffn_norm", h, [mods_lat[1]], y_dir,
                      [g_mix, g_ffn, vec(final_g), vec(ssm_d[0])],
                      [ssm_glu_w[0].astype(BF16), *ffn1], TOKEN_TILE, [F32])[0]
```

```python
import functools
import math

import numpy as np
import jax
import jax.numpy as jnp
from jax import lax
from jax.experimental import pallas as pl
from jax.experimental.pallas import tpu as pltpu

F32 = jnp.float32
BF16 = jnp.bfloat16

D_MODEL = 1024
GRID_W = 64
N_MOD = 6
N_Q_HEADS = 8
N_KV_HEADS = 2
HEAD_DIM = 64
ATTN_WIDTH = N_Q_HEADS * HEAD_DIM
KV_WIDTH = N_KV_HEADS * HEAD_DIM
FOURIER_GROUPS = 8
FOURIER_GROUP_W = 64
FOURIER_WIDTH = FOURIER_GROUPS * FOURIER_GROUP_W
KV_START = FOURIER_WIDTH + ATTN_WIDTH
MIX_IN_WIDTH = KV_START + 2 * KV_WIDTH
WINDOW = 128
BLOCK = 128
ATTN_SCALE = HEAD_DIM ** -0.5
LOG2_E = math.log2(math.e)
ROPE_BASE = 10000.0
NEG_INF = -1e30
SSM_GROUP_W = 16
SSM_GROUPS = D_MODEL // SSM_GROUP_W
SSM_STATE = 64
SSM_WIDTH = SSM_GROUPS * SSM_STATE
RMS_EPS = 1e-6

LANES = 128
SUBLANES = 8
VMEM_LIMIT_BYTES = 56 * 1024 * 1024

TOKEN_TILE = 512
FFN_CHUNK = 256
MOD_TILE = 1536
DFT_ROW_TILE = 1024
DFT_K_TILE = 4096
S5_CHUNK = 256
S5_SLAB_GROUPS = LANES // SSM_GROUP_W
S5_SLABS = SSM_GROUPS // S5_SLAB_GROUPS
S5_SLAB_STATES = S5_SLAB_GROUPS * SSM_STATE
S5_COLS = SSM_WIDTH // LANES
S5_PLANES = S5_COLS // SUBLANES


def _cparams(*sem):
    return pltpu.CompilerParams(dimension_semantics=sem, vmem_limit_bytes=VMEM_LIMIT_BYTES)


def _sigmoid(x):
    return 1.0 / (1.0 + jnp.exp(-x))


def _norm_mod(x, g, scale, shift):
    y = x * lax.rsqrt(jnp.mean(x * x, axis=-1, keepdims=True) + RMS_EPS)
    return (y * g) * (1.0 + scale) + shift


def _split_bf16(x):
    hi = x.astype(BF16)
    lo = (x - hi.astype(F32)).astype(BF16)
    return hi, lo


def _mod_kernel(cond_ref, w_ref, b_ref, o_ref):
    x = cond_ref[...]
    s_hi, s_lo = _split_bf16(x * _sigmoid(x))
    w_hi, w_lo = _split_bf16(w_ref[0])
    acc = jnp.dot(s_hi, w_hi, preferred_element_type=F32)
    acc += jnp.dot(s_lo, w_hi, preferred_element_type=F32)
    acc += jnp.dot(s_hi, w_lo, preferred_element_type=F32)
    o_ref[0] = acc + b_ref[0]


def _modulations(cond, mod_w, mod_b):
    depth, d, n = mod_w.shape
    rows = cond.shape[0]
    return pl.pallas_call(
        _mod_kernel,
        grid=(depth, n // MOD_TILE),
        in_specs=[
            pl.BlockSpec((rows, d), lambda l, j: (0, 0)),
            pl.BlockSpec((1, d, MOD_TILE), lambda l, j: (l, 0, j)),
            pl.BlockSpec((1, 1, MOD_TILE), lambda l, j: (l, 0, j)),
        ],
        out_specs=pl.BlockSpec((1, rows, MOD_TILE), lambda l, j: (l, 0, j)),
        out_shape=jax.ShapeDtypeStruct((depth, rows, n), F32),
        compiler_params=_cparams("parallel", "parallel"),
        name="adaln_mod",
    )(cond, mod_w, mod_b.reshape(depth, 1, n))


def _pre0_kernel(x_ref, mod_ref, g_ref, w_ref, bcs_ref, cos_ref, sin_ref,
                 uv_ref, q_ref, k_ref, v_ref, *, rope):
    m = mod_ref[0]
    n = _norm_mod(x_ref[0], g_ref[...], m[1:2], m[0:1])
    p = jnp.dot(n.astype(BF16), w_ref[...], preferred_element_type=F32)
    uv = jnp.dot(p[:, :FOURIER_WIDTH].astype(BF16), bcs_ref[...], preferred_element_type=F32)
    uv_ref[0, 0] = uv[:, :FOURIER_WIDTH].astype(BF16)
    uv_ref[0, 1] = uv[:, FOURIER_WIDTH:].astype(BF16)

    def rot(x):
        if not rope:
            return x
        lane = lax.broadcasted_iota(jnp.int32, x.shape, 1)
        first = (lane & (HEAD_DIM // 2)) == 0
        partner = jnp.where(first, pltpu.roll(x, LANES - HEAD_DIM // 2, 1),
                            pltpu.roll(x, HEAD_DIM // 2, 1))
        return x * cos_ref[...] + partner * sin_ref[...]

    for j in range(ATTN_WIDTH // LANES):
        lo = FOURIER_WIDTH + j * LANES
        q_ref[0, :, j * LANES:(j + 1) * LANES] = (rot(p[:, lo:lo + LANES]) * (ATTN_SCALE * LOG2_E)).astype(BF16)
    k_ref[0] = rot(p[:, KV_START:KV_START + KV_WIDTH]).astype(BF16)
    v_ref[0] = p[:, KV_START + KV_WIDTH:].astype(BF16)


def _pre0(x, mods, g, w_in, bcs, cos_t, sin_t, *, rope, rows):
    b, length, d = x.shape
    grid = (b, length // rows)
    row_map = lambda i, t: (i, t, 0)
    const2 = lambda i, t: (0, 0)
    return pl.pallas_call(
        functools.partial(_pre0_kernel, rope=rope),
        grid=grid,
        in_specs=[
            pl.BlockSpec((1, rows, d), row_map),
            pl.BlockSpec((1, N_MOD, d), lambda i, t: (i if mods.shape[0] > 1 else 0, 0, 0)),
            pl.BlockSpec((1, d), const2),
            pl.BlockSpec((d, MIX_IN_WIDTH), const2),
            pl.BlockSpec((FOURIER_WIDTH, 2 * FOURIER_WIDTH), const2),
            pl.BlockSpec((rows, LANES), lambda i, t: (t, 0)),
            pl.BlockSpec((rows, LANES), lambda i, t: (t, 0)),
        ],
        out_specs=[
            pl.BlockSpec((1, 2, rows, FOURIER_WIDTH), lambda i, t: (i, 0, t, 0)),
            pl.BlockSpec((1, rows, ATTN_WIDTH), row_map),
            pl.BlockSpec((1, rows, KV_WIDTH), row_map),
            pl.BlockSpec((1, rows, KV_WIDTH), row_map),
        ],
        out_shape=[
            jax.ShapeDtypeStruct((b, 2, length, FOURIER_WIDTH), BF16),
            jax.ShapeDtypeStruct((b, length, ATTN_WIDTH), BF16),
            jax.ShapeDtypeStruct((b, length, KV_WIDTH), BF16),
            jax.ShapeDtypeStruct((b, length, KV_WIDTH), BF16),
        ],
        compiler_params=_cparams("parallel", "parallel"),
        name="mix_in_rope" if rope else "mix_in_ctx",
    )(x, mods, g, w_in, bcs, cos_t, sin_t)


def _dft_kernel(c_ref, s_ref, u_ref, v_ref, o_ref, acc_ref):
    k = pl.program_id(2)

    @pl.when(k == 0)
    def _():
        acc_ref[...] = jnp.zeros_like(acc_ref)

    acc_ref[...] += (jnp.dot(c_ref[...], u_ref[0, 0], preferred_element_type=F32)
                     + jnp.dot(s_ref[...], v_ref[0, 0], preferred_element_type=F32))

    @pl.when(k == pl.num_programs(2) - 1)
    def _():
        o_ref[0] = acc_ref[...].astype(BF16)


def _time_dft(cmat, smat, uv):
    b, _, t, w = uv.shape
    tm = min(DFT_ROW_TILE, t)
    tk = min(DFT_K_TILE, t)
    mat_spec = pl.BlockSpec((tm, tk), lambda i, r, k: (r, k))
    return pl.pallas_call(
        _dft_kernel,
        grid=(b, t // tm, t // tk),
        in_specs=[
            mat_spec, mat_spec,
            pl.BlockSpec((1, 1, tk, w), lambda i, r, k: (i, 0, k, 0)),
            pl.BlockSpec((1, 1, tk, w), lambda i, r, k: (i, 1, k, 0)),
        ],
        out_specs=pl.BlockSpec((1, tm, w), lambda i, r, k: (i, r, 0)),
        out_shape=jax.ShapeDtypeStruct((b, t, w), BF16),
        scratch_shapes=[pltpu.VMEM((tm, w), F32)],
        compiler_params=_cparams("parallel", "parallel", "arbitrary"),
        name="time_dft",
    )(cmat, smat, uv, uv)


def _dft_half_kernel(c_ref, s_ref, u_ref, v_ref, lo_ref, mir_ref, p_acc, q_acc):
    k = pl.program_id(2)

    @pl.when(k == 0)
    def _():
        p_acc[...] = jnp.zeros_like(p_acc)
        q_acc[...] = jnp.zeros_like(q_acc)

    p_acc[...] += jnp.dot(c_ref[...], u_ref[0, 0], preferred_element_type=F32)
    q_acc[...] += jnp.dot(s_ref[...], v_ref[0, 0], preferred_element_type=F32)

    @pl.when(k == pl.num_programs(2) - 1)
    def _():
        lo_ref[0] = (p_acc[...] + q_acc[...]).astype(BF16)
        mir_ref[0] = (p_acc[...] - q_acc[...]).astype(BF16)


def _dft_mirror_kernel(lo_ref, mir_ref, o_ref):
    i = pl.program_id(1)
    half = o_ref.shape[1]

    @pl.when(i == 0)
    def _():
        o_ref[0] = lo_ref[0, :half, :]

    @pl.when(i == 1)
    def _():
        r = lax.broadcasted_iota(jnp.int32, (BLOCK, BLOCK), 0)
        c = lax.broadcasted_iota(jnp.int32, (BLOCK, BLOCK), 1)
        rev = jnp.where(c == BLOCK - r, 1.0, 0.0).astype(BF16)
        first = jnp.where((r == 0) & (c == 0), 1.0, 0.0).astype(BF16)
        for a in range(half // BLOCK):
            top = half - a * BLOCK
            z = jnp.dot(rev, mir_ref[0, top - BLOCK:top, :], preferred_element_type=F32)
            z += jnp.dot(first, mir_ref[0, top:top + BLOCK, :], preferred_element_type=F32)
            o_ref[0, a * BLOCK:(a + 1) * BLOCK, :] = z.astype(BF16)


def _time_dft_symmetric(cmat, smat, uv):
    b, _, t, w = uv.shape
    rows = cmat.shape[0]
    assert rows == t // 2 + BLOCK and rows % (2 * SUBLANES) == 0
    tm = rows // 4
    tk = min(DFT_K_TILE, t)
    mat_spec = pl.BlockSpec((tm, tk), lambda i, r, k: (r, k))
    out_spec = pl.BlockSpec((1, tm, w), lambda i, r, k: (i, r, 0))
    lo, mir = pl.pallas_call(
        _dft_half_kernel,
        grid=(b, rows // tm, t // tk),
        in_specs=[
            mat_spec, mat_spec,
            pl.BlockSpec((1, 1, tk, w), lambda i, r, k: (i, 0, k, 0)),
            pl.BlockSpec((1, 1, tk, w), lambda i, r, k: (i, 1, k, 0)),
        ],
        out_specs=[out_spec, out_spec],
        out_shape=[jax.ShapeDtypeStruct((b, rows, w), BF16)] * 2,
        scratch_shapes=[pltpu.VMEM((tm, w), F32), pltpu.VMEM((tm, w), F32)],
        compiler_params=_cparams("parallel", "parallel", "arbitrary"),
        name="time_dft_half",
    )(cmat, smat, uv, uv)
    whole = pl.BlockSpec((1, rows, w), lambda i, n: (i, 0, 0))
    return pl.pallas_call(
        _dft_mirror_kernel,
        grid=(b, 2),
        in_specs=[whole, whole],
        out_specs=pl.BlockSpec((1, t // 2, w), lambda i, n: (i, n, 0)),
        out_shape=jax.ShapeDtypeStruct((b, t, w), BF16),
        compiler_params=_cparams("parallel", "parallel"),
        name="time_dft_mirror",
    )(lo, mir)


def _attn_kernel(sink_ref, q_ref, *refs, window):
    if window:
        kp_ref, km_ref, kn_ref, vp_ref, vm_ref, vn_ref, kx_ref, vx_ref, o_ref, s_scr, p_scr = refs
        k_pieces = [kx_ref[0], kp_ref[0], km_ref[0, :BLOCK, :], km_ref[0, BLOCK:, :], kn_ref[0]]
        v_pieces = [vx_ref[0], vp_ref[0], vm_ref[0, :BLOCK, :], vm_ref[0, BLOCK:, :], vn_ref[0]]
        subs = ((0, 1, 2, 3), (0, 2, 3, 4))
    else:
        kx_ref, vx_ref, o_ref, s_scr, p_scr = refs
        k_pieces, v_pieces = [kx_ref[0]], [vx_ref[0]]
        subs = ((0,),)
    n_sub = len(subs)
    n_q = q_ref.shape[1] // n_sub

    def split_heads(x):
        low = lax.broadcasted_iota(jnp.int32, x.shape, 1) < HEAD_DIM
        zero = jnp.zeros_like(x)
        swap = pltpu.roll(x.astype(F32), HEAD_DIM, 1).astype(BF16)
        return ((jnp.where(low, x, zero), jnp.where(low, swap, zero)),
                (jnp.where(low, zero, swap), jnp.where(low, zero, x)))

    k_ext = [split_heads(x) for x in k_pieces]
    v_ext = [split_heads(x) for x in v_pieces]
    cache = {}

    def operand(name, ext, sub, half, g):
        key = (name, sub, half, g)
        if key not in cache:
            parts = [ext[i][half][g] for i in subs[sub]]
            cache[key] = parts[0] if len(parts) == 1 else jnp.concatenate(parts, axis=0)
        return cache[key]

    if window:
        step = pl.program_id(1)
        last = pl.num_programs(1) - 1
        n_ctx = kx_ref.shape[1]
        row = lax.broadcasted_iota(jnp.int32, (n_q, BLOCK), 0)
        col = lax.broadcasted_iota(jnp.int32, (n_q, BLOCK), 1)
        ok_prev = (col >= row + jnp.where(step > 0, 0, BLOCK), col >= row)
        ok_next = (col <= row, col <= row - jnp.where(step < last, 0, BLOCK))

    q_per_kv = N_Q_HEADS // N_KV_HEADS
    n_pairs = ATTN_WIDTH // LANES
    for sub in range(n_sub):
        rows = slice(sub * n_q, (sub + 1) * n_q)
        for j in range(n_pairs):
            g = (2 * j) // q_per_kv
            qp = q_ref[0, rows, j * LANES:(j + 1) * LANES]
            for half in range(2):
                s = lax.dot_general(qp, operand("k", k_ext, sub, half, g), (((1,), (1,)), ((), ())),
                                    preferred_element_type=F32)
                item = sub * N_Q_HEADS + 2 * j + half
                if window:
                    s_scr[item, :, :n_ctx] = s[:, :n_ctx]
                    s_scr[item, :, n_ctx:n_ctx + BLOCK] = jnp.where(
                        ok_prev[sub], s[:, n_ctx:n_ctx + BLOCK], NEG_INF)
                    s_scr[item, :, n_ctx + BLOCK:n_ctx + 2 * BLOCK] = s[:, n_ctx + BLOCK:n_ctx + 2 * BLOCK]
                    s_scr[item, :, n_ctx + 2 * BLOCK:] = jnp.where(
                        ok_next[sub], s[:, n_ctx + 2 * BLOCK:], NEG_INF)
                else:
                    s_scr[item] = s
    inv = []
    n_keys = s_scr.shape[2]
    col_blocks = [slice(c, c + LANES) for c in range(0, n_keys, LANES)]
    for item in range(n_sub * N_Q_HEADS):
        sink = sink_ref[item % N_Q_HEADS] * LOG2_E
        mx = s_scr[item, :, col_blocks[0]]
        for cb in col_blocks[1:]:
            mx = jnp.maximum(mx, s_scr[item, :, cb])
        mx = jnp.maximum(jnp.max(mx, axis=1, keepdims=True), sink)
        total = None
        for cb in col_blocks:
            p = jnp.exp2(s_scr[item, :, cb] - mx)
            total = p if total is None else total + p
            p_scr[item, :, cb] = p.astype(BF16)
        inv.append(1.0 / (jnp.sum(total, axis=1, keepdims=True) + jnp.exp2(sink - mx)))
    for sub in range(n_sub):
        rows = slice(sub * n_q, (sub + 1) * n_q)
        for j in range(n_pairs):
            g = (2 * j) // q_per_kv
            a = sub * N_Q_HEADS + 2 * j
            out = jnp.dot(p_scr[a], operand("v", v_ext, sub, 0, g), preferred_element_type=F32) * inv[a]
            out += jnp.dot(p_scr[a + 1], operand("v", v_ext, sub, 1, g), preferred_element_type=F32) * inv[a + 1]
            o_ref[0, rows, j * LANES:(j + 1) * LANES] = out.astype(BF16)


def _attn_scratch(n_items, n_q, n_keys):
    return [pltpu.VMEM((n_items, n_q, n_keys), F32), pltpu.VMEM((n_items, n_q, n_keys), BF16)]


def _window_attention(sink, q, k, v, kx, vx):
    b, length, _ = q.shape
    nb = length // BLOCK
    n_ctx = kx.shape[1]
    assert nb % 2 == 0
    one = lambda f: pl.BlockSpec((1, BLOCK, KV_WIDTH), f)
    two = pl.BlockSpec((1, 2 * BLOCK, KV_WIDTH), lambda i, m: (i, m, 0))
    prev = lambda i, m: (i, jnp.maximum(2 * m - 1, 0), 0)
    nxt = lambda i, m: (i, jnp.minimum(2 * m + 2, nb - 1), 0)
    ctx_spec = pl.BlockSpec((1, n_ctx, KV_WIDTH), lambda i, m: (i, 0, 0))
    q_spec = pl.BlockSpec((1, 2 * BLOCK, ATTN_WIDTH), lambda i, m: (i, m, 0))
    return pl.pallas_call(
        functools.partial(_attn_kernel, window=True),
        grid=(b, nb // 2),
        in_specs=[
            pl.BlockSpec(memory_space=pltpu.SMEM),
            q_spec,
            one(prev), two, one(nxt),
            one(prev), two, one(nxt),
            ctx_spec, ctx_spec,
        ],
        out_specs=q_spec,
        out_shape=jax.ShapeDtypeStruct((b, length, ATTN_WIDTH), BF16),
        scratch_shapes=_attn_scratch(2 * N_Q_HEADS, BLOCK, n_ctx + 3 * BLOCK),
        compiler_params=_cparams("parallel", "parallel"),
        name="window_attn",
    )(sink, q, k, k, k, v, v, v, kx, vx)


def _context_attention(sink, q, kx, vx):
    b, n_ctx, _ = q.shape
    spec = lambda w: pl.BlockSpec((1, n_ctx, w), lambda i: (i, 0, 0))
    return pl.pallas_call(
        functools.partial(_attn_kernel, window=False),
        grid=(b,),
        in_specs=[pl.BlockSpec(memory_space=pltpu.SMEM), spec(ATTN_WIDTH), spec(KV_WIDTH), spec(KV_WIDTH)],
        out_specs=spec(ATTN_WIDTH),
        out_shape=jax.ShapeDtypeStruct((b, n_ctx, ATTN_WIDTH), BF16),
        scratch_shapes=_attn_scratch(N_Q_HEADS, n_ctx, n_ctx),
        compiler_params=_cparams("parallel"),
        name="context_attn",
    )(sink, q, kx, vx)


def _swiglu_residual(h1, m, g2, wg_ref, wu_ref, wd_ref, acc_ref):
    n2 = _norm_mod(h1, g2, m[4:5], m[3:4]).astype(BF16)
    for c in range(wg_ref.shape[1] // FFN_CHUNK):
        cols = slice(c * FFN_CHUNK, (c + 1) * FFN_CHUNK)
        gate = jnp.dot(n2, wg_ref[:, cols], preferred_element_type=F32)
        up = jnp.dot(n2, wu_ref[:, cols], preferred_element_type=F32)
        act = (gate * _sigmoid(gate)) * up
        part = jnp.dot(act.astype(BF16), wd_ref[cols, :], preferred_element_type=F32)
        if c == 0:
            acc_ref[...] = part
        else:
            acc_ref[...] += part
    return h1 + m[5:6] * acc_ref[...]


def _post0_kernel(h_ref, mod_ref, nmod_ref, g2_ref, gn_ref, f_ref, a_ref, wo_ref, wg_ref, wu_ref, wd_ref,
                  o_ref, u_ref, acc_ref):
    m = mod_ref[0]
    mix = jnp.dot(f_ref[0], wo_ref[:FOURIER_WIDTH, :], preferred_element_type=F32)
    mix += jnp.dot(a_ref[0], wo_ref[FOURIER_WIDTH:, :], preferred_element_type=F32)
    h1 = h_ref[0] + m[2:3] * mix
    h2 = _swiglu_residual(h1, m, g2_ref[...], wg_ref, wu_ref, wd_ref, acc_ref)
    o_ref[0] = h2
    nm = nmod_ref[0]
    u_ref[0] = _norm_mod(h2, gn_ref[...], nm[1:2], nm[0:1]).astype(BF16)


def _gelu_tanh(x):
    sqrt_2_over_pi = np.sqrt(2 / np.pi).astype(np.float32)
    cdf = 0.5 * (1.0 + jnp.tanh(sqrt_2_over_pi * (x + 0.044715 * (x ** 3))))
    return x * cdf


def _post1_kernel(h_ref, mod_ref, g1_ref, g2_ref, gf_ref, dskip_ref, yf_ref, yb_ref, wglu_ref,
                  wg_ref, wu_ref, wd_ref, o_ref, acc_ref):
    m = mod_ref[0]
    h = h_ref[0]
    u = _norm_mod(h, g1_ref[...], m[1:2], m[0:1])
    y = dskip_ref[...] * u + yf_ref[0] + yb_ref[0]
    z = jnp.dot(_gelu_tanh(y).astype(BF16), wglu_ref[...], preferred_element_type=F32)
    h1 = h + m[2:3] * (z[:, :D_MODEL] * _sigmoid(z[:, D_MODEL:]))
    h2 = _swiglu_residual(h1, m, g2_ref[...], wg_ref, wu_ref, wd_ref, acc_ref)
    o_ref[0] = (h2 * lax.rsqrt(jnp.mean(h2 * h2, axis=-1, keepdims=True) + RMS_EPS)) * gf_ref[...]


def _resident(shape):
    zeros = (0,) * len(shape)
    return pl.BlockSpec(shape, lambda i, t: zeros, pipeline_mode=pl.Buffered(1))


def _post_call(kernel, name, h, mods, row_inputs, vec_inputs, weights, rows, out_dtypes):
    b, length, d = h.shape
    row_map = lambda i, t: (i, t, 0)
    mod_spec = lambda m: pl.BlockSpec((1, N_MOD, d), lambda i, t: (i if m.shape[0] > 1 else 0, 0, 0))
    in_specs = [pl.BlockSpec((1, rows, d), row_map)]
    in_specs += [mod_spec(m) for m in mods]
    in_specs += [pl.BlockSpec((1, d), lambda i, t: (0, 0)) for _ in vec_inputs]
    in_specs += [pl.BlockSpec((1, rows, a.shape[-1]), row_map) for a in row_inputs]
    in_specs += [_resident(w.shape) for w in weights]
    return pl.pallas_call(
        kernel,
        grid=(b, length // rows),
        in_specs=in_specs,
        out_specs=[pl.BlockSpec((1, rows, d), row_map) for _ in out_dtypes],
        out_shape=[jax.ShapeDtypeStruct((b, length, d), dt) for dt in out_dtypes],
        scratch_shapes=[pltpu.VMEM((rows, d), F32)],
        compiler_params=_cparams("parallel", "parallel"),
        name=name,
    )(h, *mods, *vec_inputs, *row_inputs, *weights)


def _disc_kernel(are_ref, aim_ref, ldt_ref, bre_ref, bim_ref, oar_ref, oai_ref, obr_ref, obi_ref):
    a_re, a_im = are_ref[0], aim_ref[0]
    dt = jnp.exp(ldt_ref[0])
    mag = jnp.exp(a_re * dt)
    abar_re, abar_im = mag * jnp.cos(a_im * dt), mag * jnp.sin(a_im * dt)
    nr, ni = abar_re - 1.0, abar_im
    den = a_re * a_re + a_im * a_im
    f_re = (nr * a_re + ni * a_im) / den
    f_im = (ni * a_re - nr * a_im) / den
    oar_ref[0] = abar_re
    oai_ref[0] = abar_im
    obr_ref[0] = f_re[None] * bre_ref[0] - f_im[None] * bim_ref[0]
    obi_ref[0] = f_re[None] * bim_ref[0] + f_im[None] * bre_ref[0]


def _discretize(a_re, a_im, log_dt, b_re, b_im):
    nd = a_re.shape[0]
    rows = SSM_WIDTH // LANES
    flat = lambda a: a.reshape(nd, rows, LANES)
    ldt = jnp.broadcast_to(log_dt[:, :, None], (nd, SSM_GROUPS, SSM_STATE))
    chan = lambda a: a.reshape(nd, SSM_WIDTH, SSM_GROUP_W).transpose(0, 2, 1).reshape(nd, SSM_GROUP_W, rows, LANES)
    aspec = pl.BlockSpec((1, rows, LANES), lambda i: (i, 0, 0))
    bspec = pl.BlockSpec((1, SSM_GROUP_W, rows, LANES), lambda i: (i, 0, 0, 0))
    ashape = jax.ShapeDtypeStruct((nd, rows, LANES), F32)
    bshape = jax.ShapeDtypeStruct((nd, SSM_GROUP_W, rows, LANES), F32)
    return pl.pallas_call(
        _disc_kernel,
        grid=(nd,),
        in_specs=[aspec, aspec, aspec, bspec, bspec],
        out_specs=[aspec, aspec, bspec, bspec],
        out_shape=[ashape, ashape, bshape, bshape],
        compiler_params=_cparams("parallel"),
        name="s5_discretize",
    )(flat(a_re), flat(a_im), flat(ldt), chan(b_re), chan(b_im))


def _s5_kernel(uc_ref, ul_ref, wd_ref, wr_ref, are_ref, aim_ref, y_ref,
               xa_re, xa_im, xb_re, xb_im, hre, him, *, reverse):
    step_idx = pl.program_id(1)
    steps = S5_CHUNK
    cols_per_slab = S5_SLAB_STATES // LANES
    seg = steps // S5_SLABS

    @pl.when(step_idx == 0)
    def _():
        hre[...] = jnp.zeros_like(hre)
        him[...] = jnp.zeros_like(him)

    def plane_of(col):
        return col // SUBLANES, col % SUBLANES

    def drive_slab(u, kk, xre, xim):
        bu = jnp.dot(u(kk * LANES, (kk + 1) * LANES), wd_ref[kk], preferred_element_type=F32)
        for c in range(cols_per_slab):
            pln, sub = plane_of(kk * cols_per_slab + c)
            xre[pln, pl.ds(sub, steps, stride=SUBLANES), :] = bu[:, c * LANES:(c + 1) * LANES]
            xim[pln, pl.ds(sub, steps, stride=SUBLANES), :] = (
                bu[:, S5_SLAB_STATES + c * LANES:S5_SLAB_STATES + (c + 1) * LANES])

    def read_slab(kk, xre, xim):
        parts = []
        for src in (xre, xim):
            for c in range(cols_per_slab):
                pln, sub = plane_of(kk * cols_per_slab + c)
                parts.append(src[pln, pl.ds(sub, steps, stride=SUBLANES), :].astype(BF16))
        return jnp.dot(jnp.concatenate(parts, axis=1), wr_ref[kk], preferred_element_type=F32)

    a_r = [are_ref[p] for p in range(S5_PLANES)]
    a_i = [aim_ref[p] for p in range(S5_PLANES)]

    def one_step(t, h, xre, xim):
        h_r, h_i = h
        r0 = t * SUBLANES if isinstance(t, int) else pl.multiple_of(t * SUBLANES, SUBLANES)
        new_r, new_i = [], []
        for p in range(S5_PLANES):
            n_r = a_r[p] * h_r[p] - a_i[p] * h_i[p] + xre[p, pl.ds(r0, SUBLANES), :]
            n_i = a_r[p] * h_i[p] + a_i[p] * h_r[p] + xim[p, pl.ds(r0, SUBLANES), :]
            xre[p, pl.ds(r0, SUBLANES), :] = n_r
            xim[p, pl.ds(r0, SUBLANES), :] = n_i
            new_r.append(n_r)
            new_i.append(n_i)
        return tuple(new_r), tuple(new_i)

    def time_of(i):
        return (steps - 1 - i) if reverse else i

    def scan_segment(k, h, xre, xim):
        for i in range(k * seg, (k + 1) * seg):
            h = one_step(time_of(i), h, xre, xim)
        return h

    def load_state():
        return (tuple(hre[p] for p in range(S5_PLANES)), tuple(him[p] for p in range(S5_PLANES)))

    def store_state(h):
        for p in range(S5_PLANES):
            hre[p] = h[0][p]
            him[p] = h[1][p]

    @pl.when(step_idx == 0)
    def _():
        u = lambda lo, hi: uc_ref[0, :, lo:hi]
        for kk in range(S5_SLABS):
            drive_slab(u, kk, xa_re, xa_im)
        h = lax.fori_loop(0, steps, lambda i, h: one_step(time_of(i), h, xa_re, xa_im), load_state(), unroll=8)
        store_state(h)

    @pl.when(step_idx > 0)
    def _():
        first, second = slice(0, steps), slice(steps, 2 * steps)
        rows_a, rows_b = (second, first) if reverse else (first, second)
        u_a = lambda lo, hi: ul_ref[0, rows_a, lo:hi]
        u_b = lambda lo, hi: ul_ref[0, rows_b, lo:hi]
        for kk in range(S5_SLABS):
            drive_slab(u_a, kk, xa_re, xa_im)
        h = load_state()
        for kk in range(S5_SLABS):
            drive_slab(u_b, kk, xb_re, xb_im)
            h = scan_segment(kk, h, xa_re, xa_im)
        for kk in range(S5_SLABS):
            h = scan_segment(kk, h, xb_re, xb_im)
            y_ref[0, rows_a, kk * LANES:(kk + 1) * LANES] = read_slab(kk, xa_re, xa_im)
        store_state(h)
        for kk in range(S5_SLABS):
            y_ref[0, rows_b, kk * LANES:(kk + 1) * LANES] = read_slab(kk, xb_re, xb_im)


def _s5_direction(u_ctx, u_lat, w_drive, w_read, a_re, a_im, *, reverse):
    b, length, d = u_lat.shape
    n_ctx = u_ctx.shape[1]
    pair = 2 * S5_CHUNK
    assert n_ctx == S5_CHUNK and length % pair == 0
    n_pairs = length // pair
    if reverse:
        lat_map = lambda i, n: (i, jnp.minimum(n_pairs - n, n_pairs - 1), 0)
    else:
        lat_map = lambda i, n: (i, jnp.maximum(n - 1, 0), 0)
    whole = lambda shape: pl.BlockSpec(shape, lambda i, n: (0,) * len(shape))
    plane_buf = pltpu.VMEM((S5_PLANES, S5_CHUNK * SUBLANES, LANES), F32)
    state_buf = pltpu.VMEM((S5_PLANES, SUBLANES, LANES), F32)
    return pl.pallas_call(
        functools.partial(_s5_kernel, reverse=reverse),
        grid=(b, n_pairs + 1),
        in_specs=[
            pl.BlockSpec((1, n_ctx, d), lambda i, n: (i, 0, 0)),
            pl.BlockSpec((1, pair, d), lat_map),
            whole(w_drive.shape), whole(w_read.shape), whole(a_re.shape), whole(a_im.shape),
        ],
        out_specs=pl.BlockSpec((1, pair, d), lat_map),
        out_shape=jax.ShapeDtypeStruct((b, length, d), F32),
        scratch_shapes=[plane_buf, plane_buf, plane_buf, plane_buf, state_buf, state_buf],
        compiler_params=_cparams("parallel", "arbitrary"),
        name="s5_scan_bwd" if reverse else "s5_scan_fwd",
    )(u_ctx, u_lat, w_drive, w_read, a_re, a_im)


def _rope_tables(length):
    rows = length // GRID_W
    t_row = jnp.repeat(jnp.arange(rows, dtype=F32), GRID_W)
    t_col = jnp.tile(jnp.arange(GRID_W, dtype=F32), rows)
    n_freq = HEAD_DIM // 4
    inv_freq = ROPE_BASE ** (-jnp.arange(n_freq, dtype=F32) / n_freq)
    ang = jnp.concatenate([t_row[:, None] * inv_freq, t_col[:, None] * inv_freq], axis=-1)
    cos, sin = jnp.cos(ang), jnp.sin(ang)
    reps = LANES // HEAD_DIM
    return (jnp.tile(jnp.concatenate([cos, cos], axis=-1), (1, reps)),
            jnp.tile(jnp.concatenate([-sin, sin], axis=-1), (1, reps)))


def _half_split_perm():
    perm = np.arange(MIX_IN_WIDTH)
    head = np.concatenate([np.arange(0, HEAD_DIM, 2), np.arange(1, HEAD_DIM, 2)])
    for start in range(FOURIER_WIDTH, KV_START + KV_WIDTH, HEAD_DIM):
        perm[start:start + HEAD_DIM] = start + head
    return perm


def _channel_dft():
    c = np.arange(FOURIER_GROUP_W)
    ang = 2.0 * np.pi * np.outer(c, c) / FOURIER_GROUP_W
    eye = np.eye(FOURIER_GROUPS)
    scale = FOURIER_GROUP_W ** -0.5
    table = np.concatenate([np.kron(eye, np.cos(ang) * scale), np.kron(eye, np.sin(ang) * scale)], axis=1)
    return jnp.asarray(table, F32).astype(BF16)


def _time_dft_matrices(t, rows=None):
    r1 = 1 << ((t.bit_length() - 1 + 1) // 2)
    r2 = t // r1
    assert r1 * r2 == t and r2 % SUBLANES == 0
    rows = t if rows is None else rows
    assert rows % r2 == 0
    n = jnp.arange(t, dtype=jnp.int32)[None, :]
    unit = 2.0 * math.pi / t
    ang_a = (((jnp.arange(rows // r2, dtype=jnp.int32) * r2)[:, None] * n) % t).astype(F32) * unit
    ang_b = ((jnp.arange(r2, dtype=jnp.int32)[:, None] * n) % t).astype(F32) * unit
    scale = t ** -0.5
    ca, sa = jnp.cos(ang_a)[:, None, :], jnp.sin(ang_a)[:, None, :]
    cb, sb = (jnp.cos(ang_b) * scale)[None, :, :], (jnp.sin(ang_b) * scale)[None, :, :]
    cmat = (ca * cb - sa * sb).reshape(rows, t).astype(BF16)
    smat = (-(sa * cb + ca * sb)).reshape(rows, t).astype(BF16)
    return cmat, smat


def _ffn_weights(w_gate, w_up, w_down):
    return w_gate.astype(BF16), w_up.astype(BF16), w_down.astype(BF16)


def _s5_weights(bbar_re, bbar_im, c_re, c_im):
    nd = bbar_re.shape[0]
    eye = jnp.eye(S5_SLAB_GROUPS, dtype=F32)

    def drive(bb):
        bb = bb.reshape(nd, SSM_GROUP_W, S5_SLABS, S5_SLAB_GROUPS, SSM_STATE)
        w = jnp.einsum("dhkgp,gG->dkghGp", bb, eye)
        return w.reshape(nd, S5_SLABS, LANES, S5_SLAB_STATES)

    def read(cc):
        cc = cc.astype(F32).reshape(nd, S5_SLABS, S5_SLAB_GROUPS, SSM_GROUP_W, SSM_STATE)
        w = jnp.einsum("dkghp,gG->dkgpGh", cc, eye)
        return w.reshape(nd, S5_SLABS, S5_SLAB_STATES, LANES)

    w_drive = jnp.concatenate([drive(bbar_re), drive(bbar_im)], axis=-1).astype(BF16)
    w_read = jnp.concatenate([read(c_re), -read(c_im)], axis=2).astype(BF16)
    return w_drive, w_read


def kernel(x, c, ctx, c_ctx, mod_w, mod_b, norm_g, ffn_w_gate, ffn_w_up, ffn_w_down, mix_w_in, mix_w_out,
           attn_sink, ssm_a_re, ssm_a_im, ssm_log_dt, ssm_b_re, ssm_b_im, ssm_c_re, ssm_c_im, ssm_d,
           ssm_glu_w, final_g):
    bn, length, d = x.shape
    n_ctx = ctx.shape[1]
    depth = mod_w.shape[0]
    assert depth == 2 and d == D_MODEL

    cond = jnp.zeros((SUBLANES, d), F32).at[:bn].set(c).at[bn].set(c_ctx)
    mods = _modulations(cond, mod_w, mod_b).reshape(depth, SUBLANES, N_MOD, d)
    mods_lat, mods_ctx = mods[:, :bn], mods[:, bn:bn + 1]
    vec = lambda a: a.reshape(1, d)

    w_in = mix_w_in[0][:, _half_split_perm()].astype(BF16)
    w_out = mix_w_out[0].astype(BF16)
    bcs = _channel_dft()
    cos_t, sin_t = _rope_tables(length)
    ffn0 = _ffn_weights(ffn_w_gate[0], ffn_w_up[0], ffn_w_down[0])
    g_mix, g_ffn = vec(norm_g[0, 0]), vec(norm_g[0, 1])

    uv, q, k, v = _pre0(x, mods_lat[0], g_mix, w_in, bcs, cos_t, sin_t, rope=True, rows=TOKEN_TILE)
    uvc, qc, kc, vc = _pre0(ctx, mods_ctx[0], g_mix, w_in, bcs, cos_t, sin_t, rope=False, rows=n_ctx)
    four = _time_dft_symmetric(*_time_dft_matrices(length, length // 2 + BLOCK), uv)
    four_c = _time_dft(*_time_dft_matrices(n_ctx), uvc)
    att = _window_attention(attn_sink[0], q, k, v, kc, vc)
    att_c = _context_attention(attn_sink[0], qc, kc, vc)
    g_mix1 = vec(norm_g[1, 0])
    h, u_lat = _post_call(_post0_kernel, "mix_out_ffn", x, [mods_lat[0], mods_lat[1]], [four, att],
                          [g_ffn, g_mix1], [w_out, *ffn0], TOKEN_TILE, [F32, BF16])
    _, u_ctx = _post_call(_post0_kernel, "mix_out_ffn_ctx", ctx, [mods_ctx[0], mods_ctx[1]], [four_c, att_c],
                          [g_ffn, g_mix1], [w_out, *ffn0], n_ctx, [F32, BF16])

    g_mix, g_ffn = g_mix1, vec(norm_g[1, 1])
    abar_re, abar_im, bbar_re, bbar_im = _discretize(
        ssm_a_re[0], ssm_a_im[0], ssm_log_dt[0], ssm_b_re[0], ssm_b_im[0])
    w_drive, w_read = _s5_weights(bbar_re, bbar_im, ssm_c_re[0], ssm_c_im[0])
    planes = lambda a: a.reshape(2, S5_PLANES, SUBLANES, LANES)
    abar_re, abar_im = planes(abar_re), planes(abar_im)
    y_dir = [_s5_direction(u_ctx, u_lat, w_drive[i], w_read[i], abar_re[i], abar_im[i], reverse=bool(i))
             for i in range(2)]
    ffn1 = _ffn_weights(ffn_w_gate[1], ffn_w_up[1], ffn_w_down[1])
    return _post_call(_post1_kernel, "s5_glu_ffn_norm", h, [mods_lat[1]], y_dir,
                      [g_mix, g_ffn, vec(final_g), vec(ssm_d[0])],
                      [ssm_glu_w[0].astype(BF16), *ffn1], TOKEN_TILE, [F32])[0]
```

```python
import functools
import math

import numpy as np
import jax
import jax.numpy as jnp
from jax import lax
from jax.experimental import pallas as pl
from jax.experimental.pallas import tpu as pltpu

F32 = jnp.float32
BF16 = jnp.bfloat16

D_MODEL = 1024
GRID_W = 64
N_MOD = 6
N_Q_HEADS = 8
N_KV_HEADS = 2
HEAD_DIM = 64
ATTN_WIDTH = N_Q_HEADS * HEAD_DIM
KV_WIDTH = N_KV_HEADS * HEAD_DIM
FOURIER_GROUPS = 8
FOURIER_GROUP_W = 64
FOURIER_WIDTH = FOURIER_GROUPS * FOURIER_GROUP_W
KV_START = FOURIER_WIDTH + ATTN_WIDTH
MIX_IN_WIDTH = KV_START + 2 * KV_WIDTH
WINDOW = 128
BLOCK = 128
ATTN_SCALE = HEAD_DIM ** -0.5
LOG2_E = math.log2(math.e)
ROPE_BASE = 10000.0
NEG_INF = -1e30
SSM_GROUP_W = 16
SSM_GROUPS = D_MODEL // SSM_GROUP_W
SSM_STATE = 64
SSM_WIDTH = SSM_GROUPS * SSM_STATE
RMS_EPS = 1e-6

LANES = 128
SUBLANES = 8
KV_SPLIT_WIDTH = 2 * N_KV_HEADS * LANES
VMEM_LIMIT_BYTES = 56 * 1024 * 1024

TOKEN_TILE = 512
FFN_CHUNK = 256
MOD_TILE = 1536
DFT_ROW_TILE = 1024
DFT_K_TILE = 4096
S5_CHUNK = 256
S5_SLAB_GROUPS = LANES // SSM_GROUP_W
S5_SLABS = SSM_GROUPS // S5_SLAB_GROUPS
S5_SLAB_STATES = S5_SLAB_GROUPS * SSM_STATE
S5_COLS = SSM_WIDTH // LANES
S5_PLANES = S5_COLS // SUBLANES


def _cparams(*sem):
    return pltpu.CompilerParams(dimension_semantics=sem, vmem_limit_bytes=VMEM_LIMIT_BYTES)


def _sigmoid(x):
    return 1.0 / (1.0 + jnp.exp(-x))


def _norm_mod(x, g, scale, shift):
    y = x * lax.rsqrt(jnp.mean(x * x, axis=-1, keepdims=True) + RMS_EPS)
    return (y * g) * (1.0 + scale) + shift


def _split_bf16(x):
    hi = x.astype(BF16)
    lo = (x - hi.astype(F32)).astype(BF16)
    return hi, lo


def _mod_kernel(cond_ref, w_ref, b_ref, o_ref):
    x = cond_ref[...]
    s_hi, s_lo = _split_bf16(x * _sigmoid(x))
    w_hi, w_lo = _split_bf16(w_ref[0])
    acc = jnp.dot(s_hi, w_hi, preferred_element_type=F32)
    acc += jnp.dot(s_lo, w_hi, preferred_element_type=F32)
    acc += jnp.dot(s_hi, w_lo, preferred_element_type=F32)
    o_ref[0] = acc + b_ref[0]


def _modulations(cond, mod_w, mod_b):
    depth, d, n = mod_w.shape
    rows = cond.shape[0]
    return pl.pallas_call(
        _mod_kernel,
        grid=(depth, n // MOD_TILE),
        in_specs=[
            pl.BlockSpec((rows, d), lambda l, j: (0, 0)),
            pl.BlockSpec((1, d, MOD_TILE), lambda l, j: (l, 0, j)),
            pl.BlockSpec((1, 1, MOD_TILE), lambda l, j: (l, 0, j)),
        ],
        out_specs=pl.BlockSpec((1, rows, MOD_TILE), lambda l, j: (l, 0, j)),
        out_shape=jax.ShapeDtypeStruct((depth, rows, n), F32),
        compiler_params=_cparams("parallel", "parallel"),
        name="adaln_mod",
    )(cond, mod_w, mod_b.reshape(depth, 1, n))


def _pre0_kernel(x_ref, mod_ref, g_ref, w_ref, bcs_ref, cos_ref, sin_ref,
                 uv_ref, q_ref, k_ref, v_ref, *, rope):
    m = mod_ref[0]
    n = _norm_mod(x_ref[0], g_ref[...], m[1:2], m[0:1])
    p = jnp.dot(n.astype(BF16), w_ref[...], preferred_element_type=F32)
    uv = jnp.dot(p[:, :FOURIER_WIDTH].astype(BF16), bcs_ref[...], preferred_element_type=F32)
    uv_ref[0, 0] = uv[:, :FOURIER_WIDTH].astype(BF16)
    uv_ref[0, 1] = uv[:, FOURIER_WIDTH:].astype(BF16)

    def rot(x):
        if not rope:
            return x
        lane = lax.broadcasted_iota(jnp.int32, x.shape, 1)
        first = (lane & (HEAD_DIM // 2)) == 0
        partner = jnp.where(first, pltpu.roll(x, LANES - HEAD_DIM // 2, 1),
                            pltpu.roll(x, HEAD_DIM // 2, 1))
        return x * cos_ref[...] + partner * sin_ref[...]

    for j in range(ATTN_WIDTH // LANES):
        lo = FOURIER_WIDTH + j * LANES
        q_ref[0, :, j * LANES:(j + 1) * LANES] = (rot(p[:, lo:lo + LANES]) * (ATTN_SCALE * LOG2_E)).astype(BF16)

    def split_heads(x):
        low = lax.broadcasted_iota(jnp.int32, x.shape, 1) < HEAD_DIM
        swap = pltpu.roll(x, HEAD_DIM, 1)
        parts = (jnp.where(low, x, 0.0), jnp.where(low, swap, 0.0), jnp.where(low, 0.0, swap), jnp.where(low, 0.0, x))
        return jnp.concatenate(parts, axis=1).astype(BF16)

    k_ref[0] = split_heads(rot(p[:, KV_START:KV_START + KV_WIDTH]))
    v_ref[0] = split_heads(p[:, KV_START + KV_WIDTH:])


def _pre0(x, mods, g, w_in, bcs, cos_t, sin_t, *, rope, rows):
    b, length, d = x.shape
    grid = (b, length // rows)
    row_map = lambda i, t: (i, t, 0)
    const2 = lambda i, t: (0, 0)
    return pl.pallas_call(
        functools.partial(_pre0_kernel, rope=rope),
        grid=grid,
        in_specs=[
            pl.BlockSpec((1, rows, d), row_map),
            pl.BlockSpec((1, N_MOD, d), lambda i, t: (i if mods.shape[0] > 1 else 0, 0, 0)),
            pl.BlockSpec((1, d), const2),
            pl.BlockSpec((d, MIX_IN_WIDTH), const2),
            pl.BlockSpec((FOURIER_WIDTH, 2 * FOURIER_WIDTH), const2),
            pl.BlockSpec((rows, LANES), lambda i, t: (t, 0)),
            pl.BlockSpec((rows, LANES), lambda i, t: (t, 0)),
        ],
        out_specs=[
            pl.BlockSpec((1, 2, rows, FOURIER_WIDTH), lambda i, t: (i, 0, t, 0)),
            pl.BlockSpec((1, rows, ATTN_WIDTH), row_map),
            pl.BlockSpec((1, rows, KV_SPLIT_WIDTH), row_map),
            pl.BlockSpec((1, rows, KV_SPLIT_WIDTH), row_map),
        ],
        out_shape=[
            jax.ShapeDtypeStruct((b, 2, length, FOURIER_WIDTH), BF16),
            jax.ShapeDtypeStruct((b, length, ATTN_WIDTH), BF16),
            jax.ShapeDtypeStruct((b, length, KV_SPLIT_WIDTH), BF16),
            jax.ShapeDtypeStruct((b, length, KV_SPLIT_WIDTH), BF16),
        ],
        compiler_params=_cparams("parallel", "parallel"),
        name="mix_in_rope" if rope else "mix_in_ctx",
    )(x, mods, g, w_in, bcs, cos_t, sin_t)


def _dft_kernel(c_ref, s_ref, u_ref, v_ref, o_ref, acc_ref):
    k = pl.program_id(2)

    @pl.when(k == 0)
    def _():
        acc_ref[...] = jnp.zeros_like(acc_ref)

    acc_ref[...] += (jnp.dot(c_ref[...], u_ref[0, 0], preferred_element_type=F32)
                     + jnp.dot(s_ref[...], v_ref[0, 0], preferred_element_type=F32))

    @pl.when(k == pl.num_programs(2) - 1)
    def _():
        o_ref[0] = acc_ref[...].astype(BF16)


def _time_dft(cmat, smat, uv):
    b, _, t, w = uv.shape
    tm = min(DFT_ROW_TILE, t)
    tk = min(DFT_K_TILE, t)
    mat_spec = pl.BlockSpec((tm, tk), lambda i, r, k: (r, k))
    return pl.pallas_call(
        _dft_kernel,
        grid=(b, t // tm, t // tk),
        in_specs=[
            mat_spec, mat_spec,
            pl.BlockSpec((1, 1, tk, w), lambda i, r, k: (i, 0, k, 0)),
            pl.BlockSpec((1, 1, tk, w), lambda i, r, k: (i, 1, k, 0)),
        ],
        out_specs=pl.BlockSpec((1, tm, w), lambda i, r, k: (i, r, 0)),
        out_shape=jax.ShapeDtypeStruct((b, t, w), BF16),
        scratch_shapes=[pltpu.VMEM((tm, w), F32)],
        compiler_params=_cparams("parallel", "parallel", "arbitrary"),
        name="time_dft",
    )(cmat, smat, uv, uv)


def _dft_half_kernel(c_ref, s_ref, u_ref, v_ref, lo_ref, mir_ref, p_acc, q_acc):
    k = pl.program_id(2)

    @pl.when(k == 0)
    def _():
        p_acc[...] = jnp.zeros_like(p_acc)
        q_acc[...] = jnp.zeros_like(q_acc)

    p_acc[...] += jnp.dot(c_ref[...], u_ref[0, 0], preferred_element_type=F32)
    q_acc[...] += jnp.dot(s_ref[...], v_ref[0, 0], preferred_element_type=F32)

    @pl.when(k == pl.num_programs(2) - 1)
    def _():
        lo_ref[0] = (p_acc[...] + q_acc[...]).astype(BF16)
        mir_ref[0] = (p_acc[...] - q_acc[...]).astype(BF16)


def _dft_mirror_kernel(lo_ref, mir_ref, o_ref):
    i = pl.program_id(1)
    half = o_ref.shape[1]

    @pl.when(i == 0)
    def _():
        o_ref[0] = lo_ref[0, :half, :]

    @pl.when(i == 1)
    def _():
        r = lax.broadcasted_iota(jnp.int32, (BLOCK, BLOCK), 0)
        c = lax.broadcasted_iota(jnp.int32, (BLOCK, BLOCK), 1)
        rev = jnp.where(c == BLOCK - r, 1.0, 0.0).astype(BF16)
        first = jnp.where((r == 0) & (c == 0), 1.0, 0.0).astype(BF16)
        for a in range(half // BLOCK):
            top = half - a * BLOCK
            z = jnp.dot(rev, mir_ref[0, top - BLOCK:top, :], preferred_element_type=F32)
            z += jnp.dot(first, mir_ref[0, top:top + BLOCK, :], preferred_element_type=F32)
            o_ref[0, a * BLOCK:(a + 1) * BLOCK, :] = z.astype(BF16)


def _time_dft_symmetric(cmat, smat, uv):
    b, _, t, w = uv.shape
    rows = cmat.shape[0]
    assert rows == t // 2 + BLOCK and rows % (2 * SUBLANES) == 0
    tm = rows // 4
    tk = min(DFT_K_TILE, t)
    mat_spec = pl.BlockSpec((tm, tk), lambda i, r, k: (r, k))
    out_spec = pl.BlockSpec((1, tm, w), lambda i, r, k: (i, r, 0))
    lo, mir = pl.pallas_call(
        _dft_half_kernel,
        grid=(b, rows // tm, t // tk),
        in_specs=[
            mat_spec, mat_spec,
            pl.BlockSpec((1, 1, tk, w), lambda i, r, k: (i, 0, k, 0)),
            pl.BlockSpec((1, 1, tk, w), lambda i, r, k: (i, 1, k, 0)),
        ],
        out_specs=[out_spec, out_spec],
        out_shape=[jax.ShapeDtypeStruct((b, rows, w), BF16)] * 2,
        scratch_shapes=[pltpu.VMEM((tm, w), F32), pltpu.VMEM((tm, w), F32)],
        compiler_params=_cparams("parallel", "parallel", "arbitrary"),
        name="time_dft_half",
    )(cmat, smat, uv, uv)
    whole = pl.BlockSpec((1, rows, w), lambda i, n: (i, 0, 0))
    return pl.pallas_call(
        _dft_mirror_kernel,
        grid=(b, 2),
        in_specs=[whole, whole],
        out_specs=pl.BlockSpec((1, t // 2, w), lambda i, n: (i, n, 0)),
        out_shape=jax.ShapeDtypeStruct((b, t, w), BF16),
        compiler_params=_cparams("parallel", "parallel"),
        name="time_dft_mirror",
    )(lo, mir)


def _attn_kernel(sink_ref, q_ref, *refs, window):
    if window:
        kp_ref, km_ref, kn_ref, vp_ref, vm_ref, vn_ref, kx_ref, vx_ref, o_ref, s_scr, p_scr = refs
        k_pieces = [kx_ref[0], kp_ref[0], km_ref[0, :BLOCK, :], km_ref[0, BLOCK:, :], kn_ref[0]]
        v_pieces = [vx_ref[0], vp_ref[0], vm_ref[0, :BLOCK, :], vm_ref[0, BLOCK:, :], vn_ref[0]]
        subs = ((0, 1, 2, 3), (0, 2, 3, 4))
    else:
        kx_ref, vx_ref, o_ref, s_scr, p_scr = refs
        k_pieces, v_pieces = [kx_ref[0]], [vx_ref[0]]
        subs = ((0,),)
    n_sub = len(subs)
    n_q = q_ref.shape[1] // n_sub

    cache = {}

    def operand(name, pieces, sub, half, g):
        key = (name, sub, half, g)
        if key not in cache:
            lo = (2 * half + g) * LANES
            parts = [pieces[i][:, lo:lo + LANES] for i in subs[sub]]
            cache[key] = parts[0] if len(parts) == 1 else jnp.concatenate(parts, axis=0)
        return cache[key]

    if window:
        step = pl.program_id(1)
        last = pl.num_programs(1) - 1
        n_ctx = kx_ref.shape[1]
        row = lax.broadcasted_iota(jnp.int32, (n_q, BLOCK), 0)
        col = lax.broadcasted_iota(jnp.int32, (n_q, BLOCK), 1)
        ok_prev = (col >= row + jnp.where(step > 0, 0, BLOCK), col >= row)
        ok_next = (col <= row, col <= row - jnp.where(step < last, 0, BLOCK))

    q_per_kv = N_Q_HEADS // N_KV_HEADS
    n_pairs = ATTN_WIDTH // LANES
    for sub in range(n_sub):
        rows = slice(sub * n_q, (sub + 1) * n_q)
        for j in range(n_pairs):
            g = (2 * j) // q_per_kv
            qp = q_ref[0, rows, j * LANES:(j + 1) * LANES]
            for half in range(2):
                s = lax.dot_general(qp, operand("k", k_pieces, sub, half, g), (((1,), (1,)), ((), ())),
                                    preferred_element_type=F32)
                item = sub * N_Q_HEADS + 2 * j + half
                if window:
                    s_scr[item, :, :n_ctx] = s[:, :n_ctx]
                    s_scr[item, :, n_ctx:n_ctx + BLOCK] = jnp.where(
                        ok_prev[sub], s[:, n_ctx:n_ctx + BLOCK], NEG_INF)
                    s_scr[item, :, n_ctx + BLOCK:n_ctx + 2 * BLOCK] = s[:, n_ctx + BLOCK:n_ctx + 2 * BLOCK]
                    s_scr[item, :, n_ctx + 2 * BLOCK:] = jnp.where(
                        ok_next[sub], s[:, n_ctx + 2 * BLOCK:], NEG_INF)
                else:
                    s_scr[item] = s
    inv = []
    n_keys = s_scr.shape[2]
    col_blocks = [slice(c, c + LANES) for c in range(0, n_keys, LANES)]
    for item in range(n_sub * N_Q_HEADS):
        sink = sink_ref[item % N_Q_HEADS] * LOG2_E
        mx = s_scr[item, :, col_blocks[0]]
        for cb in col_blocks[1:]:
            mx = jnp.maximum(mx, s_scr[item, :, cb])
        mx = jnp.maximum(jnp.max(mx, axis=1, keepdims=True), sink)
        total = None
        for cb in col_blocks:
            p = jnp.exp2(s_scr[item, :, cb] - mx)
            total = p if total is None else total + p
            p_scr[item, :, cb] = p.astype(BF16)
        inv.append(1.0 / (jnp.sum(total, axis=1, keepdims=True) + jnp.exp2(sink - mx)))
    for sub in range(n_sub):
        rows = slice(sub * n_q, (sub + 1) * n_q)
        for j in range(n_pairs):
            g = (2 * j) // q_per_kv
            a = sub * N_Q_HEADS + 2 * j
            out = jnp.dot(p_scr[a], operand("v", v_pieces, sub, 0, g), preferred_element_type=F32) * inv[a]
            out += jnp.dot(p_scr[a + 1], operand("v", v_pieces, sub, 1, g), preferred_element_type=F32) * inv[a + 1]
            o_ref[0, rows, j * LANES:(j + 1) * LANES] = out.astype(BF16)


def _attn_scratch(n_items, n_q, n_keys):
    return [pltpu.VMEM((n_items, n_q, n_keys), F32), pltpu.VMEM((n_items, n_q, n_keys), BF16)]


def _window_attention(sink, q, k, v, kx, vx):
    b, length, _ = q.shape
    nb = length // BLOCK
    n_ctx = kx.shape[1]
    assert nb % 2 == 0
    one = lambda f: pl.BlockSpec((1, BLOCK, KV_SPLIT_WIDTH), f)
    two = pl.BlockSpec((1, 2 * BLOCK, KV_SPLIT_WIDTH), lambda i, m: (i, m, 0))
    prev = lambda i, m: (i, jnp.maximum(2 * m - 1, 0), 0)
    nxt = lambda i, m: (i, jnp.minimum(2 * m + 2, nb - 1), 0)
    ctx_spec = pl.BlockSpec((1, n_ctx, KV_SPLIT_WIDTH), lambda i, m: (i, 0, 0))
    q_spec = pl.BlockSpec((1, 2 * BLOCK, ATTN_WIDTH), lambda i, m: (i, m, 0))
    return pl.pallas_call(
        functools.partial(_attn_kernel, window=True),
        grid=(b, nb // 2),
        in_specs=[
            pl.BlockSpec(memory_space=pltpu.SMEM),
            q_spec,
            one(prev), two, one(nxt),
            one(prev), two, one(nxt),
            ctx_spec, ctx_spec,
        ],
        out_specs=q_spec,
        out_shape=jax.ShapeDtypeStruct((b, length, ATTN_WIDTH), BF16),
        scratch_shapes=_attn_scratch(2 * N_Q_HEADS, BLOCK, n_ctx + 3 * BLOCK),
        compiler_params=_cparams("parallel", "parallel"),
        name="window_attn",
    )(sink, q, k, k, k, v, v, v, kx, vx)


def _context_attention(sink, q, kx, vx):
    b, n_ctx, _ = q.shape
    spec = lambda w: pl.BlockSpec((1, n_ctx, w), lambda i: (i, 0, 0))
    return pl.pallas_call(
        functools.partial(_attn_kernel, window=False),
        grid=(b,),
        in_specs=[pl.BlockSpec(memory_space=pltpu.SMEM), spec(ATTN_WIDTH), spec(KV_SPLIT_WIDTH),
                  spec(KV_SPLIT_WIDTH)],
        out_specs=spec(ATTN_WIDTH),
        out_shape=jax.ShapeDtypeStruct((b, n_ctx, ATTN_WIDTH), BF16),
        scratch_shapes=_attn_scratch(N_Q_HEADS, n_ctx, n_ctx),
        compiler_params=_cparams("parallel"),
        name="context_attn",
    )(sink, q, kx, vx)


def _swiglu_residual(h1, m, g2, wg_ref, wu_ref, wd_ref, acc_ref):
    n2 = _norm_mod(h1, g2, m[4:5], m[3:4]).astype(BF16)
    for c in range(wg_ref.shape[1] // FFN_CHUNK):
        cols = slice(c * FFN_CHUNK, (c + 1) * FFN_CHUNK)
        gate = jnp.dot(n2, wg_ref[:, cols], preferred_element_type=F32)
        up = jnp.dot(n2, wu_ref[:, cols], preferred_element_type=F32)
        act = (gate * _sigmoid(gate)) * up
        part = jnp.dot(act.astype(BF16), wd_ref[cols, :], preferred_element_type=F32)
        if c == 0:
            acc_ref[...] = part
        else:
            acc_ref[...] += part
    return h1 + m[5:6] * acc_ref[...]


def _post0_kernel(h_ref, mod_ref, nmod_ref, g2_ref, gn_ref, f_ref, a_ref, wo_ref, wg_ref, wu_ref, wd_ref,
                  o_ref, u_ref, acc_ref):
    m = mod_ref[0]
    mix = jnp.dot(f_ref[0], wo_ref[:FOURIER_WIDTH, :], preferred_element_type=F32)
    mix += jnp.dot(a_ref[0], wo_ref[FOURIER_WIDTH:, :], preferred_element_type=F32)
    h1 = h_ref[0] + m[2:3] * mix
    h2 = _swiglu_residual(h1, m, g2_ref[...], wg_ref, wu_ref, wd_ref, acc_ref)
    o_ref[0] = h2
    nm = nmod_ref[0]
    u_ref[0] = _norm_mod(h2, gn_ref[...], nm[1:2], nm[0:1]).astype(BF16)


def _gelu_tanh(x):
    sqrt_2_over_pi = np.sqrt(2 / np.pi).astype(np.float32)
    cdf = 0.5 * (1.0 + jnp.tanh(sqrt_2_over_pi * (x + 0.044715 * (x ** 3))))
    return x * cdf


def _post1_kernel(h_ref, mod_ref, g1_ref, g2_ref, gf_ref, dskip_ref, yf_ref, yb_ref, wglu_ref,
                  wg_ref, wu_ref, wd_ref, o_ref, acc_ref):
    m = mod_ref[0]
    h = h_ref[0]
    u = _norm_mod(h, g1_ref[...], m[1:2], m[0:1])
    y = dskip_ref[...] * u + yf_ref[0] + yb_ref[0]
    z = jnp.dot(_gelu_tanh(y).astype(BF16), wglu_ref[...], preferred_element_type=F32)
    h1 = h + m[2:3] * (z[:, :D_MODEL] * _sigmoid(z[:, D_MODEL:]))
    h2 = _swiglu_residual(h1, m, g2_ref[...], wg_ref, wu_ref, wd_ref, acc_ref)
    o_ref[0] = (h2 * lax.rsqrt(jnp.mean(h2 * h2, axis=-1, keepdims=True) + RMS_EPS)) * gf_ref[...]


def _resident(shape):
    zeros = (0,) * len(shape)
    return pl.BlockSpec(shape, lambda i, t: zeros, pipeline_mode=pl.Buffered(1))


def _post_call(kernel, name, h, mods, row_inputs, vec_inputs, weights, rows, out_dtypes):
    b, length, d = h.shape
    row_map = lambda i, t: (i, t, 0)
    mod_spec = lambda m: pl.BlockSpec((1, N_MOD, d), lambda i, t: (i if m.shape[0] > 1 else 0, 0, 0))
    in_specs = [pl.BlockSpec((1, rows, d), row_map)]
    in_specs += [mod_spec(m) for m in mods]
    in_specs += [pl.BlockSpec((1, d), lambda i, t: (0, 0)) for _ in vec_inputs]
    in_specs += [pl.BlockSpec((1, rows, a.shape[-1]), row_map) for a in row_inputs]
    in_specs += [_resident(w.shape) for w in weights]
    return pl.pallas_call(
        kernel,
        grid=(b, length // rows),
        in_specs=in_specs,
        out_specs=[pl.BlockSpec((1, rows, d), row_map) for _ in out_dtypes],
        out_shape=[jax.ShapeDtypeStruct((b, length, d), dt) for dt in out_dtypes],
        scratch_shapes=[pltpu.VMEM((rows, d), F32)],
        compiler_params=_cparams("parallel", "parallel"),
        name=name,
    )(h, *mods, *vec_inputs, *row_inputs, *weights)


def _disc_kernel(are_ref, aim_ref, ldt_ref, bre_ref, bim_ref, oar_ref, oai_ref, obr_ref, obi_ref):
    a_re, a_im = are_ref[0], aim_ref[0]
    dt = jnp.exp(ldt_ref[0])
    mag = jnp.exp(a_re * dt)
    abar_re, abar_im = mag * jnp.cos(a_im * dt), mag * jnp.sin(a_im * dt)
    nr, ni = abar_re - 1.0, abar_im
    den = a_re * a_re + a_im * a_im
    f_re = (nr * a_re + ni * a_im) / den
    f_im = (ni * a_re - nr * a_im) / den
    oar_ref[0] = abar_re
    oai_ref[0] = abar_im
    obr_ref[0] = f_re[None] * bre_ref[0] - f_im[None] * bim_ref[0]
    obi_ref[0] = f_re[None] * bim_ref[0] + f_im[None] * bre_ref[0]


def _discretize(a_re, a_im, log_dt, b_re, b_im):
    nd = a_re.shape[0]
    rows = SSM_WIDTH // LANES
    flat = lambda a: a.reshape(nd, rows, LANES)
    ldt = jnp.broadcast_to(log_dt[:, :, None], (nd, SSM_GROUPS, SSM_STATE))
    chan = lambda a: a.reshape(nd, SSM_WIDTH, SSM_GROUP_W).transpose(0, 2, 1).reshape(nd, SSM_GROUP_W, rows, LANES)
    aspec = pl.BlockSpec((1, rows, LANES), lambda i: (i, 0, 0))
    bspec = pl.BlockSpec((1, SSM_GROUP_W, rows, LANES), lambda i: (i, 0, 0, 0))
    ashape = jax.ShapeDtypeStruct((nd, rows, LANES), F32)
    bshape = jax.ShapeDtypeStruct((nd, SSM_GROUP_W, rows, LANES), F32)
    return pl.pallas_call(
        _disc_kernel,
        grid=(nd,),
        in_specs=[aspec, aspec, aspec, bspec, bspec],
        out_specs=[aspec, aspec, bspec, bspec],
        out_shape=[ashape, ashape, bshape, bshape],
        compiler_params=_cparams("parallel"),
        name="s5_discretize",
    )(flat(a_re), flat(a_im), flat(ldt), chan(b_re), chan(b_im))


def _s5_kernel(uc_ref, ul_ref, wd_ref, wr_ref, are_ref, aim_ref, y_ref,
               xa_re, xa_im, xb_re, xb_im, hre, him, *, reverse):
    step_idx = pl.program_id(1)
    steps = S5_CHUNK
    cols_per_slab = S5_SLAB_STATES // LANES
    seg = steps // S5_SLABS

    @pl.when(step_idx == 0)
    def _():
        hre[...] = jnp.zeros_like(hre)
        him[...] = jnp.zeros_like(him)

    def plane_of(col):
        return col // SUBLANES, col % SUBLANES

    def drive_slab(u, kk, xre, xim):
        bu = jnp.dot(u(kk * LANES, (kk + 1) * LANES), wd_ref[kk], preferred_element_type=F32)
        for c in range(cols_per_slab):
            pln, sub = plane_of(kk * cols_per_slab + c)
            xre[pln, pl.ds(sub, steps, stride=SUBLANES), :] = bu[:, c * LANES:(c + 1) * LANES]
            xim[pln, pl.ds(sub, steps, stride=SUBLANES), :] = (
                bu[:, S5_SLAB_STATES + c * LANES:S5_SLAB_STATES + (c + 1) * LANES])

    def read_slab(kk, xre, xim):
        parts = []
        for src in (xre, xim):
            for c in range(cols_per_slab):
                pln, sub = plane_of(kk * cols_per_slab + c)
                parts.append(src[pln, pl.ds(sub, steps, stride=SUBLANES), :].astype(BF16))
        return jnp.dot(jnp.concatenate(parts, axis=1), wr_ref[kk], preferred_element_type=F32)

    a_r = [are_ref[p] for p in range(S5_PLANES)]
    a_i = [aim_ref[p] for p in range(S5_PLANES)]

    def one_step(t, h, xre, xim):
        h_r, h_i = h
        r0 = t * SUBLANES if isinstance(t, int) else pl.multiple_of(t * SUBLANES, SUBLANES)
        new_r, new_i = [], []
        for p in range(S5_PLANES):
            n_r = a_r[p] * h_r[p] - a_i[p] * h_i[p] + xre[p, pl.ds(r0, SUBLANES), :]
            n_i = a_r[p] * h_i[p] + a_i[p] * h_r[p] + xim[p, pl.ds(r0, SUBLANES), :]
            xre[p, pl.ds(r0, SUBLANES), :] = n_r
            xim[p, pl.ds(r0, SUBLANES), :] = n_i
            new_r.append(n_r)
            new_i.append(n_i)
        return tuple(new_r), tuple(new_i)

    def time_of(i):
        return (steps - 1 - i) if reverse else i

    def scan_segment(k, h, xre, xim):
        for i in range(k * seg, (k + 1) * seg):
            h = one_step(time_of(i), h, xre, xim)
        return h

    def load_state():
        return (tuple(hre[p] for p in range(S5_PLANES)), tuple(him[p] for p in range(S5_PLANES)))

    def store_state(h):
        for p in range(S5_PLANES):
            hre[p] = h[0][p]
            him[p] = h[1][p]

    @pl.when(step_idx == 0)
    def _():
        u = lambda lo, hi: uc_ref[0, :, lo:hi]
        for kk in range(S5_SLABS):
            drive_slab(u, kk, xa_re, xa_im)
        h = lax.fori_loop(0, steps, lambda i, h: one_step(time_of(i), h, xa_re, xa_im), load_state(), unroll=8)
        store_state(h)

    @pl.when(step_idx > 0)
    def _():
        first, second = slice(0, steps), slice(steps, 2 * steps)
        rows_a, rows_b = (second, first) if reverse else (first, second)
        u_a = lambda lo, hi: ul_ref[0, rows_a, lo:hi]
        u_b = lambda lo, hi: ul_ref[0, rows_b, lo:hi]
        for kk in range(S5_SLABS):
            drive_slab(u_a, kk, xa_re, xa_im)
        h = load_state()
        for kk in range(S5_SLABS):
            drive_slab(u_b, kk, xb_re, xb_im)
            h = scan_segment(kk, h, xa_re, xa_im)
        for kk in range(S5_SLABS):
            h = scan_segment(kk, h, xb_re, xb_im)
            y_ref[0, rows_a, kk * LANES:(kk + 1) * LANES] = read_slab(kk, xa_re, xa_im)
        store_state(h)
        for kk in range(S5_SLABS):
            y_ref[0, rows_b, kk * LANES:(kk + 1) * LANES] = read_slab(kk, xb_re, xb_im)


def _s5_direction(u_ctx, u_lat, w_drive, w_read, a_re, a_im, *, reverse):
    b, length, d = u_lat.shape
    n_ctx = u_ctx.shape[1]
    pair = 2 * S5_CHUNK
    assert n_ctx == S5_CHUNK and length % pair == 0
    n_pairs = length // pair
    if reverse:
        lat_map = lambda i, n: (i, jnp.minimum(n_pairs - n, n_pairs - 1), 0)
    else:
        lat_map = lambda i, n: (i, jnp.maximum(n - 1, 0), 0)
    whole = lambda shape: pl.BlockSpec(shape, lambda i, n: (0,) * len(shape))
    plane_buf = pltpu.VMEM((S5_PLANES, S5_CHUNK * SUBLANES, LANES), F32)
    state_buf = pltpu.VMEM((S5_PLANES, SUBLANES, LANES), F32)
    return pl.pallas_call(
        functools.partial(_s5_kernel, reverse=reverse),
        grid=(b, n_pairs + 1),
        in_specs=[
            pl.BlockSpec((1, n_ctx, d), lambda i, n: (i, 0, 0)),
            pl.BlockSpec((1, pair, d), lat_map),
            whole(w_drive.shape), whole(w_read.shape), whole(a_re.shape), whole(a_im.shape),
        ],
        out_specs=pl.BlockSpec((1, pair, d), lat_map),
        out_shape=jax.ShapeDtypeStruct((b, length, d), F32),
        scratch_shapes=[plane_buf, plane_buf, plane_buf, plane_buf, state_buf, state_buf],
        compiler_params=_cparams("parallel", "arbitrary"),
        name="s5_scan_bwd" if reverse else "s5_scan_fwd",
    )(u_ctx, u_lat, w_drive, w_read, a_re, a_im)


def _rope_tables(length):
    rows = length // GRID_W
    t_row = jnp.repeat(jnp.arange(rows, dtype=F32), GRID_W)
    t_col = jnp.tile(jnp.arange(GRID_W, dtype=F32), rows)
    n_freq = HEAD_DIM // 4
    inv_freq = ROPE_BASE ** (-jnp.arange(n_freq, dtype=F32) / n_freq)
    ang = jnp.concatenate([t_row[:, None] * inv_freq, t_col[:, None] * inv_freq], axis=-1)
    cos, sin = jnp.cos(ang), jnp.sin(ang)
    reps = LANES // HEAD_DIM
    return (jnp.tile(jnp.concatenate([cos, cos], axis=-1), (1, reps)),
            jnp.tile(jnp.concatenate([-sin, sin], axis=-1), (1, reps)))


def _half_split_perm():
    perm = np.arange(MIX_IN_WIDTH)
    head = np.concatenate([np.arange(0, HEAD_DIM, 2), np.arange(1, HEAD_DIM, 2)])
    for start in range(FOURIER_WIDTH, KV_START + KV_WIDTH, HEAD_DIM):
        perm[start:start + HEAD_DIM] = start + head
    return perm


def _channel_dft():
    c = np.arange(FOURIER_GROUP_W)
    ang = 2.0 * np.pi * np.outer(c, c) / FOURIER_GROUP_W
    eye = np.eye(FOURIER_GROUPS)
    scale = FOURIER_GROUP_W ** -0.5
    table = np.concatenate([np.kron(eye, np.cos(ang) * scale), np.kron(eye, np.sin(ang) * scale)], axis=1)
    return jnp.asarray(table, F32).astype(BF16)


def _time_dft_matrices(t, rows=None):
    r1 = 1 << ((t.bit_length() - 1 + 1) // 2)
    r2 = t // r1
    assert r1 * r2 == t and r2 % SUBLANES == 0
    rows = t if rows is None else rows
    assert rows % r2 == 0
    n = jnp.arange(t, dtype=jnp.int32)[None, :]
    unit = 2.0 * math.pi / t
    ang_a = (((jnp.arange(rows // r2, dtype=jnp.int32) * r2)[:, None] * n) % t).astype(F32) * unit
    ang_b = ((jnp.arange(r2, dtype=jnp.int32)[:, None] * n) % t).astype(F32) * unit
    scale = t ** -0.5
    ca, sa = jnp.cos(ang_a)[:, None, :], jnp.sin(ang_a)[:, None, :]
    cb, sb = (jnp.cos(ang_b) * scale)[None, :, :], (jnp.sin(ang_b) * scale)[None, :, :]
    cmat = (ca * cb - sa * sb).reshape(rows, t).astype(BF16)
    smat = (-(sa * cb + ca * sb)).reshape(rows, t).astype(BF16)
    return cmat, smat


def _ffn_weights(w_gate, w_up, w_down):
    return w_gate.astype(BF16), w_up.astype(BF16), w_down.astype(BF16)


def _s5_weights(bbar_re, bbar_im, c_re, c_im):
    nd = bbar_re.shape[0]
    eye = jnp.eye(S5_SLAB_GROUPS, dtype=F32)

    def drive(bb):
        bb = bb.reshape(nd, SSM_GROUP_W, S5_SLABS, S5_SLAB_GROUPS, SSM_STATE)
        w = jnp.einsum("dhkgp,gG->dkghGp", bb, eye)
        return w.reshape(nd, S5_SLABS, LANES, S5_SLAB_STATES)

    def read(cc):
        cc = cc.astype(F32).reshape(nd, S5_SLABS, S5_SLAB_GROUPS, SSM_GROUP_W, SSM_STATE)
        w = jnp.einsum("dkghp,gG->dkgpGh", cc, eye)
        return w.reshape(nd, S5_SLABS, S5_SLAB_STATES, LANES)

    w_drive = jnp.concatenate([drive(bbar_re), drive(bbar_im)], axis=-1).astype(BF16)
    w_read = jnp.concatenate([read(c_re), -read(c_im)], axis=2).astype(BF16)
    return w_drive, w_read


def kernel(x, c, ctx, c_ctx, mod_w, mod_b, norm_g, ffn_w_gate, ffn_w_up, ffn_w_down, mix_w_in, mix_w_out,
           attn_sink, ssm_a_re, ssm_a_im, ssm_log_dt, ssm_b_re, ssm_b_im, ssm_c_re, ssm_c_im, ssm_d,
           ssm_glu_w, final_g):
    bn, length, d = x.shape
    n_ctx = ctx.shape[1]
    depth = mod_w.shape[0]
    assert depth == 2 and d == D_MODEL

    cond = jnp.zeros((SUBLANES, d), F32).at[:bn].set(c).at[bn].set(c_ctx)
    mods = _modulations(cond, mod_w, mod_b).reshape(depth, SUBLANES, N_MOD, d)
    mods_lat, mods_ctx = mods[:, :bn], mods[:, bn:bn + 1]
    vec = lambda a: a.reshape(1, d)

    w_in = mix_w_in[0][:, _half_split_perm()].astype(BF16)
    w_out = mix_w_out[0].astype(BF16)
    bcs = _channel_dft()
    cos_t, sin_t = _rope_tables(length)
    ffn0 = _ffn_weights(ffn_w_gate[0], ffn_w_up[0], ffn_w_down[0])
    g_mix, g_ffn = vec(norm_g[0, 0]), vec(norm_g[0, 1])

    uv, q, k, v = _pre0(x, mods_lat[0], g_mix, w_in, bcs, cos_t, sin_t, rope=True, rows=TOKEN_TILE)
    uvc, qc, kc, vc = _pre0(ctx, mods_ctx[0], g_mix, w_in, bcs, cos_t, sin_t, rope=False, rows=n_ctx)
    four = _time_dft_symmetric(*_time_dft_matrices(length, length // 2 + BLOCK), uv)
    four_c = _time_dft(*_time_dft_matrices(n_ctx), uvc)
    att = _window_attention(attn_sink[0], q, k, v, kc, vc)
    att_c = _context_attention(attn_sink[0], qc, kc, vc)
    g_mix1 = vec(norm_g[1, 0])
    h, u_lat = _post_call(_post0_kernel, "mix_out_ffn", x, [mods_lat[0], mods_lat[1]], [four, att],
                          [g_ffn, g_mix1], [w_out, *ffn0], TOKEN_TILE, [F32, BF16])
    _, u_ctx = _post_call(_post0_kernel, "mix_out_ffn_ctx", ctx, [mods_ctx[0], mods_ctx[1]], [four_c, att_c],
                          [g_ffn, g_mix1], [w_out, *ffn0], n_ctx, [F32, BF16])

    g_mix, g_ffn = g_mix1, vec(norm_g[1, 1])
    abar_re, abar_im, bbar_re, bbar_im = _discretize(
        ssm_a_re[0], ssm_a_im[0], ssm_log_dt[0], ssm_b_re[0], ssm_b_im[0])
    w_drive, w_read = _s5_weights(bbar_re, bbar_im, ssm_c_re[0], ssm_c_im[0])
    planes = lambda a: a.reshape(2, S5_PLANES, SUBLANES, LANES)
    abar_re, abar_im = planes(abar_re), planes(abar_im)
    y_dir = [_s5_direction(u_ctx, u_lat, w_drive[i], w_read[i], abar_re[i], abar_im[i], reverse=bool(i))
             for i in range(2)]
    ffn1 = _ffn_weights(ffn_w_gate[1], ffn_w_up[1], ffn_w_down[1])
    return _post_call(_post1_kernel, "s5_glu_ffn_norm", h, [mods_lat[1]], y_dir,
                      [g_mix, g_ffn, vec(final_g), vec(ssm_d[0])],
                      [ssm_glu_w[0].astype(BF16), *ffn1], TOKEN_TILE, [F32])[0]
```

```python
import functools
import math

import numpy as np
import jax
import jax.numpy as jnp
from jax import lax
from jax.experimental import pallas as pl
from jax.experimental.pallas import tpu as pltpu

F32 = jnp.float32
BF16 = jnp.bfloat16

D_MODEL = 1024
GRID_W = 64
N_MOD = 6
N_Q_HEADS = 8
N_KV_HEADS = 2
HEAD_DIM = 64
ATTN_WIDTH = N_Q_HEADS * HEAD_DIM
KV_WIDTH = N_KV_HEADS * HEAD_DIM
FOURIER_GROUPS = 8
FOURIER_GROUP_W = 64
FOURIER_WIDTH = FOURIER_GROUPS * FOURIER_GROUP_W
KV_START = FOURIER_WIDTH + ATTN_WIDTH
MIX_IN_WIDTH = KV_START + 2 * KV_WIDTH
WINDOW = 128
BLOCK = 128
ATTN_SCALE = HEAD_DIM ** -0.5
LOG2_E = math.log2(math.e)
ROPE_BASE = 10000.0
NEG_INF = -1e30
SSM_GROUP_W = 16
SSM_GROUPS = D_MODEL // SSM_GROUP_W
SSM_STATE = 64
SSM_WIDTH = SSM_GROUPS * SSM_STATE
RMS_EPS = 1e-6

LANES = 128
SUBLANES = 8
VMEM_LIMIT_BYTES = 56 * 1024 * 1024

TOKEN_TILE = 512
FFN_CHUNK = 256
MOD_TILE = 1536
DFT_ROW_TILE = 1024
DFT_K_TILE = 4096
S5_CHUNK = 256
S5_SLAB_GROUPS = LANES // SSM_GROUP_W
S5_SLABS = SSM_GROUPS // S5_SLAB_GROUPS
S5_SLAB_STATES = S5_SLAB_GROUPS * SSM_STATE
S5_COLS = SSM_WIDTH // LANES
S5_PLANES = S5_COLS // SUBLANES


def _cparams(*sem):
    return pltpu.CompilerParams(dimension_semantics=sem, vmem_limit_bytes=VMEM_LIMIT_BYTES)


def _sigmoid(x):
    return 1.0 / (1.0 + jnp.exp(-x))


def _norm_mod(x, g, scale, shift):
    y = x * lax.rsqrt(jnp.mean(x * x, axis=-1, keepdims=True) + RMS_EPS)
    return (y * g) * (1.0 + scale) + shift


def _split_bf16(x):
    hi = x.astype(BF16)
    lo = (x - hi.astype(F32)).astype(BF16)
    return hi, lo


def _mod_kernel(cond_ref, w_ref, b_ref, o_ref):
    x = cond_ref[...]
    s_hi, s_lo = _split_bf16(x * _sigmoid(x))
    w_hi, w_lo = _split_bf16(w_ref[0])
    acc = jnp.dot(s_hi, w_hi, preferred_element_type=F32)
    acc += jnp.dot(s_lo, w_hi, preferred_element_type=F32)
    acc += jnp.dot(s_hi, w_lo, preferred_element_type=F32)
    o_ref[0] = acc + b_ref[0]


def _modulations(cond, mod_w, mod_b):
    depth, d, n = mod_w.shape
    rows = cond.shape[0]
    return pl.pallas_call(
        _mod_kernel,
        grid=(depth, n // MOD_TILE),
        in_specs=[
            pl.BlockSpec((rows, d), lambda l, j: (0, 0)),
            pl.BlockSpec((1, d, MOD_TILE), lambda l, j: (l, 0, j)),
            pl.BlockSpec((1, 1, MOD_TILE), lambda l, j: (l, 0, j)),
        ],
        out_specs=pl.BlockSpec((1, rows, MOD_TILE), lambda l, j: (l, 0, j)),
        out_shape=jax.ShapeDtypeStruct((depth, rows, n), F32),
        compiler_params=_cparams("parallel", "parallel"),
        name="adaln_mod",
    )(cond, mod_w, mod_b.reshape(depth, 1, n))


def _pre0_kernel(x_ref, mod_ref, g_ref, w_ref, bcs_ref, cos_ref, sin_ref,
                 uv_ref, q_ref, k_ref, v_ref, *, rope):
    m = mod_ref[0]
    n = _norm_mod(x_ref[0], g_ref[...], m[1:2], m[0:1])
    p = jnp.dot(n.astype(BF16), w_ref[...], preferred_element_type=F32)
    uv = jnp.dot(p[:, :FOURIER_WIDTH].astype(BF16), bcs_ref[...], preferred_element_type=F32)
    uv_ref[0, 0] = uv[:, :FOURIER_WIDTH].astype(BF16)
    uv_ref[0, 1] = uv[:, FOURIER_WIDTH:].astype(BF16)

    def rot(x):
        if not rope:
            return x
        lane = lax.broadcasted_iota(jnp.int32, x.shape, 1)
        first = (lane & (HEAD_DIM // 2)) == 0
        partner = jnp.where(first, pltpu.roll(x, LANES - HEAD_DIM // 2, 1),
                            pltpu.roll(x, HEAD_DIM // 2, 1))
        return x * cos_ref[...] + partner * sin_ref[...]

    for j in range(ATTN_WIDTH // LANES):
        lo = FOURIER_WIDTH + j * LANES
        q_ref[0, :, j * LANES:(j + 1) * LANES] = (rot(p[:, lo:lo + LANES]) * (ATTN_SCALE * LOG2_E)).astype(BF16)
    k_ref[0] = rot(p[:, KV_START:KV_START + KV_WIDTH]).astype(BF16)
    v_ref[0] = p[:, KV_START + KV_WIDTH:].astype(BF16)


def _pre0(x, mods, g, w_in, bcs, cos_t, sin_t, *, rope, rows):
    b, length, d = x.shape
    grid = (b, length // rows)
    row_map = lambda i, t: (i, t, 0)
    const2 = lambda i, t: (0, 0)
    return pl.pallas_call(
        functools.partial(_pre0_kernel, rope=rope),
        grid=grid,
        in_specs=[
            pl.BlockSpec((1, rows, d), row_map),
            pl.BlockSpec((1, N_MOD, d), lambda i, t: (i if mods.shape[0] > 1 else 0, 0, 0)),
            pl.BlockSpec((1, d), const2),
            pl.BlockSpec((d, MIX_IN_WIDTH), const2),
            pl.BlockSpec((FOURIER_WIDTH, 2 * FOURIER_WIDTH), const2),
            pl.BlockSpec((rows, LANES), lambda i, t: (t, 0)),
            pl.BlockSpec((rows, LANES), lambda i, t: (t, 0)),
        ],
        out_specs=[
            pl.BlockSpec((1, 2, rows, FOURIER_WIDTH), lambda i, t: (i, 0, t, 0)),
            pl.BlockSpec((1, rows, ATTN_WIDTH), row_map),
            pl.BlockSpec((1, rows, KV_WIDTH), row_map),
            pl.BlockSpec((1, rows, KV_WIDTH), row_map),
        ],
        out_shape=[
            jax.ShapeDtypeStruct((b, 2, length, FOURIER_WIDTH), BF16),
            jax.ShapeDtypeStruct((b, length, ATTN_WIDTH), BF16),
            jax.ShapeDtypeStruct((b, length, KV_WIDTH), BF16),
            jax.ShapeDtypeStruct((b, length, KV_WIDTH), BF16),
        ],
        compiler_params=_cparams("parallel", "parallel"),
        name="mix_in_rope" if rope else "mix_in_ctx",
    )(x, mods, g, w_in, bcs, cos_t, sin_t)


def _dft_kernel(c_ref, s_ref, u_ref, v_ref, o_ref, acc_ref):
    k = pl.program_id(2)

    @pl.when(k == 0)
    def _():
        acc_ref[...] = jnp.zeros_like(acc_ref)

    acc_ref[...] += (jnp.dot(c_ref[...], u_ref[0, 0], preferred_element_type=F32)
                     + jnp.dot(s_ref[...], v_ref[0, 0], preferred_element_type=F32))

    @pl.when(k == pl.num_programs(2) - 1)
    def _():
        o_ref[0] = acc_ref[...].astype(BF16)


def _time_dft(cmat, smat, uv):
    b, _, t, w = uv.shape
    tm = min(DFT_ROW_TILE, t)
    tk = min(DFT_K_TILE, t)
    mat_spec = pl.BlockSpec((tm, tk), lambda i, r, k: (r, k))
    return pl.pallas_call(
        _dft_kernel,
        grid=(b, t // tm, t // tk),
        in_specs=[
            mat_spec, mat_spec,
            pl.BlockSpec((1, 1, tk, w), lambda i, r, k: (i, 0, k, 0)),
            pl.BlockSpec((1, 1, tk, w), lambda i, r, k: (i, 1, k, 0)),
        ],
        out_specs=pl.BlockSpec((1, tm, w), lambda i, r, k: (i, r, 0)),
        out_shape=jax.ShapeDtypeStruct((b, t, w), BF16),
        scratch_shapes=[pltpu.VMEM((tm, w), F32)],
        compiler_params=_cparams("parallel", "parallel", "arbitrary"),
        name="time_dft",
    )(cmat, smat, uv, uv)


def _dft_half_kernel(c_ref, s_ref, u_ref, v_ref, lo_ref, mir_ref, p_acc, q_acc):
    k = pl.program_id(2)

    @pl.when(k == 0)
    def _():
        p_acc[...] = jnp.zeros_like(p_acc)
        q_acc[...] = jnp.zeros_like(q_acc)

    p_acc[...] += jnp.dot(c_ref[...], u_ref[0, 0], preferred_element_type=F32)
    q_acc[...] += jnp.dot(s_ref[...], v_ref[0, 0], preferred_element_type=F32)

    @pl.when(k == pl.num_programs(2) - 1)
    def _():
        lo_ref[0] = (p_acc[...] + q_acc[...]).astype(BF16)
        mir_ref[0] = (p_acc[...] - q_acc[...]).astype(BF16)


def _dft_mirror_kernel(lo_ref, mir_ref, o_ref):
    i = pl.program_id(1)
    half = o_ref.shape[1]

    @pl.when(i == 0)
    def _():
        o_ref[0] = lo_ref[0, :half, :]

    @pl.when(i == 1)
    def _():
        r = lax.broadcasted_iota(jnp.int32, (BLOCK, BLOCK), 0)
        c = lax.broadcasted_iota(jnp.int32, (BLOCK, BLOCK), 1)
        rev = jnp.where(c == BLOCK - r, 1.0, 0.0).astype(BF16)
        first = jnp.where((r == 0) & (c == 0), 1.0, 0.0).astype(BF16)
        for a in range(half // BLOCK):
            top = half - a * BLOCK
            z = jnp.dot(rev, mir_ref[0, top - BLOCK:top, :], preferred_element_type=F32)
            z += jnp.dot(first, mir_ref[0, top:top + BLOCK, :], preferred_element_type=F32)
            o_ref[0, a * BLOCK:(a + 1) * BLOCK, :] = z.astype(BF16)


def _time_dft_symmetric(cmat, smat, uv):
    b, _, t, w = uv.shape
    rows = cmat.shape[0]
    assert rows == t // 2 + BLOCK and rows % (2 * SUBLANES) == 0
    tm = rows // 4
    tk = min(DFT_K_TILE, t)
    mat_spec = pl.BlockSpec((tm, tk), lambda i, r, k: (r, k))
    out_spec = pl.BlockSpec((1, tm, w), lambda i, r, k: (i, r, 0))
    lo, mir = pl.pallas_call(
        _dft_half_kernel,
        grid=(b, rows // tm, t // tk),
        in_specs=[
            mat_spec, mat_spec,
            pl.BlockSpec((1, 1, tk, w), lambda i, r, k: (i, 0, k, 0)),
            pl.BlockSpec((1, 1, tk, w), lambda i, r, k: (i, 1, k, 0)),
        ],
        out_specs=[out_spec, out_spec],
        out_shape=[jax.ShapeDtypeStruct((b, rows, w), BF16)] * 2,
        scratch_shapes=[pltpu.VMEM((tm, w), F32), pltpu.VMEM((tm, w), F32)],
        compiler_params=_cparams("parallel", "parallel", "arbitrary"),
        name="time_dft_half",
    )(cmat, smat, uv, uv)
    whole = pl.BlockSpec((1, rows, w), lambda i, n: (i, 0, 0))
    return pl.pallas_call(
        _dft_mirror_kernel,
        grid=(b, 2),
        in_specs=[whole, whole],
        out_specs=pl.BlockSpec((1, t // 2, w), lambda i, n: (i, n, 0)),
        out_shape=jax.ShapeDtypeStruct((b, t, w), BF16),
        compiler_params=_cparams("parallel", "parallel"),
        name="time_dft_mirror",
    )(lo, mir)


def _attn_kernel(sink_ref, q_ref, *refs, window):
    if window:
        kp_ref, km_ref, kn_ref, vp_ref, vm_ref, vn_ref, kx_ref, vx_ref, o_ref, s_scr, p_scr = refs
        k_pieces = [kx_ref[0], kp_ref[0], km_ref[0, :BLOCK, :], km_ref[0, BLOCK:, :], kn_ref[0]]
        v_pieces = [vx_ref[0], vp_ref[0], vm_ref[0, :BLOCK, :], vm_ref[0, BLOCK:, :], vn_ref[0]]
        subs = ((0, 1, 2, 3), (0, 2, 3, 4))
    else:
        kx_ref, vx_ref, o_ref, s_scr, p_scr = refs
        k_pieces, v_pieces = [kx_ref[0]], [vx_ref[0]]
        subs = ((0,),)
    n_sub = len(subs)
    n_q = q_ref.shape[1] // n_sub

    def split_heads(x):
        low = lax.broadcasted_iota(jnp.int32, x.shape, 1) < HEAD_DIM
        zero = jnp.zeros_like(x)
        swap = pltpu.roll(x.astype(F32), HEAD_DIM, 1).astype(BF16)
        return ((jnp.where(low, x, zero), jnp.where(low, swap, zero)),
                (jnp.where(low, zero, swap), jnp.where(low, zero, x)))

    k_ext = [split_heads(x) for x in k_pieces]
    v_ext = [split_heads(x) for x in v_pieces]
    cache = {}

    def operand(name, ext, sub, half, g):
        key = (name, sub, half, g)
        if key not in cache:
            parts = [ext[i][half][g] for i in subs[sub]]
            cache[key] = parts[0] if len(parts) == 1 else jnp.concatenate(parts, axis=0)
        return cache[key]

    if window:
        step = pl.program_id(1)
        last = pl.num_programs(1) - 1
        n_ctx = kx_ref.shape[1]
        row = lax.broadcasted_iota(jnp.int32, (n_q, BLOCK), 0)
        col = lax.broadcasted_iota(jnp.int32, (n_q, BLOCK), 1)
        ok_prev = (col >= row + jnp.where(step > 0, 0, BLOCK), col >= row)
        ok_next = (col <= row, col <= row - jnp.where(step < last, 0, BLOCK))

    q_per_kv = N_Q_HEADS // N_KV_HEADS
    n_pairs = ATTN_WIDTH // LANES
    for sub in range(n_sub):
        rows = slice(sub * n_q, (sub + 1) * n_q)
        for j in range(n_pairs):
            g = (2 * j) // q_per_kv
            qp = q_ref[0, rows, j * LANES:(j + 1) * LANES]
            for half in range(2):
                s = lax.dot_general(qp, operand("k", k_ext, sub, half, g), (((1,), (1,)), ((), ())),
                                    preferred_element_type=F32)
                item = sub * N_Q_HEADS + 2 * j + half
                if window:
                    s_scr[item, :, :n_ctx] = s[:, :n_ctx]
                    s_scr[item, :, n_ctx:n_ctx + BLOCK] = jnp.where(
                        ok_prev[sub], s[:, n_ctx:n_ctx + BLOCK], NEG_INF)
                    s_scr[item, :, n_ctx + BLOCK:n_ctx + 2 * BLOCK] = s[:, n_ctx + BLOCK:n_ctx + 2 * BLOCK]
                    s_scr[item, :, n_ctx + 2 * BLOCK:] = jnp.where(
                        ok_next[sub], s[:, n_ctx + 2 * BLOCK:], NEG_INF)
                else:
                    s_scr[item] = s
    inv = []
    n_keys = s_scr.shape[2]
    col_blocks = [slice(c, c + LANES) for c in range(0, n_keys, LANES)]
    for item in range(n_sub * N_Q_HEADS):
        sink = sink_ref[item % N_Q_HEADS] * LOG2_E
        mx = s_scr[item, :, col_blocks[0]]
        for cb in col_blocks[1:]:
            mx = jnp.maximum(mx, s_scr[item, :, cb])
        mx = jnp.maximum(jnp.max(mx, axis=1, keepdims=True), sink)
        total = None
        for cb in col_blocks:
            p = jnp.exp2(s_scr[item, :, cb] - mx)
            total = p if total is None else total + p
            p_scr[item, :, cb] = p.astype(BF16)
        inv.append(1.0 / (jnp.sum(total, axis=1, keepdims=True) + jnp.exp2(sink - mx)))
    for sub in range(n_sub):
        rows = slice(sub * n_q, (sub + 1) * n_q)
        for j in range(n_pairs):
            g = (2 * j) // q_per_kv
            a = sub * N_Q_HEADS + 2 * j
            out = jnp.dot(p_scr[a], operand("v", v_ext, sub, 0, g), preferred_element_type=F32) * inv[a]
            out += jnp.dot(p_scr[a + 1], operand("v", v_ext, sub, 1, g), preferred_element_type=F32) * inv[a + 1]
            o_ref[0, rows, j * LANES:(j + 1) * LANES] = out.astype(BF16)


def _attn_scratch(n_items, n_q, n_keys):
    return [pltpu.VMEM((n_items, n_q, n_keys), F32), pltpu.VMEM((n_items, n_q, n_keys), BF16)]


def _window_attention(sink, q, k, v, kx, vx):
    b, length, _ = q.shape
    nb = length // BLOCK
    n_ctx = kx.shape[1]
    assert nb % 2 == 0
    one = lambda f: pl.BlockSpec((1, BLOCK, KV_WIDTH), f)
    two = pl.BlockSpec((1, 2 * BLOCK, KV_WIDTH), lambda i, m: (i, m, 0))
    prev = lambda i, m: (i, jnp.maximum(2 * m - 1, 0), 0)
    nxt = lambda i, m: (i, jnp.minimum(2 * m + 2, nb - 1), 0)
    ctx_spec = pl.BlockSpec((1, n_ctx, KV_WIDTH), lambda i, m: (i, 0, 0))
    q_spec = pl.BlockSpec((1, 2 * BLOCK, ATTN_WIDTH), lambda i, m: (i, m, 0))
    return pl.pallas_call(
        functools.partial(_attn_kernel, window=True),
        grid=(b, nb // 2),
        in_specs=[
            pl.BlockSpec(memory_space=pltpu.SMEM),
            q_spec,
            one(prev), two, one(nxt),
            one(prev), two, one(nxt),
            ctx_spec, ctx_spec,
        ],
        out_specs=q_spec,
        out_shape=jax.ShapeDtypeStruct((b, length, ATTN_WIDTH), BF16),
        scratch_shapes=_attn_scratch(2 * N_Q_HEADS, BLOCK, n_ctx + 3 * BLOCK),
        compiler_params=_cparams("parallel", "parallel"),
        name="window_attn",
    )(sink, q, k, k, k, v, v, v, kx, vx)


def _context_attention(sink, q, kx, vx):
    b, n_ctx, _ = q.shape
    spec = lambda w: pl.BlockSpec((1, n_ctx, w), lambda i: (i, 0, 0))
    return pl.pallas_call(
        functools.partial(_attn_kernel, window=False),
        grid=(b,),
        in_specs=[pl.BlockSpec(memory_space=pltpu.SMEM), spec(ATTN_WIDTH), spec(KV_WIDTH), spec(KV_WIDTH)],
        out_specs=spec(ATTN_WIDTH),
        out_shape=jax.ShapeDtypeStruct((b, n_ctx, ATTN_WIDTH), BF16),
        scratch_shapes=_attn_scratch(N_Q_HEADS, n_ctx, n_ctx),
        compiler_params=_cparams("parallel"),
        name="context_attn",
    )(sink, q, kx, vx)


def _swiglu_residual(h1, m, g2, wg_ref, wu_ref, wd_ref, acc_ref):
    n2 = _norm_mod(h1, g2, m[4:5], m[3:4]).astype(BF16)
    for c in range(wg_ref.shape[1] // FFN_CHUNK):
        cols = slice(c * FFN_CHUNK, (c + 1) * FFN_CHUNK)
        gate = jnp.dot(n2, wg_ref[:, cols], preferred_element_type=F32)
        up = jnp.dot(n2, wu_ref[:, cols], preferred_element_type=F32)
        act = (gate * _sigmoid(gate)) * up
        part = jnp.dot(act.astype(BF16), wd_ref[cols, :], preferred_element_type=F32)
        if c == 0:
            acc_ref[...] = part
        else:
            acc_ref[...] += part
    return h1 + m[5:6] * acc_ref[...]


def _post0_kernel(h_ref, mod_ref, nmod_ref, g2_ref, gn_ref, f_ref, a_ref, wo_ref, wg_ref, wu_ref, wd_ref,
                  o_ref, u_ref, acc_ref):
    m = mod_ref[0]
    mix = jnp.dot(f_ref[0], wo_ref[:FOURIER_WIDTH, :], preferred_element_type=F32)
    mix += jnp.dot(a_ref[0], wo_ref[FOURIER_WIDTH:, :], preferred_element_type=F32)
    h1 = h_ref[0] + m[2:3] * mix
    h2 = _swiglu_residual(h1, m, g2_ref[...], wg_ref, wu_ref, wd_ref, acc_ref)
    o_ref[0] = h2
    nm = nmod_ref[0]
    u_ref[0] = _norm_mod(h2, gn_ref[...], nm[1:2], nm[0:1]).astype(BF16)


def _gelu_tanh(x):
    sqrt_2_over_pi = np.sqrt(2 / np.pi).astype(np.float32)
    cdf = 0.5 * (1.0 + jnp.tanh(sqrt_2_over_pi * (x + 0.044715 * (x ** 3))))
    return x * cdf


def _post1_kernel(h_ref, mod_ref, g1_ref, g2_ref, gf_ref, dskip_ref, yf_ref, yb_ref, wglu_ref,
                  wg_ref, wu_ref, wd_ref, o_ref, acc_ref):
    m = mod_ref[0]
    h = h_ref[0]
    u = _norm_mod(h, g1_ref[...], m[1:2], m[0:1])
    y = dskip_ref[...] * u + yf_ref[0] + yb_ref[0]
    ge = _gelu_tanh(y).astype(BF16)
    for c in range(D_MODEL // FFN_CHUNK):
        cols = slice(c * FFN_CHUNK, (c + 1) * FFN_CHUNK)
        gcols = slice(D_MODEL + c * FFN_CHUNK, D_MODEL + (c + 1) * FFN_CHUNK)
        val = jnp.dot(ge, wglu_ref[:, cols], preferred_element_type=F32)
        gate = jnp.dot(ge, wglu_ref[:, gcols], preferred_element_type=F32)
        o_ref[0, :, cols] = h[:, cols] + m[2:3, cols] * (val * _sigmoid(gate))
    h1 = o_ref[0]
    h2 = _swiglu_residual(h1, m, g2_ref[...], wg_ref, wu_ref, wd_ref, acc_ref)
    o_ref[0] = (h2 * lax.rsqrt(jnp.mean(h2 * h2, axis=-1, keepdims=True) + RMS_EPS)) * gf_ref[...]


def _resident(shape):
    zeros = (0,) * len(shape)
    return pl.BlockSpec(shape, lambda i, t: zeros, pipeline_mode=pl.Buffered(1))


def _post_call(kernel, name, h, mods, row_inputs, vec_inputs, weights, rows, out_dtypes):
    b, length, d = h.shape
    row_map = lambda i, t: (i, t, 0)
    mod_spec = lambda m: pl.BlockSpec((1, N_MOD, d), lambda i, t: (i if m.shape[0] > 1 else 0, 0, 0))
    in_specs = [pl.BlockSpec((1, rows, d), row_map)]
    in_specs += [mod_spec(m) for m in mods]
    in_specs += [pl.BlockSpec((1, d), lambda i, t: (0, 0)) for _ in vec_inputs]
    in_specs += [pl.BlockSpec((1, rows, a.shape[-1]), row_map) for a in row_inputs]
    in_specs += [_resident(w.shape) for w in weights]
    return pl.pallas_call(
        kernel,
        grid=(b, length // rows),
        in_specs=in_specs,
        out_specs=[pl.BlockSpec((1, rows, d), row_map) for _ in out_dtypes],
        out_shape=[jax.ShapeDtypeStruct((b, length, d), dt) for dt in out_dtypes],
        scratch_shapes=[pltpu.VMEM((rows, d), F32)],
        compiler_params=_cparams("parallel", "parallel"),
        name=name,
    )(h, *mods, *vec_inputs, *row_inputs, *weights)


def _disc_kernel(are_ref, aim_ref, ldt_ref, bre_ref, bim_ref, oar_ref, oai_ref, obr_ref, obi_ref):
    a_re, a_im = are_ref[0], aim_ref[0]
    dt = jnp.exp(ldt_ref[0])
    mag = jnp.exp(a_re * dt)
    abar_re, abar_im = mag * jnp.cos(a_im * dt), mag * jnp.sin(a_im * dt)
    nr, ni = abar_re - 1.0, abar_im
    den = a_re * a_re + a_im * a_im
    f_re = (nr * a_re + ni * a_im) / den
    f_im = (ni * a_re - nr * a_im) / den
    oar_ref[0] = abar_re
    oai_ref[0] = abar_im
    obr_ref[0] = f_re[None] * bre_ref[0] - f_im[None] * bim_ref[0]
    obi_ref[0] = f_re[None] * bim_ref[0] + f_im[None] * bre_ref[0]


def _discretize(a_re, a_im, log_dt, b_re, b_im):
    nd = a_re.shape[0]
    rows = SSM_WIDTH // LANES
    flat = lambda a: a.reshape(nd, rows, LANES)
    ldt = jnp.broadcast_to(log_dt[:, :, None], (nd, SSM_GROUPS, SSM_STATE))
    chan = lambda a: a.reshape(nd, SSM_WIDTH, SSM_GROUP_W).transpose(0, 2, 1).reshape(nd, SSM_GROUP_W, rows, LANES)
    aspec = pl.BlockSpec((1, rows, LANES), lambda i: (i, 0, 0))
    bspec = pl.BlockSpec((1, SSM_GROUP_W, rows, LANES), lambda i: (i, 0, 0, 0))
    ashape = jax.ShapeDtypeStruct((nd, rows, LANES), F32)
    bshape = jax.ShapeDtypeStruct((nd, SSM_GROUP_W, rows, LANES), F32)
    return pl.pallas_call(
        _disc_kernel,
        grid=(nd,),
        in_specs=[aspec, aspec, aspec, bspec, bspec],
        out_specs=[aspec, aspec, bspec, bspec],
        out_shape=[ashape, ashape, bshape, bshape],
        compiler_params=_cparams("parallel"),
        name="s5_discretize",
    )(flat(a_re), flat(a_im), flat(ldt), chan(b_re), chan(b_im))


def _s5_kernel(uc_ref, ul_ref, wd_ref, wr_ref, are_ref, aim_ref, y_ref,
               xa_re, xa_im, xb_re, xb_im, hre, him, *, reverse):
    step_idx = pl.program_id(1)
    steps = S5_CHUNK
    cols_per_slab = S5_SLAB_STATES // LANES
    seg = steps // S5_SLABS

    @pl.when(step_idx == 0)
    def _():
        hre[...] = jnp.zeros_like(hre)
        him[...] = jnp.zeros_like(him)

    def plane_of(col):
        return col // SUBLANES, col % SUBLANES

    def drive_slab(u, kk, xre, xim):
        bu = jnp.dot(u(kk * LANES, (kk + 1) * LANES), wd_ref[kk], preferred_element_type=F32)
        for c in range(cols_per_slab):
            pln, sub = plane_of(kk * cols_per_slab + c)
            xre[pln, pl.ds(sub, steps, stride=SUBLANES), :] = bu[:, c * LANES:(c + 1) * LANES]
            xim[pln, pl.ds(sub, steps, stride=SUBLANES), :] = (
                bu[:, S5_SLAB_STATES + c * LANES:S5_SLAB_STATES + (c + 1) * LANES])

    def read_slab(kk, xre, xim):
        parts = []
        for src in (xre, xim):
            for c in range(cols_per_slab):
                pln, sub = plane_of(kk * cols_per_slab + c)
                parts.append(src[pln, pl.ds(sub, steps, stride=SUBLANES), :].astype(BF16))
        return jnp.dot(jnp.concatenate(parts, axis=1), wr_ref[kk], preferred_element_type=F32)

    a_r = [are_ref[p] for p in range(S5_PLANES)]
    a_i = [aim_ref[p] for p in range(S5_PLANES)]

    def one_step(t, h, xre, xim):
        h_r, h_i = h
        r0 = t * SUBLANES if isinstance(t, int) else pl.multiple_of(t * SUBLANES, SUBLANES)
        new_r, new_i = [], []
        for p in range(S5_PLANES):
            n_r = a_r[p] * h_r[p] - a_i[p] * h_i[p] + xre[p, pl.ds(r0, SUBLANES), :]
            n_i = a_r[p] * h_i[p] + a_i[p] * h_r[p] + xim[p, pl.ds(r0, SUBLANES), :]
            xre[p, pl.ds(r0, SUBLANES), :] = n_r
            xim[p, pl.ds(r0, SUBLANES), :] = n_i
            new_r.append(n_r)
            new_i.append(n_i)
        return tuple(new_r), tuple(new_i)

    def time_of(i):
        return (steps - 1 - i) if reverse else i

    def scan_segment(k, h, xre, xim):
        for i in range(k * seg, (k + 1) * seg):
            h = one_step(time_of(i), h, xre, xim)
        return h

    def load_state():
        return (tuple(hre[p] for p in range(S5_PLANES)), tuple(him[p] for p in range(S5_PLANES)))

    def store_state(h):
        for p in range(S5_PLANES):
            hre[p] = h[0][p]
            him[p] = h[1][p]

    @pl.when(step_idx == 0)
    def _():
        u = lambda lo, hi: uc_ref[0, :, lo:hi]
        for kk in range(S5_SLABS):
            drive_slab(u, kk, xa_re, xa_im)
        h = lax.fori_loop(0, steps, lambda i, h: one_step(time_of(i), h, xa_re, xa_im), load_state(), unroll=8)
        store_state(h)

    @pl.when(step_idx > 0)
    def _():
        first, second = slice(0, steps), slice(steps, 2 * steps)
        rows_a, rows_b = (second, first) if reverse else (first, second)
        u_a = lambda lo, hi: ul_ref[0, rows_a, lo:hi]
        u_b = lambda lo, hi: ul_ref[0, rows_b, lo:hi]
        for kk in range(S5_SLABS):
            drive_slab(u_a, kk, xa_re, xa_im)
        h = load_state()
        for kk in range(S5_SLABS):
            drive_slab(u_b, kk, xb_re, xb_im)
            h = scan_segment(kk, h, xa_re, xa_im)
        for kk in range(S5_SLABS):
            h = scan_segment(kk, h, xb_re, xb_im)
            y_ref[0, rows_a, kk * LANES:(kk + 1) * LANES] = read_slab(kk, xa_re, xa_im)
        store_state(h)
        for kk in range(S5_SLABS):
            y_ref[0, rows_b, kk * LANES:(kk + 1) * LANES] = read_slab(kk, xb_re, xb_im)


def _s5_direction(u_ctx, u_lat, w_drive, w_read, a_re, a_im, *, reverse):
    b, length, d = u_lat.shape
    n_ctx = u_ctx.shape[1]
    pair = 2 * S5_CHUNK
    assert n_ctx == S5_CHUNK and length % pair == 0
    n_pairs = length // pair
    if reverse:
        lat_map = lambda i, n: (i, jnp.minimum(n_pairs - n, n_pairs - 1), 0)
    else:
        lat_map = lambda i, n: (i, jnp.maximum(n - 1, 0), 0)
    whole = lambda shape: pl.BlockSpec(shape, lambda i, n: (0,) * len(shape))
    plane_buf = pltpu.VMEM((S5_PLANES, S5_CHUNK * SUBLANES, LANES), F32)
    state_buf = pltpu.VMEM((S5_PLANES, SUBLANES, LANES), F32)
    return pl.pallas_call(
        functools.partial(_s5_kernel, reverse=reverse),
        grid=(b, n_pairs + 1),
        in_specs=[
            pl.BlockSpec((1, n_ctx, d), lambda i, n: (i, 0, 0)),
            pl.BlockSpec((1, pair, d), lat_map),
            whole(w_drive.shape), whole(w_read.shape), whole(a_re.shape), whole(a_im.shape),
        ],
        out_specs=pl.BlockSpec((1, pair, d), lat_map),
        out_shape=jax.ShapeDtypeStruct((b, length, d), F32),
        scratch_shapes=[plane_buf, plane_buf, plane_buf, plane_buf, state_buf, state_buf],
        compiler_params=_cparams("parallel", "arbitrary"),
        name="s5_scan_bwd" if reverse else "s5_scan_fwd",
    )(u_ctx, u_lat, w_drive, w_read, a_re, a_im)


def _rope_tables(length):
    rows = length // GRID_W
    t_row = jnp.repeat(jnp.arange(rows, dtype=F32), GRID_W)
    t_col = jnp.tile(jnp.arange(GRID_W, dtype=F32), rows)
    n_freq = HEAD_DIM // 4
    inv_freq = ROPE_BASE ** (-jnp.arange(n_freq, dtype=F32) / n_freq)
    ang = jnp.concatenate([t_row[:, None] * inv_freq, t_col[:, None] * inv_freq], axis=-1)
    cos, sin = jnp.cos(ang), jnp.sin(ang)
    reps = LANES // HEAD_DIM
    return (jnp.tile(jnp.concatenate([cos, cos], axis=-1), (1, reps)),
            jnp.tile(jnp.concatenate([-sin, sin], axis=-1), (1, reps)))


def _half_split_perm():
    perm = np.arange(MIX_IN_WIDTH)
    head = np.concatenate([np.arange(0, HEAD_DIM, 2), np.arange(1, HEAD_DIM, 2)])
    for start in range(FOURIER_WIDTH, KV_START + KV_WIDTH, HEAD_DIM):
        perm[start:start + HEAD_DIM] = start + head
    return perm


def _channel_dft():
    c = np.arange(FOURIER_GROUP_W)
    ang = 2.0 * np.pi * np.outer(c, c) / FOURIER_GROUP_W
    eye = np.eye(FOURIER_GROUPS)
    scale = FOURIER_GROUP_W ** -0.5
    table = np.concatenate([np.kron(eye, np.cos(ang) * scale), np.kron(eye, np.sin(ang) * scale)], axis=1)
    return jnp.asarray(table, F32).astype(BF16)


def _time_dft_matrices(t, rows=None):
    r1 = 1 << ((t.bit_length() - 1 + 1) // 2)
    r2 = t // r1
    assert r1 * r2 == t and r2 % SUBLANES == 0
    rows = t if rows is None else rows
    assert rows % r2 == 0
    n = jnp.arange(t, dtype=jnp.int32)[None, :]
    unit = 2.0 * math.pi / t
    ang_a = (((jnp.arange(rows // r2, dtype=jnp.int32) * r2)[:, None] * n) % t).astype(F32) * unit
    ang_b = ((jnp.arange(r2, dtype=jnp.int32)[:, None] * n) % t).astype(F32) * unit
    scale = t ** -0.5
    ca, sa = jnp.cos(ang_a)[:, None, :], jnp.sin(ang_a)[:, None, :]
    cb, sb = (jnp.cos(ang_b) * scale)[None, :, :], (jnp.sin(ang_b) * scale)[None, :, :]
    cmat = (ca * cb - sa * sb).reshape(rows, t).astype(BF16)
    smat = (-(sa * cb + ca * sb)).reshape(rows, t).astype(BF16)
    return cmat, smat


def _ffn_weights(w_gate, w_up, w_down):
    return w_gate.astype(BF16), w_up.astype(BF16), w_down.astype(BF16)


def _s5_weights(bbar_re, bbar_im, c_re, c_im):
    nd = bbar_re.shape[0]
    eye = jnp.eye(S5_SLAB_GROUPS, dtype=F32)

    def drive(bb):
        bb = bb.reshape(nd, SSM_GROUP_W, S5_SLABS, S5_SLAB_GROUPS, SSM_STATE)
        w = jnp.einsum("dhkgp,gG->dkghGp", bb, eye)
        return w.reshape(nd, S5_SLABS, LANES, S5_SLAB_STATES)

    def read(cc):
        cc = cc.astype(F32).reshape(nd, S5_SLABS, S5_SLAB_GROUPS, SSM_GROUP_W, SSM_STATE)
        w = jnp.einsum("dkghp,gG->dkgpGh", cc, eye)
        return w.reshape(nd, S5_SLABS, S5_SLAB_STATES, LANES)

    w_drive = jnp.concatenate([drive(bbar_re), drive(bbar_im)], axis=-1).astype(BF16)
    w_read = jnp.concatenate([read(c_re), -read(c_im)], axis=2).astype(BF16)
    return w_drive, w_read


def kernel(x, c, ctx, c_ctx, mod_w, mod_b, norm_g, ffn_w_gate, ffn_w_up, ffn_w_down, mix_w_in, mix_w_out,
           attn_sink, ssm_a_re, ssm_a_im, ssm_log_dt, ssm_b_re, ssm_b_im, ssm_c_re, ssm_c_im, ssm_d,
           ssm_glu_w, final_g):
    bn, length, d = x.shape
    n_ctx = ctx.shape[1]
    depth = mod_w.shape[0]
    assert depth == 2 and d == D_MODEL

    cond = jnp.zeros((SUBLANES, d), F32).at[:bn].set(c).at[bn].set(c_ctx)
    mods = _modulations(cond, mod_w, mod_b).reshape(depth, SUBLANES, N_MOD, d)
    mods_lat, mods_ctx = mods[:, :bn], mods[:, bn:bn + 1]
    vec = lambda a: a.reshape(1, d)

    w_in = mix_w_in[0][:, _half_split_perm()].astype(BF16)
    w_out = mix_w_out[0].astype(BF16)
    bcs = _channel_dft()
    cos_t, sin_t = _rope_tables(length)
    ffn0 = _ffn_weights(ffn_w_gate[0], ffn_w_up[0], ffn_w_down[0])
    g_mix, g_ffn = vec(norm_g[0, 0]), vec(norm_g[0, 1])

    uv, q, k, v = _pre0(x, mods_lat[0], g_mix, w_in, bcs, cos_t, sin_t, rope=True, rows=TOKEN_TILE)
    uvc, qc, kc, vc = _pre0(ctx, mods_ctx[0], g_mix, w_in, bcs, cos_t, sin_t, rope=False, rows=n_ctx)
    four = _time_dft_symmetric(*_time_dft_matrices(length, length // 2 + BLOCK), uv)
    four_c = _time_dft(*_time_dft_matrices(n_ctx), uvc)
    att = _window_attention(attn_sink[0], q, k, v, kc, vc)
    att_c = _context_attention(attn_sink[0], qc, kc, vc)
    g_mix1 = vec(norm_g[1, 0])
    h, u_lat = _post_call(_post0_kernel, "mix_out_ffn", x, [mods_lat[0], mods_lat[1]], [four, att],
                          [g_ffn, g_mix1], [w_out, *ffn0], TOKEN_TILE, [F32, BF16])
    _, u_ctx = _post_call(_post0_kernel, "mix_out_ffn_ctx", ctx, [mods_ctx[0], mods_ctx[1]], [four_c, att_c],
                          [g_ffn, g_mix1], [w_out, *ffn0], n_ctx, [F32, BF16])

    g_mix, g_ffn = g_mix1, vec(norm_g[1, 1])
    abar_re, abar_im, bbar_re, bbar_im = _discretize(
        ssm_a_re[0], ssm_a_im[0], ssm_log_dt[0], ssm_b_re[0], ssm_b_im[0])
    w_drive, w_read = _s5_weights(bbar_re, bbar_im, ssm_c_re[0], ssm_c_im[0])
    planes = lambda a: a.reshape(2, S5_PLANES, SUBLANES, LANES)
    abar_re, abar_im = planes(abar_re), planes(abar_im)
    y_dir = [_s5_direction(u_ctx, u_lat, w_drive[i], w_read[i], abar_re[i], abar_im[i], reverse=bool(i))
             for i in range(2)]
    ffn1 = _ffn_weights(ffn_w_gate[1], ffn_w_up[1], ffn_w_down[1])
    return _post_call(_post1_kernel, "s5_glu_ffn_norm", h, [mods_lat[1]], y_dir,
                      [g_mix, g_ffn, vec(final_g), vec(ssm_d[0])],
                      [ssm_glu_w[0].astype(BF16), *ffn1], TOKEN_TILE, [F32])[0]
```

```python
import functools
import math

import numpy as np
import jax
import jax.numpy as jnp
from jax import lax
from jax.experimental import pallas as pl
from jax.experimental.pallas import tpu as pltpu

F32 = jnp.float32
BF16 = jnp.bfloat16

D_MODEL = 1024
GRID_W = 64
N_MOD = 6
N_Q_HEADS = 8
N_KV_HEADS = 2
HEAD_DIM = 64
ATTN_WIDTH = N_Q_HEADS * HEAD_DIM
KV_WIDTH = N_KV_HEADS * HEAD_DIM
FOURIER_GROUPS = 8
FOURIER_GROUP_W = 64
FOURIER_WIDTH = FOURIER_GROUPS * FOURIER_GROUP_W
KV_START = FOURIER_WIDTH + ATTN_WIDTH
MIX_IN_WIDTH = KV_START + 2 * KV_WIDTH
WINDOW = 128
BLOCK = 128
ATTN_SCALE = HEAD_DIM ** -0.5
LOG2_E = math.log2(math.e)
ROPE_BASE = 10000.0
NEG_INF = -1e30
SSM_GROUP_W = 16
SSM_GROUPS = D_MODEL // SSM_GROUP_W
SSM_STATE = 64
SSM_WIDTH = SSM_GROUPS * SSM_STATE
RMS_EPS = 1e-6

LANES = 128
SUBLANES = 8
VMEM_LIMIT_BYTES = 56 * 1024 * 1024

TOKEN_TILE = 512
FFN_CHUNK = 256
MOD_TILE = 1536
DFT_ROW_TILE = 1024
DFT_K_TILE = 4096
S5_CHUNK = 256
S5_SLAB_GROUPS = LANES // SSM_GROUP_W
S5_SLABS = SSM_GROUPS // S5_SLAB_GROUPS
S5_SLAB_STATES = S5_SLAB_GROUPS * SSM_STATE
S5_COLS = SSM_WIDTH // LANES
S5_PLANES = S5_COLS // SUBLANES


def _cparams(*sem):
    return pltpu.CompilerParams(dimension_semantics=sem, vmem_limit_bytes=VMEM_LIMIT_BYTES)


def _sigmoid(x):
    return 1.0 / (1.0 + jnp.exp(-x))


def _norm_mod(x, g, scale, shift):
    y = x * lax.rsqrt(jnp.mean(x * x, axis=-1, keepdims=True) + RMS_EPS)
    return (y * g) * (1.0 + scale) + shift


def _split_bf16(x):
    hi = x.astype(BF16)
    lo = (x - hi.astype(F32)).astype(BF16)
    return hi, lo


def _mod_kernel(cond_ref, w_ref, b_ref, o_ref):
    x = cond_ref[...]
    s_hi, s_lo = _split_bf16(x * _sigmoid(x))
    w_hi, w_lo = _split_bf16(w_ref[0])
    acc = jnp.dot(s_hi, w_hi, preferred_element_type=F32)
    acc += jnp.dot(s_lo, w_hi, preferred_element_type=F32)
    acc += jnp.dot(s_hi, w_lo, preferred_element_type=F32)
    o_ref[0] = acc + b_ref[0]


def _modulations(cond, mod_w, mod_b):
    depth, d, n = mod_w.shape
    rows = cond.shape[0]
    return pl.pallas_call(
        _mod_kernel,
        grid=(depth, n // MOD_TILE),
        in_specs=[
            pl.BlockSpec((rows, d), lambda l, j: (0, 0)),
            pl.BlockSpec((1, d, MOD_TILE), lambda l, j: (l, 0, j)),
            pl.BlockSpec((1, 1, MOD_TILE), lambda l, j: (l, 0, j)),
        ],
        out_specs=pl.BlockSpec((1, rows, MOD_TILE), lambda l, j: (l, 0, j)),
        out_shape=jax.ShapeDtypeStruct((depth, rows, n), F32),
        compiler_params=_cparams("parallel", "parallel"),
        name="adaln_mod",
    )(cond, mod_w, mod_b.reshape(depth, 1, n))


def _pre0_kernel(x_ref, mod_ref, g_ref, w_ref, bcs_ref, cos_ref, sin_ref,
                 uv_ref, q_ref, k_ref, v_ref, *, rope):
    m = mod_ref[0]
    n = _norm_mod(x_ref[0], g_ref[...], m[1:2], m[0:1])
    p = jnp.dot(n.astype(BF16), w_ref[...], preferred_element_type=F32)
    uv = jnp.dot(p[:, :FOURIER_WIDTH].astype(BF16), bcs_ref[...], preferred_element_type=F32)
    uv_ref[0, 0] = uv[:, :FOURIER_WIDTH].astype(BF16)
    uv_ref[0, 1] = uv[:, FOURIER_WIDTH:].astype(BF16)

    def rot(x):
        if not rope:
            return x
        lane = lax.broadcasted_iota(jnp.int32, x.shape, 1)
        first = (lane & (HEAD_DIM // 2)) == 0
        partner = jnp.where(first, pltpu.roll(x, LANES - HEAD_DIM // 2, 1),
                            pltpu.roll(x, HEAD_DIM // 2, 1))
        return x * cos_ref[...] + partner * sin_ref[...]

    for j in range(ATTN_WIDTH // LANES):
        lo = FOURIER_WIDTH + j * LANES
        q_ref[0, :, j * LANES:(j + 1) * LANES] = (rot(p[:, lo:lo + LANES]) * (ATTN_SCALE * LOG2_E)).astype(BF16)
    k_ref[0] = rot(p[:, KV_START:KV_START + KV_WIDTH]).astype(BF16)
    v_ref[0] = p[:, KV_START + KV_WIDTH:].astype(BF16)


def _pre0(x, mods, g, w_in, bcs, cos_t, sin_t, *, rope, rows):
    b, length, d = x.shape
    grid = (b, length // rows)
    row_map = lambda i, t: (i, t, 0)
    const2 = lambda i, t: (0, 0)
    return pl.pallas_call(
        functools.partial(_pre0_kernel, rope=rope),
        grid=grid,
        in_specs=[
            pl.BlockSpec((1, rows, d), row_map),
            pl.BlockSpec((1, N_MOD, d), lambda i, t: (i if mods.shape[0] > 1 else 0, 0, 0)),
            pl.BlockSpec((1, d), const2),
            pl.BlockSpec((d, MIX_IN_WIDTH), const2),
            pl.BlockSpec((FOURIER_WIDTH, 2 * FOURIER_WIDTH), const2),
            pl.BlockSpec((rows, LANES), lambda i, t: (t, 0)),
            pl.BlockSpec((rows, LANES), lambda i, t: (t, 0)),
        ],
        out_specs=[
            pl.BlockSpec((1, 2, rows, FOURIER_WIDTH), lambda i, t: (i, 0, t, 0)),
            pl.BlockSpec((1, rows, ATTN_WIDTH), row_map),
            pl.BlockSpec((1, rows, KV_WIDTH), row_map),
            pl.BlockSpec((1, rows, KV_WIDTH), row_map),
        ],
        out_shape=[
            jax.ShapeDtypeStruct((b, 2, length, FOURIER_WIDTH), BF16),
            jax.ShapeDtypeStruct((b, length, ATTN_WIDTH), BF16),
            jax.ShapeDtypeStruct((b, length, KV_WIDTH), BF16),
            jax.ShapeDtypeStruct((b, length, KV_WIDTH), BF16),
        ],
        compiler_params=_cparams("parallel", "parallel"),
        name="mix_in_rope" if rope else "mix_in_ctx",
    )(x, mods, g, w_in, bcs, cos_t, sin_t)


def _dft_kernel(c_ref, s_ref, u_ref, v_ref, o_ref, acc_ref):
    k = pl.program_id(2)

    @pl.when(k == 0)
    def _():
        acc_ref[...] = jnp.zeros_like(acc_ref)

    acc_ref[...] += (jnp.dot(c_ref[...], u_ref[0, 0], preferred_element_type=F32)
                     + jnp.dot(s_ref[...], v_ref[0, 0], preferred_element_type=F32))

    @pl.when(k == pl.num_programs(2) - 1)
    def _():
        o_ref[0] = acc_ref[...].astype(BF16)


def _time_dft(cmat, smat, uv):
    b, _, t, w = uv.shape
    tm = min(DFT_ROW_TILE, t)
    tk = min(DFT_K_TILE, t)
    mat_spec = pl.BlockSpec((tm, tk), lambda i, r, k: (r, k))
    return pl.pallas_call(
        _dft_kernel,
        grid=(b, t // tm, t // tk),
        in_specs=[
            mat_spec, mat_spec,
            pl.BlockSpec((1, 1, tk, w), lambda i, r, k: (i, 0, k, 0)),
            pl.BlockSpec((1, 1, tk, w), lambda i, r, k: (i, 1, k, 0)),
        ],
        out_specs=pl.BlockSpec((1, tm, w), lambda i, r, k: (i, r, 0)),
        out_shape=jax.ShapeDtypeStruct((b, t, w), BF16),
        scratch_shapes=[pltpu.VMEM((tm, w), F32)],
        compiler_params=_cparams("parallel", "parallel", "arbitrary"),
        name="time_dft",
    )(cmat, smat, uv, uv)


def _dft_half_kernel(c_ref, s_ref, u_ref, v_ref, lo_ref, mir_ref, p_acc, q_acc):
    k = pl.program_id(2)

    @pl.when(k == 0)
    def _():
        p_acc[...] = jnp.zeros_like(p_acc)
        q_acc[...] = jnp.zeros_like(q_acc)

    p_acc[...] += jnp.dot(c_ref[...], u_ref[0, 0], preferred_element_type=F32)
    q_acc[...] += jnp.dot(s_ref[...], v_ref[0, 0], preferred_element_type=F32)

    @pl.when(k == pl.num_programs(2) - 1)
    def _():
        lo_ref[0] = (p_acc[...] + q_acc[...]).astype(BF16)
        mir_ref[0] = (p_acc[...] - q_acc[...]).astype(BF16)


def _dft_mirror_kernel(lo_ref, mir_ref, o_ref):
    i = pl.program_id(1)
    half = o_ref.shape[1]

    @pl.when(i == 0)
    def _():
        o_ref[0] = lo_ref[0, :half, :]

    @pl.when(i == 1)
    def _():
        r = lax.broadcasted_iota(jnp.int32, (BLOCK, BLOCK), 0)
        c = lax.broadcasted_iota(jnp.int32, (BLOCK, BLOCK), 1)
        rev = jnp.where(c == BLOCK - r, 1.0, 0.0).astype(BF16)
        first = jnp.where((r == 0) & (c == 0), 1.0, 0.0).astype(BF16)
        for a in range(half // BLOCK):
            top = half - a * BLOCK
            z = jnp.dot(rev, mir_ref[0, top - BLOCK:top, :], preferred_element_type=F32)
            z += jnp.dot(first, mir_ref[0, top:top + BLOCK, :], preferred_element_type=F32)
            o_ref[0, a * BLOCK:(a + 1) * BLOCK, :] = z.astype(BF16)


def _time_dft_symmetric(cmat, smat, uv):
    b, _, t, w = uv.shape
    rows = cmat.shape[0]
    assert rows == t // 2 + BLOCK and rows % (2 * SUBLANES) == 0
    tm = rows // 4
    tk = min(DFT_K_TILE, t)
    mat_spec = pl.BlockSpec((tm, tk), lambda i, r, k: (r, k))
    out_spec = pl.BlockSpec((1, tm, w), lambda i, r, k: (i, r, 0))
    lo, mir = pl.pallas_call(
        _dft_half_kernel,
        grid=(b, rows // tm, t // tk),
        in_specs=[
            mat_spec, mat_spec,
            pl.BlockSpec((1, 1, tk, w), lambda i, r, k: (i, 0, k, 0)),
            pl.BlockSpec((1, 1, tk, w), lambda i, r, k: (i, 1, k, 0)),
        ],
        out_specs=[out_spec, out_spec],
        out_shape=[jax.ShapeDtypeStruct((b, rows, w), BF16)] * 2,
        scratch_shapes=[pltpu.VMEM((tm, w), F32), pltpu.VMEM((tm, w), F32)],
        compiler_params=_cparams("parallel", "parallel", "arbitrary"),
        name="time_dft_half",
    )(cmat, smat, uv, uv)
    whole = pl.BlockSpec((1, rows, w), lambda i, n: (i, 0, 0))
    return pl.pallas_call(
        _dft_mirror_kernel,
        grid=(b, 2),
        in_specs=[whole, whole],
        out_specs=pl.BlockSpec((1, t // 2, w), lambda i, n: (i, n, 0)),
        out_shape=jax.ShapeDtypeStruct((b, t, w), BF16),
        compiler_params=_cparams("parallel", "parallel"),
        name="time_dft_mirror",
    )(lo, mir)


def _attn_kernel(sink_ref, q_ref, *refs, window):
    if window:
        kp_ref, km_ref, kn_ref, vp_ref, vm_ref, vn_ref, kx_ref, vx_ref, o_ref, s_scr, p_scr = refs
        k_pieces = [kx_ref[0], kp_ref[0], km_ref[0, :BLOCK, :], km_ref[0, BLOCK:, :], kn_ref[0]]
        v_pieces = [vx_ref[0], vp_ref[0], vm_ref[0, :BLOCK, :], vm_ref[0, BLOCK:, :], vn_ref[0]]
        subs = ((0, 1, 2, 3), (0, 2, 3, 4))
    else:
        kx_ref, vx_ref, o_ref, s_scr, p_scr = refs
        k_pieces, v_pieces = [kx_ref[0]], [vx_ref[0]]
        subs = ((0,),)
    n_sub = len(subs)
    n_q = q_ref.shape[1] // n_sub

    def split_heads(x):
        low = lax.broadcasted_iota(jnp.int32, x.shape, 1) < HEAD_DIM
        zero = jnp.zeros_like(x)
        swap = pltpu.roll(x.astype(F32), HEAD_DIM, 1).astype(BF16)
        return ((jnp.where(low, x, zero), jnp.where(low, swap, zero)),
                (jnp.where(low, zero, swap), jnp.where(low, zero, x)))

    k_ext = [split_heads(x) for x in k_pieces]
    v_ext = [split_heads(x) for x in v_pieces]
    cache = {}

    def operand(name, ext, sub, half, g):
        key = (name, sub, half, g)
        if key not in cache:
            parts = [ext[i][half][g] for i in subs[sub]]
            cache[key] = parts[0] if len(parts) == 1 else jnp.concatenate(parts, axis=0)
        return cache[key]

    if window:
        step = pl.program_id(1)
        last = pl.num_programs(1) - 1
        n_ctx = kx_ref.shape[1]
        row = lax.broadcasted_iota(jnp.int32, (n_q, BLOCK), 0)
        col = lax.broadcasted_iota(jnp.int32, (n_q, BLOCK), 1)
        ok_prev = (col >= row + jnp.where(step > 0, 0, BLOCK), col >= row)
        ok_next = (col <= row, col <= row - jnp.where(step < last, 0, BLOCK))

    q_per_kv = N_Q_HEADS // N_KV_HEADS
    n_pairs = ATTN_WIDTH // LANES
    for sub in range(n_sub):
        rows = slice(sub * n_q, (sub + 1) * n_q)
        for j in range(n_pairs):
            g = (2 * j) // q_per_kv
            qp = q_ref[0, rows, j * LANES:(j + 1) * LANES]
            for half in range(2):
                s = lax.dot_general(qp, operand("k", k_ext, sub, half, g), (((1,), (1,)), ((), ())),
                                    preferred_element_type=F32)
                item = sub * N_Q_HEADS + 2 * j + half
                if window:
                    s_scr[item, :, :n_ctx] = s[:, :n_ctx]
                    s_scr[item, :, n_ctx:n_ctx + BLOCK] = jnp.where(
                        ok_prev[sub], s[:, n_ctx:n_ctx + BLOCK], NEG_INF)
                    s_scr[item, :, n_ctx + BLOCK:n_ctx + 2 * BLOCK] = s[:, n_ctx + BLOCK:n_ctx + 2 * BLOCK]
                    s_scr[item, :, n_ctx + 2 * BLOCK:] = jnp.where(
                        ok_next[sub], s[:, n_ctx + 2 * BLOCK:], NEG_INF)
                else:
                    s_scr[item] = s
    inv = []
    n_keys = s_scr.shape[2]
    col_blocks = [slice(c, c + LANES) for c in range(0, n_keys, LANES)]
    for item in range(n_sub * N_Q_HEADS):
        sink = sink_ref[item % N_Q_HEADS] * LOG2_E
        mx = s_scr[item, :, col_blocks[0]]
        for cb in col_blocks[1:]:
            mx = jnp.maximum(mx, s_scr[item, :, cb])
        mx = jnp.maximum(jnp.max(mx, axis=1, keepdims=True), sink)
        total = None
        for cb in col_blocks:
            p = jnp.exp2(s_scr[item, :, cb] - mx)
            total = p if total is None else total + p
            p_scr[item, :, cb] = p.astype(BF16)
        inv.append(1.0 / (jnp.sum(total, axis=1, keepdims=True) + jnp.exp2(sink - mx)))
    for sub in range(n_sub):
        rows = slice(sub * n_q, (sub + 1) * n_q)
        for j in range(n_pairs):
            g = (2 * j) // q_per_kv
            a = sub * N_Q_HEADS + 2 * j
            out = jnp.dot(p_scr[a], operand("v", v_ext, sub, 0, g), preferred_element_type=F32) * inv[a]
            out += jnp.dot(p_scr[a + 1], operand("v", v_ext, sub, 1, g), preferred_element_type=F32) * inv[a + 1]
            o_ref[0, rows, j * LANES:(j + 1) * LANES] = out.astype(BF16)


def _attn_scratch(n_items, n_q, n_keys):
    return [pltpu.VMEM((n_items, n_q, n_keys), F32), pltpu.VMEM((n_items, n_q, n_keys), BF16)]


def _window_attention(sink, q, k, v, kx, vx):
    b, length, _ = q.shape
    nb = length // BLOCK
    n_ctx = kx.shape[1]
    assert nb % 2 == 0
    one = lambda f: pl.BlockSpec((1, BLOCK, KV_WIDTH), f)
    two = pl.BlockSpec((1, 2 * BLOCK, KV_WIDTH), lambda i, m: (i, m, 0))
    prev = lambda i, m: (i, jnp.maximum(2 * m - 1, 0), 0)
    nxt = lambda i, m: (i, jnp.minimum(2 * m + 2, nb - 1), 0)
    ctx_spec = pl.BlockSpec((1, n_ctx, KV_WIDTH), lambda i, m: (i, 0, 0))
    q_spec = pl.BlockSpec((1, 2 * BLOCK, ATTN_WIDTH), lambda i, m: (i, m, 0))
    return pl.pallas_call(
        functools.partial(_attn_kernel, window=True),
        grid=(b, nb // 2),
        in_specs=[
            pl.BlockSpec(memory_space=pltpu.SMEM),
            q_spec,
            one(prev), two, one(nxt),
            one(prev), two, one(nxt),
            ctx_spec, ctx_spec,
        ],
        out_specs=q_spec,
        out_shape=jax.ShapeDtypeStruct((b, length, ATTN_WIDTH), BF16),
        scratch_shapes=_attn_scratch(2 * N_Q_HEADS, BLOCK, n_ctx + 3 * BLOCK),
        compiler_params=_cparams("parallel", "parallel"),
        name="window_attn",
    )(sink, q, k, k, k, v, v, v, kx, vx)


def _context_attention(sink, q, kx, vx):
    b, n_ctx, _ = q.shape
    spec = lambda w: pl.BlockSpec((1, n_ctx, w), lambda i: (i, 0, 0))
    return pl.pallas_call(
        functools.partial(_attn_kernel, window=False),
        grid=(b,),
        in_specs=[pl.BlockSpec(memory_space=pltpu.SMEM), spec(ATTN_WIDTH), spec(KV_WIDTH), spec(KV_WIDTH)],
        out_specs=spec(ATTN_WIDTH),
        out_shape=jax.ShapeDtypeStruct((b, n_ctx, ATTN_WIDTH), BF16),
        scratch_shapes=_attn_scratch(N_Q_HEADS, n_ctx, n_ctx),
        compiler_params=_cparams("parallel"),
        name="context_attn",
    )(sink, q, kx, vx)


def _swiglu_residual(h1, m, g2, wg_ref, wu_ref, wd_ref, acc_ref):
    n2 = _norm_mod(h1, g2, m[4:5], m[3:4]).astype(BF16)
    for c in range(wg_ref.shape[1] // FFN_CHUNK):
        cols = slice(c * FFN_CHUNK, (c + 1) * FFN_CHUNK)
        gate = jnp.dot(n2, wg_ref[:, cols], preferred_element_type=F32)
        up = jnp.dot(n2, wu_ref[:, cols], preferred_element_type=F32)
        act = (gate * _sigmoid(gate)) * up
        part = jnp.dot(act.astype(BF16), wd_ref[cols, :], preferred_element_type=F32)
        if c == 0:
            acc_ref[...] = part
        else:
            acc_ref[...] += part
    return h1 + m[5:6] * acc_ref[...]


def _post0_kernel(h_ref, mod_ref, nmod_ref, g2_ref, gn_ref, f_ref, a_ref, wo_ref, wg_ref, wu_ref, wd_ref,
                  o_ref, u_ref, acc_ref):
    m = mod_ref[0]
    for c in range(D_MODEL // FFN_CHUNK):
        cols = slice(c * FFN_CHUNK, (c + 1) * FFN_CHUNK)
        mix = jnp.dot(f_ref[0], wo_ref[:FOURIER_WIDTH, cols], preferred_element_type=F32)
        mix += jnp.dot(a_ref[0], wo_ref[FOURIER_WIDTH:, cols], preferred_element_type=F32)
        o_ref[0, :, cols] = h_ref[0, :, cols] + m[2:3, cols] * mix
    h1 = o_ref[0]
    h2 = _swiglu_residual(h1, m, g2_ref[...], wg_ref, wu_ref, wd_ref, acc_ref)
    o_ref[0] = h2
    nm = nmod_ref[0]
    u_ref[0] = _norm_mod(h2, gn_ref[...], nm[1:2], nm[0:1]).astype(BF16)


def _gelu_tanh(x):
    sqrt_2_over_pi = np.sqrt(2 / np.pi).astype(np.float32)
    cdf = 0.5 * (1.0 + jnp.tanh(sqrt_2_over_pi * (x + 0.044715 * (x ** 3))))
    return x * cdf


def _post1_kernel(h_ref, mod_ref, g1_ref, g2_ref, gf_ref, dskip_ref, yf_ref, yb_ref, wglu_ref,
                  wg_ref, wu_ref, wd_ref, o_ref, acc_ref):
    m = mod_ref[0]
    h = h_ref[0]
    u = _norm_mod(h, g1_ref[...], m[1:2], m[0:1])
    y = dskip_ref[...] * u + yf_ref[0] + yb_ref[0]
    ge = _gelu_tanh(y).astype(BF16)
    for c in range(D_MODEL // FFN_CHUNK):
        cols = slice(c * FFN_CHUNK, (c + 1) * FFN_CHUNK)
        gcols = slice(D_MODEL + c * FFN_CHUNK, D_MODEL + (c + 1) * FFN_CHUNK)
        val = jnp.dot(ge, wglu_ref[:, cols], preferred_element_type=F32)
        gate = jnp.dot(ge, wglu_ref[:, gcols], preferred_element_type=F32)
        o_ref[0, :, cols] = h[:, cols] + m[2:3, cols] * (val * _sigmoid(gate))
    h1 = o_ref[0]
    h2 = _swiglu_residual(h1, m, g2_ref[...], wg_ref, wu_ref, wd_ref, acc_ref)
    o_ref[0] = (h2 * lax.rsqrt(jnp.mean(h2 * h2, axis=-1, keepdims=True) + RMS_EPS)) * gf_ref[...]


def _resident(shape):
    zeros = (0,) * len(shape)
    return pl.BlockSpec(shape, lambda i, t: zeros, pipeline_mode=pl.Buffered(1))


def _post_call(kernel, name, h, mods, row_inputs, vec_inputs, weights, rows, out_dtypes):
    b, length, d = h.shape
    row_map = lambda i, t: (i, t, 0)
    mod_spec = lambda m: pl.BlockSpec((1, N_MOD, d), lambda i, t: (i if m.shape[0] > 1 else 0, 0, 0))
    in_specs = [pl.BlockSpec((1, rows, d), row_map)]
    in_specs += [mod_spec(m) for m in mods]
    in_specs += [pl.BlockSpec((1, d), lambda i, t: (0, 0)) for _ in vec_inputs]
    in_specs += [pl.BlockSpec((1, rows, a.shape[-1]), row_map) for a in row_inputs]
    in_specs += [_resident(w.shape) for w in weights]
    return pl.pallas_call(
        kernel,
        grid=(b, length // rows),
        in_specs=in_specs,
        out_specs=[pl.BlockSpec((1, rows, d), row_map) for _ in out_dtypes],
        out_shape=[jax.ShapeDtypeStruct((b, length, d), dt) for dt in out_dtypes],
        scratch_shapes=[pltpu.VMEM((rows, d), F32)],
        compiler_params=_cparams("parallel", "parallel"),
        name=name,
    )(h, *mods, *vec_inputs, *row_inputs, *weights)


def _disc_kernel(are_ref, aim_ref, ldt_ref, bre_ref, bim_ref, oar_ref, oai_ref, obr_ref, obi_ref):
    a_re, a_im = are_ref[0], aim_ref[0]
    dt = jnp.exp(ldt_ref[0])
    mag = jnp.exp(a_re * dt)
    abar_re, abar_im = mag * jnp.cos(a_im * dt), mag * jnp.sin(a_im * dt)
    nr, ni = abar_re - 1.0, abar_im
    den = a_re * a_re + a_im * a_im
    f_re = (nr * a_re + ni * a_im) / den
    f_im = (ni * a_re - nr * a_im) / den
    oar_ref[0] = abar_re
    oai_ref[0] = abar_im
    obr_ref[0] = f_re[None] * bre_ref[0] - f_im[None] * bim_ref[0]
    obi_ref[0] = f_re[None] * bim_ref[0] + f_im[None] * bre_ref[0]


def _discretize(a_re, a_im, log_dt, b_re, b_im):
    nd = a_re.shape[0]
    rows = SSM_WIDTH // LANES
    flat = lambda a: a.reshape(nd, rows, LANES)
    ldt = jnp.broadcast_to(log_dt[:, :, None], (nd, SSM_GROUPS, SSM_STATE))
    chan = lambda a: a.reshape(nd, SSM_WIDTH, SSM_GROUP_W).transpose(0, 2, 1).reshape(nd, SSM_GROUP_W, rows, LANES)
    aspec = pl.BlockSpec((1, rows, LANES), lambda i: (i, 0, 0))
    bspec = pl.BlockSpec((1, SSM_GROUP_W, rows, LANES), lambda i: (i, 0, 0, 0))
    ashape = jax.ShapeDtypeStruct((nd, rows, LANES), F32)
    bshape = jax.ShapeDtypeStruct((nd, SSM_GROUP_W, rows, LANES), F32)
    return pl.pallas_call(
        _disc_kernel,
        grid=(nd,),
        in_specs=[aspec, aspec, aspec, bspec, bspec],
        out_specs=[aspec, aspec, bspec, bspec],
        out_shape=[ashape, ashape, bshape, bshape],
        compiler_params=_cparams("parallel"),
        name="s5_discretize",
    )(flat(a_re), flat(a_im), flat(ldt), chan(b_re), chan(b_im))


def _s5_kernel(uc_ref, ul_ref, wd_ref, wr_ref, are_ref, aim_ref, y_ref,
               xa_re, xa_im, xb_re, xb_im, hre, him, *, reverse):
    step_idx = pl.program_id(1)
    steps = S5_CHUNK
    cols_per_slab = S5_SLAB_STATES // LANES
    seg = steps // S5_SLABS

    @pl.when(step_idx == 0)
    def _():
        hre[...] = jnp.zeros_like(hre)
        him[...] = jnp.zeros_like(him)

    def plane_of(col):
        return col // SUBLANES, col % SUBLANES

    def drive_slab(u, kk, xre, xim):
        bu = jnp.dot(u(kk * LANES, (kk + 1) * LANES), wd_ref[kk], preferred_element_type=F32)
        for c in range(cols_per_slab):
            pln, sub = plane_of(kk * cols_per_slab + c)
            xre[pln, pl.ds(sub, steps, stride=SUBLANES), :] = bu[:, c * LANES:(c + 1) * LANES]
            xim[pln, pl.ds(sub, steps, stride=SUBLANES), :] = (
                bu[:, S5_SLAB_STATES + c * LANES:S5_SLAB_STATES + (c + 1) * LANES])

    def read_slab(kk, xre, xim):
        parts = []
        for src in (xre, xim):
            for c in range(cols_per_slab):
                pln, sub = plane_of(kk * cols_per_slab + c)
                parts.append(src[pln, pl.ds(sub, steps, stride=SUBLANES), :].astype(BF16))
        return jnp.dot(jnp.concatenate(parts, axis=1), wr_ref[kk], preferred_element_type=F32)

    a_r = [are_ref[p] for p in range(S5_PLANES)]
    a_i = [aim_ref[p] for p in range(S5_PLANES)]

    def one_step(t, h, xre, xim):
        h_r, h_i = h
        r0 = t * SUBLANES if isinstance(t, int) else pl.multiple_of(t * SUBLANES, SUBLANES)
        new_r, new_i = [], []
        for p in range(S5_PLANES):
            n_r = a_r[p] * h_r[p] - a_i[p] * h_i[p] + xre[p, pl.ds(r0, SUBLANES), :]
            n_i = a_r[p] * h_i[p] + a_i[p] * h_r[p] + xim[p, pl.ds(r0, SUBLANES), :]
            xre[p, pl.ds(r0, SUBLANES), :] = n_r
            xim[p, pl.ds(r0, SUBLANES), :] = n_i
            new_r.append(n_r)
            new_i.append(n_i)
        return tuple(new_r), tuple(new_i)

    def time_of(i):
        return (steps - 1 - i) if reverse else i

    def scan_segment(k, h, xre, xim):
        for i in range(k * seg, (k + 1) * seg):
            h = one_step(time_of(i), h, xre, xim)
        return h

    def load_state():
        return (tuple(hre[p] for p in range(S5_PLANES)), tuple(him[p] for p in range(S5_PLANES)))

    def store_state(h):
        for p in range(S5_PLANES):
            hre[p] = h[0][p]
            him[p] = h[1][p]

    @pl.when(step_idx == 0)
    def _():
        u = lambda lo, hi: uc_ref[0, :, lo:hi]
        for kk in range(S5_SLABS):
            drive_slab(u, kk, xa_re, xa_im)
        h = lax.fori_loop(0, steps, lambda i, h: one_step(time_of(i), h, xa_re, xa_im), load_state(), unroll=8)
        store_state(h)

    @pl.when(step_idx > 0)
    def _():
        first, second = slice(0, steps), slice(steps, 2 * steps)
        rows_a, rows_b = (second, first) if reverse else (first, second)
        u_a = lambda lo, hi: ul_ref[0, rows_a, lo:hi]
        u_b = lambda lo, hi: ul_ref[0, rows_b, lo:hi]
        for kk in range(S5_SLABS):
            drive_slab(u_a, kk, xa_re, xa_im)
        h = load_state()
        for kk in range(S5_SLABS):
            drive_slab(u_b, kk, xb_re, xb_im)
            h = scan_segment(kk, h, xa_re, xa_im)
        for kk in range(S5_SLABS):
            h = scan_segment(kk, h, xb_re, xb_im)
            y_ref[0, rows_a, kk * LANES:(kk + 1) * LANES] = read_slab(kk, xa_re, xa_im)
        store_state(h)
        for kk in range(S5_SLABS):
            y_ref[0, rows_b, kk * LANES:(kk + 1) * LANES] = read_slab(kk, xb_re, xb_im)


def _s5_direction(u_ctx, u_lat, w_drive, w_read, a_re, a_im, *, reverse):
    b, length, d = u_lat.shape
    n_ctx = u_ctx.shape[1]
    pair = 2 * S5_CHUNK
    assert n_ctx == S5_CHUNK and length % pair == 0
    n_pairs = length // pair
    if reverse:
        lat_map = lambda i, n: (i, jnp.minimum(n_pairs - n, n_pairs - 1), 0)
    else:
        lat_map = lambda i, n: (i, jnp.maximum(n - 1, 0), 0)
    whole = lambda shape: pl.BlockSpec(shape, lambda i, n: (0,) * len(shape))
    plane_buf = pltpu.VMEM((S5_PLANES, S5_CHUNK * SUBLANES, LANES), F32)
    state_buf = pltpu.VMEM((S5_PLANES, SUBLANES, LANES), F32)
    return pl.pallas_call(
        functools.partial(_s5_kernel, reverse=reverse),
        grid=(b, n_pairs + 1),
        in_specs=[
            pl.BlockSpec((1, n_ctx, d), lambda i, n: (i, 0, 0)),
            pl.BlockSpec((1, pair, d), lat_map),
            whole(w_drive.shape), whole(w_read.shape), whole(a_re.shape), whole(a_im.shape),
        ],
        out_specs=pl.BlockSpec((1, pair, d), lat_map),
        out_shape=jax.ShapeDtypeStruct((b, length, d), F32),
        scratch_shapes=[plane_buf, plane_buf, plane_buf, plane_buf, state_buf, state_buf],
        compiler_params=_cparams("parallel", "arbitrary"),
        name="s5_scan_bwd" if reverse else "s5_scan_fwd",
    )(u_ctx, u_lat, w_drive, w_read, a_re, a_im)


def _rope_tables(length):
    rows = length // GRID_W
    t_row = jnp.repeat(jnp.arange(rows, dtype=F32), GRID_W)
    t_col = jnp.tile(jnp.arange(GRID_W, dtype=F32), rows)
    n_freq = HEAD_DIM // 4
    inv_freq = ROPE_BASE ** (-jnp.arange(n_freq, dtype=F32) / n_freq)
    ang = jnp.concatenate([t_row[:, None] * inv_freq, t_col[:, None] * inv_freq], axis=-1)
    cos, sin = jnp.cos(ang), jnp.sin(ang)
    reps = LANES // HEAD_DIM
    return (jnp.tile(jnp.concatenate([cos, cos], axis=-1), (1, reps)),
            jnp.tile(jnp.concatenate([-sin, sin], axis=-1), (1, reps)))


def _half_split_perm():
    perm = np.arange(MIX_IN_WIDTH)
    head = np.concatenate([np.arange(0, HEAD_DIM, 2), np.arange(1, HEAD_DIM, 2)])
    for start in range(FOURIER_WIDTH, KV_START + KV_WIDTH, HEAD_DIM):
        perm[start:start + HEAD_DIM] = start + head
    return perm


def _channel_dft():
    c = np.arange(FOURIER_GROUP_W)
    ang = 2.0 * np.pi * np.outer(c, c) / FOURIER_GROUP_W
    eye = np.eye(FOURIER_GROUPS)
    scale = FOURIER_GROUP_W ** -0.5
    table = np.concatenate([np.kron(eye, np.cos(ang) * scale), np.kron(eye, np.sin(ang) * scale)], axis=1)
    return jnp.asarray(table, F32).astype(BF16)


def _time_dft_matrices(t, rows=None):
    r1 = 1 << ((t.bit_length() - 1 + 1) // 2)
    r2 = t // r1
    assert r1 * r2 == t and r2 % SUBLANES == 0
    rows = t if rows is None else rows
    assert rows % r2 == 0
    n = jnp.arange(t, dtype=jnp.int32)[None, :]
    unit = 2.0 * math.pi / t
    ang_a = (((jnp.arange(rows // r2, dtype=jnp.int32) * r2)[:, None] * n) % t).astype(F32) * unit
    ang_b = ((jnp.arange(r2, dtype=jnp.int32)[:, None] * n) % t).astype(F32) * unit
    scale = t ** -0.5
    ca, sa = jnp.cos(ang_a)[:, None, :], jnp.sin(ang_a)[:, None, :]
    cb, sb = (jnp.cos(ang_b) * scale)[None, :, :], (jnp.sin(ang_b) * scale)[None, :, :]
    cmat = (ca * cb - sa * sb).reshape(rows, t).astype(BF16)
    smat = (-(sa * cb + ca * sb)).reshape(rows, t).astype(BF16)
    return cmat, smat


def _ffn_weights(w_gate, w_up, w_down):
    return w_gate.astype(BF16), w_up.astype(BF16), w_down.astype(BF16)


def _s5_weights(bbar_re, bbar_im, c_re, c_im):
    nd = bbar_re.shape[0]
    eye = jnp.eye(S5_SLAB_GROUPS, dtype=F32)

    def drive(bb):
        bb = bb.reshape(nd, SSM_GROUP_W, S5_SLABS, S5_SLAB_GROUPS, SSM_STATE)
        w = jnp.einsum("dhkgp,gG->dkghGp", bb, eye)
        return w.reshape(nd, S5_SLABS, LANES, S5_SLAB_STATES)

    def read(cc):
        cc = cc.astype(F32).reshape(nd, S5_SLABS, S5_SLAB_GROUPS, SSM_GROUP_W, SSM_STATE)
        w = jnp.einsum("dkghp,gG->dkgpGh", cc, eye)
        return w.reshape(nd, S5_SLABS, S5_SLAB_STATES, LANES)

    w_drive = jnp.concatenate([drive(bbar_re), drive(bbar_im)], axis=-1).astype(BF16)
    w_read = jnp.concatenate([read(c_re), -read(c_im)], axis=2).astype(BF16)
    return w_drive, w_read


def kernel(x, c, ctx, c_ctx, mod_w, mod_b, norm_g, ffn_w_gate, ffn_w_up, ffn_w_down, mix_w_in, mix_w_out,
           attn_sink, ssm_a_re, ssm_a_im, ssm_log_dt, ssm_b_re, ssm_b_im, ssm_c_re, ssm_c_im, ssm_d,
           ssm_glu_w, final_g):
    bn, length, d = x.shape
    n_ctx = ctx.shape[1]
    depth = mod_w.shape[0]
    assert depth == 2 and d == D_MODEL

    cond = jnp.zeros((SUBLANES, d), F32).at[:bn].set(c).at[bn].set(c_ctx)
    mods = _modulations(cond, mod_w, mod_b).reshape(depth, SUBLANES, N_MOD, d)
    mods_lat, mods_ctx = mods[:, :bn], mods[:, bn:bn + 1]
    vec = lambda a: a.reshape(1, d)

    w_in = mix_w_in[0][:, _half_split_perm()].astype(BF16)
    w_out = mix_w_out[0].astype(BF16)
    bcs = _channel_dft()
    cos_t, sin_t = _rope_tables(length)
    ffn0 = _ffn_weights(ffn_w_gate[0], ffn_w_up[0], ffn_w_down[0])
    g_mix, g_ffn = vec(norm_g[0, 0]), vec(norm_g[0, 1])

    uv, q, k, v = _pre0(x, mods_lat[0], g_mix, w_in, bcs, cos_t, sin_t, rope=True, rows=TOKEN_TILE)
    uvc, qc, kc, vc = _pre0(ctx, mods_ctx[0], g_mix, w_in, bcs, cos_t, sin_t, rope=False, rows=n_ctx)
    four = _time_dft_symmetric(*_time_dft_matrices(length, length // 2 + BLOCK), uv)
    four_c = _time_dft(*_time_dft_matrices(n_ctx), uvc)
    att = _window_attention(attn_sink[0], q, k, v, kc, vc)
    att_c = _context_attention(attn_sink[0], qc, kc, vc)
    g_mix1 = vec(norm_g[1, 0])
    h, u_lat = _post_call(_post0_kernel, "mix_out_ffn", x, [mods_lat[0], mods_lat[1]], [four, att],
                          [g_ffn, g_mix1], [w_out, *ffn0], TOKEN_TILE, [F32, BF16])
    _, u_ctx = _post_call(_post0_kernel, "mix_out_ffn_ctx", ctx, [mods_ctx[0], mods_ctx[1]], [four_c, att_c],
                          [g_ffn, g_mix1], [w_out, *ffn0], n_ctx, [F32, BF16])

    g_mix, g_ffn = g_mix1, vec(norm_g[1, 1])
    abar_re, abar_im, bbar_re, bbar_im = _discretize(
        ssm_a_re[0], ssm_a_im[0], ssm_log_dt[0], ssm_b_re[0], ssm_b_im[0])
    w_drive, w_read = _s5_weights(bbar_re, bbar_im, ssm_c_re[0], ssm_c_im[0])
    planes = lambda a: a.reshape(2, S5_PLANES, SUBLANES, LANES)
    abar_re, abar_im = planes(abar_re), planes(abar_im)
    y_dir = [_s5_direction(u_ctx, u_lat, w_drive[i], w_read[i], abar_re[i], abar_im[i], reverse=bool(i))
             for i in range(2)]
    ffn1 = _ffn_weights(ffn_w_gate[1], ffn_w_up[1], ffn_w_down[1])
    return _post_call(_post1_kernel, "s5_glu_ffn_norm", h, [mods_lat[1]], y_dir,
                      [g_mix, g_ffn, vec(final_g), vec(ssm_d[0])],
                      [ssm_glu_w[0].astype(BF16), *ffn1], TOKEN_TILE, [F32])[0]
```
